```python
import math
import jax, jax.numpy as jnp
from jax import lax
import numpy as np

D_MODEL = 1024
BATCH = 16
SEQ = 2048
DEPTH = 1

MIX_WIDTH = D_MODEL
GDN_HEAD_DIM = 128
GDN_WIDTH = MIX_WIDTH // 2
GDN_HEADS = GDN_WIDTH // GDN_HEAD_DIM
GDN_CONV = 4
GDN_CHUNK = 64
MOBA_HEAD_DIM = 128
MOBA_WIDTH = MIX_WIDTH - GDN_WIDTH
MOBA_HEADS = MOBA_WIDTH // MOBA_HEAD_DIM
MOBA_BLOCK = 256
MOBA_TOPK = 3
MOBA_Q_CHUNK = 16
ROPE_THETA = 500000.0
ROPE_DIMS = MOBA_HEAD_DIM // 4
D_FF = 4 * D_MODEL
PLE_DIM = 256
RMS_EPS = 1e-6
IN_PROJ = 4 * GDN_WIDTH + 2 * GDN_HEADS + 3 * MOBA_WIDTH
IN_SPLITS = (GDN_WIDTH, 2 * GDN_WIDTH, 3 * GDN_WIDTH, 4 * GDN_WIDTH,
             4 * GDN_WIDTH + GDN_HEADS, 4 * GDN_WIDTH + 2 * GDN_HEADS,
             4 * GDN_WIDTH + 2 * GDN_HEADS + MOBA_WIDTH,
             4 * GDN_WIDTH + 2 * GDN_HEADS + 2 * MOBA_WIDTH)

kernel_name = "hybrid_gdn_moba_parallel_block"


def rms_norm(x, w):
    xf = x.astype(jnp.float32)
    y = xf * lax.rsqrt(jnp.mean(xf * xf, axis=-1, keepdims=True) + RMS_EPS)
    return (y * w.astype(jnp.float32)).astype(x.dtype)


def l2_normalize(x):
    xf = x.astype(jnp.float32)
    return xf * lax.rsqrt(jnp.sum(xf * xf, axis=-1, keepdims=True) + RMS_EPS)


def to_heads(t, n_heads):
    b, s, _ = t.shape
    return t.reshape(b, s, n_heads, -1).transpose(0, 2, 1, 3)


def from_heads(t):
    b, h, s, d = t.shape
    return t.transpose(0, 2, 1, 3).reshape(b, s, h * d)


def causal_depthwise_conv(x, w):
    k, c = w.shape
    return lax.conv_general_dilated(
        x, w[:, None, :].astype(x.dtype), window_strides=(1,), padding=[(k - 1, 0)],
        dimension_numbers=('NWC', 'WIO', 'NWC'), feature_group_count=c)


def partial_rotary(x, positions):
    half = ROPE_DIMS // 2
    inv_freq = ROPE_THETA ** (-jnp.arange(half, dtype=jnp.float32) * (2.0 / ROPE_DIMS))
    ang = positions.astype(jnp.float32)[:, None] * inv_freq[None, :]
    cos, sin = jnp.cos(ang), jnp.sin(ang)
    xr = x[..., :ROPE_DIMS].astype(jnp.float32)
    x1, x2 = xr[..., :half], xr[..., half:]
    rot = jnp.concatenate([x1 * cos - x2 * sin, x2 * cos + x1 * sin], axis=-1).astype(x.dtype)
    return jnp.concatenate([rot, x[..., ROPE_DIMS:]], axis=-1)


def chunk_gated_delta_rule(q, k, v, g, beta):
    b, h, t, dk = q.shape
    dv = v.shape[-1]
    c = GDN_CHUNK
    n = t // c
    q = q * (dk ** -0.5)
    q, k, v = (a.reshape(b, h, n, c, a.shape[-1]) for a in (q, k, v))
    g = jnp.cumsum(g.reshape(b, h, n, c), axis=-1)
    beta = beta.reshape(b, h, n, c)[..., None]
    k_beta = k * beta
    causal = jnp.tril(jnp.ones((c, c), dtype=bool))
    strict = jnp.tril(jnp.ones((c, c), dtype=bool), -1)
    decay = jnp.exp(jnp.where(causal, g[..., :, None] - g[..., None, :], -jnp.inf))
    lower = jnp.where(strict, jnp.einsum('bhncd,bhnsd->bhncs', k_beta, k) * decay, 0.0)
    eye = jnp.eye(c, dtype=q.dtype)
    rhs = jnp.concatenate([v * beta, k_beta * jnp.exp(g)[..., None]], axis=-1)
    sol = lax.linalg.triangular_solve(eye + lower, rhs, left_side=True, lower=True)
    u, w = sol[..., :dv], sol[..., dv:]
    intra = jnp.where(causal, jnp.einsum('bhncd,bhnsd->bhncs', q, k) * decay, 0.0)

    def step(state, xs):
        q_i, k_i, u_i, w_i, g_i, a_i = xs
        v_new = u_i - jnp.einsum('bhcd,bhdv->bhcv', w_i, state)
        o_i = (jnp.einsum('bhcd,bhdv->bhcv', q_i * jnp.exp(g_i)[..., None], state)
               + jnp.einsum('bhcs,bhsv->bhcv', a_i, v_new))
        g_last = g_i[..., -1:]
        state = (state * jnp.exp(g_last)[..., None]
                 + jnp.einsum('bhcd,bhcv->bhdv', k_i * jnp.exp(g_last - g_i)[..., None], v_new))
        return state, o_i

    xs = tuple(jnp.moveaxis(a, 2, 0) for a in (q, k, u, w, g, intra))
    s0 = jnp.zeros((b, h, dk, dv), q.dtype)
    _, o = lax.scan(step, s0, xs)
    return jnp.moveaxis(o, 0, 2).reshape(b, h, t, dv)


def gated_deltanet_group(gq, gk, gv, gz, gb, ga, conv_w, a_log, dt_bias, norm_w):
    b, t, _ = gq.shape
    qkv = jax.nn.silu(causal_depthwise_conv(jnp.concatenate([gq, gk, gv], axis=-1), conv_w))
    q, k, v = jnp.split(qkv, 3, axis=-1)
    q = l2_normalize(to_heads(q, GDN_HEADS))
    k = l2_normalize(to_heads(k, GDN_HEADS))
    v = to_heads(v, GDN_HEADS).astype(jnp.float32)
    beta = jax.nn.sigmoid(gb.astype(jnp.float32)).transpose(0, 2, 1)
    g = (-jnp.exp(a_log.astype(jnp.float32))
         * jax.nn.softplus(ga.astype(jnp.float32) + dt_bias.astype(jnp.float32))).transpose(0, 2, 1)
    o = chunk_gated_delta_rule(q, k, v, g, beta).transpose(0, 2, 1, 3)
    z = gz.reshape(b, t, GDN_HEADS, GDN_HEAD_DIM).astype(jnp.float32)
    o = rms_norm(o, norm_w) * jax.nn.silu(z)
    return o.reshape(b, t, GDN_WIDTH).astype(gq.dtype)


def moba_attention(q, k, v):
    b, h, t, d = q.shape
    nb = -(-t // MOBA_BLOCK)
    t_pad = nb * MOBA_BLOCK
    pad = [(0, 0), (0, 0), (0, t_pad - t), (0, 0)]
    kp, vp = jnp.pad(k, pad), jnp.pad(v, pad)
    kb = kp.reshape(b, h, nb, MOBA_BLOCK, d)
    vb = vp.reshape(b, h, nb, MOBA_BLOCK, d)
    k_mean = jnp.mean(kb.astype(jnp.float32), axis=3)
    gate = jnp.einsum('bhtd,bhnd->bhtn', q.astype(jnp.float32), k_mean)
    q_blk = jnp.arange(t) // MOBA_BLOCK
    past = jnp.arange(nb)[None, :] < q_blk[:, None]
    gate = jnp.where(past, gate, -jnp.inf)
    topk = min(MOBA_TOPK, nb)
    _, sel = lax.top_k(gate, topk)
    sel_valid = sel < q_blk[None, None, :, None]
    scale = d ** -0.5
    bi = jnp.arange(b)[:, None, None, None]
    hi = jnp.arange(h)[None, :, None, None]

    def query_chunk(ci):
        t0 = ci * MOBA_Q_CHUNK
        q_c = lax.dynamic_slice_in_dim(q, t0, MOBA_Q_CHUNK, axis=2)
        sel_c = lax.dynamic_slice_in_dim(sel, t0, MOBA_Q_CHUNK, axis=2)
        val_c = lax.dynamic_slice_in_dim(sel_valid, t0, MOBA_Q_CHUNK, axis=2)
        k_sel = kb[bi, hi, sel_c]
        v_sel = vb[bi, hi, sel_c]
        s_sel = jnp.einsum('bhqd,bhqnkd->bhqnk', q_c, k_sel).astype(jnp.float32) * scale
        s_sel = jnp.where(val_c[..., None], s_sel, -jnp.inf).reshape(b, h, MOBA_Q_CHUNK, topk * MOBA_BLOCK)
        blk_start = (t0 // MOBA_BLOCK) * MOBA_BLOCK
        k_own = lax.dynamic_slice_in_dim(kp, blk_start, MOBA_BLOCK, axis=2)
        v_own = lax.dynamic_slice_in_dim(vp, blk_start, MOBA_BLOCK, axis=2)
        s_own = jnp.einsum('bhqd,bhkd->bhqk', q_c, k_own).astype(jnp.float32) * scale
        qpos = t0 + jnp.arange(MOBA_Q_CHUNK)
        kpos = blk_start + jnp.arange(MOBA_BLOCK)
        s_own = jnp.where(kpos[None, :] <= qpos[:, None], s_own, -jnp.inf)
        prob = jax.nn.softmax(jnp.concatenate([s_sel, s_own], axis=-1), axis=-1).astype(v.dtype)
        p_sel = prob[..., :topk * MOBA_BLOCK].reshape(b, h, MOBA_Q_CHUNK, topk, MOBA_BLOCK)
        p_own = prob[..., topk * MOBA_BLOCK:]
        return (jnp.einsum('bhqnk,bhqnkd->bhqd', p_sel, v_sel)
                + jnp.einsum('bhqk,bhkd->bhqd', p_own, v_own))

    out = lax.map(query_chunk, jnp.arange(t // MOBA_Q_CHUNK))
    return jnp.moveaxis(out, 0, 2).reshape(b, h, t, d)


def moba_group(mq, mk, mv, positions):
    q = partial_rotary(to_heads(mq, MOBA_HEADS), positions)
    k = partial_rotary(to_heads(mk, MOBA_HEADS), positions)
    v = to_heads(mv, MOBA_HEADS)
    return from_heads(moba_attention(q, k, v))


def setup_inputs(seed: int = 0) -> dict:
    key = jax.random.key(seed)
    ks = jax.random.split(key, 18)
    f32 = jnp.float32

    def nrm(k, shape, scale):
        return jax.random.normal(k, shape, f32) * scale

    dt = jnp.exp(jax.random.uniform(ks[5], (DEPTH, GDN_HEADS), f32, math.log(1e-3), math.log(1e-1)))
    return {
        "x": nrm(ks[0], (BATCH, SEQ, D_MODEL), 1.0),
        "p": nrm(ks[1], (DEPTH, BATCH, SEQ, PLE_DIM), 1.0),
        "w_in": nrm(ks[2], (DEPTH, D_MODEL, IN_PROJ), D_MODEL ** -0.5),
        "conv_w": nrm(ks[3], (DEPTH, GDN_CONV, 3 * GDN_WIDTH), GDN_CONV ** -0.5),
        "a_log": jnp.log(jax.random.uniform(ks[4], (DEPTH, GDN_HEADS), f32, 1.0, 16.0)),
        "dt_bias": dt + jnp.log(-jnp.expm1(-dt)),
        "gdn_norm_w": 1.0 + nrm(ks[6], (DEPTH, GDN_HEAD_DIM), 0.02),
        "w_out": nrm(ks[7], (DEPTH, MIX_WIDTH, D_MODEL), MIX_WIDTH ** -0.5),
        "attn_pre_norm": 1.0 + nrm(ks[8], (DEPTH, D_MODEL), 0.02),
        "attn_post_norm": 1.0 + nrm(ks[9], (DEPTH, D_MODEL), 0.02),
        "mlp_pre_norm": 1.0 + nrm(ks[10], (DEPTH, D_MODEL), 0.02),
        "mlp_post_norm": 1.0 + nrm(ks[11], (DEPTH, D_MODEL), 0.02),
        "w_up": nrm(ks[12], (DEPTH, D_MODEL, D_FF), D_MODEL ** -0.5),
        "w_down": nrm(ks[13], (DEPTH, D_FF, D_MODEL), D_FF ** -0.5),
        "w_ple": nrm(ks[14], (DEPTH, PLE_DIM, D_MODEL), PLE_DIM ** -0.5),
        "w_ple_gate": nrm(ks[15], (DEPTH, D_MODEL, D_MODEL), D_MODEL ** -0.5),
    }


def reference(x, p, w_in, conv_w, a_log, dt_bias, gdn_norm_w, w_out, attn_pre_norm,
              attn_post_norm, mlp_pre_norm, mlp_post_norm, w_up, w_down, w_ple, w_ple_gate):
    t = x.shape[1]
    positions = jnp.arange(t, dtype=jnp.int32)
    h = x
    for i in range(DEPTH):
        u = rms_norm(h, attn_pre_norm[i])
        proj = u @ w_in[i]
        gq, gk, gv, gz, gb, ga, mq, mk, mv = jnp.split(proj, IN_SPLITS, axis=-1)
        o_gdn = gated_deltanet_group(gq, gk, gv, gz, gb, ga, conv_w[i], a_log[i], dt_bias[i], gdn_norm_w[i])
        o_moba = moba_group(mq, mk, mv, positions)
        mix = jnp.concatenate([o_gdn, o_moba], axis=-1) @ w_out[i]
        h = h + rms_norm(mix, attn_post_norm[i])
        f = jnp.square(jax.nn.relu(rms_norm(h, mlp_pre_norm[i]) @ w_up[i])) @ w_down[i]
        h = h + rms_norm(f, mlp_post_norm[i])
        h = h + jax.nn.sigmoid(h @ w_ple_gate[i]) * (p[i] @ w_ple[i])
    return h
```

```python
import functools

import jax
import jax.numpy as jnp
from jax import lax
from jax.experimental import pallas as pl
from jax.experimental.pallas import tpu as pltpu

F32 = jnp.float32
BF16 = jnp.bfloat16

D_MODEL = 1024
HEAD_DIM = 128
GDN_HEADS = 4
GDN_WIDTH = GDN_HEADS * HEAD_DIM
GDN_CONV = 4
GDN_CHUNK = 64
MOBA_HEADS = 4
MOBA_WIDTH = MOBA_HEADS * HEAD_DIM
MOBA_BLOCK = 256
MOBA_TOPK = 3
ROPE_DIMS = 32
ROPE_THETA = 500000.0
D_FF = 4 * D_MODEL
PLE_DIM = 256
RMS_EPS = 1e-6
LANES = 128
CONV_HALO = 8
QK_SCALE = HEAD_DIM ** -0.5
VMEM_LIMIT = 56 * 1024 * 1024

C_GQKV = 0
C_GZ = 3 * GDN_WIDTH
C_MQ = 4 * GDN_WIDTH
C_MK = C_MQ + MOBA_WIDTH
C_MV = C_MK + MOBA_WIDTH
C_BA = C_MV + MOBA_WIDTH
C_END = C_BA + LANES


def _rms(x, w):
    return x * lax.rsqrt(jnp.mean(x * x, axis=-1, keepdims=True) + RMS_EPS) * w


def _sigmoid(x):
    return 1.0 / (1.0 + jnp.exp(-x))


def _dot(a, b):
    return jnp.dot(a.astype(BF16), b.astype(BF16), preferred_element_type=F32)


def _dot_nt(a, b):
    return lax.dot_general(a.astype(BF16), b.astype(BF16), (((1,), (1,)), ((), ())),
                           preferred_element_type=F32)


def _dot_tn(a, b):
    return lax.dot_general(a.astype(BF16), b.astype(BF16), (((0,), (0,)), ((), ())),
                           preferred_element_type=F32)


def _in_proj_body(x_ref, nw_ref, w_ref, cw_ref, gp_ref, cos_ref, sa_ref, sb_ref,
                  gq_ref, gk_ref, gv_ref, gz_ref, bg_ref, mq_ref, mk_ref, mv_ref,
                  cbuf, *, tm):
    i = pl.program_id(1)
    u = _rms(x_ref[...], nw_ref[...]).astype(BF16)

    def proj(c0, width):
        return jnp.dot(u, w_ref[:, c0:c0 + width], preferred_element_type=F32)

    @pl.when(i == 0)
    def _():
        cbuf[0:CONV_HALO, :] = jnp.zeros((CONV_HALO, 3 * GDN_WIDTH), F32)

    @pl.when(i > 0)
    def _():
        cbuf[0:CONV_HALO, :] = cbuf[tm:tm + CONV_HALO, :]

    cbuf[CONV_HALO:CONV_HALO + tm, :] = proj(C_GQKV, 3 * GDN_WIDTH)

    outs = (gq_ref, gk_ref, gv_ref)
    for s in range(3 * GDN_HEADS):
        c0 = s * HEAD_DIM
        acc = None
        for j in range(GDN_CONV):
            r0 = CONV_HALO - (GDN_CONV - 1) + j
            term = cw_ref[j:j + 1, c0:c0 + HEAD_DIM] * cbuf[r0:r0 + tm, c0:c0 + HEAD_DIM]
            acc = term if acc is None else acc + term
        y = acc * _sigmoid(acc)
        which, h = divmod(s, GDN_HEADS)
        if which < 2:
            y = y * lax.rsqrt(jnp.sum(y * y, axis=-1, keepdims=True) + RMS_EPS)
            if which == 0:
                y = y * QK_SCALE
        outs[which][:, h * HEAD_DIM:(h + 1) * HEAD_DIM] = y

    gz_ref[...] = proj(C_GZ, GDN_WIDTH)

    ba = proj(C_BA, LANES)
    lane = lax.broadcasted_iota(jnp.int32, (tm, LANES), 1)
    xg = ba + gp_ref[1:2, :]
    softplus = jnp.maximum(xg, 0.0) + jnp.log1p(jnp.exp(-jnp.abs(xg)))
    g = -jnp.exp(gp_ref[0:1, :]) * softplus
    bg_ref[...] = jnp.where(lane < GDN_HEADS, _sigmoid(ba),
                            jnp.where(lane < 2 * GDN_HEADS, g, 0.0))

    cos, sa, sb = cos_ref[...], sa_ref[...], sb_ref[...]
    half = ROPE_DIMS // 2
    for ref, c_base, scale in ((mq_ref, C_MQ, QK_SCALE), (mk_ref, C_MK, None)):
        for h in range(MOBA_HEADS):
            xr = proj(c_base + h * HEAD_DIM, HEAD_DIM)
            rot = (xr * cos + pltpu.roll(xr, LANES - half, 1) * sa
                   + pltpu.roll(xr, half, 1) * sb)
            if scale is not None:
                rot = rot * scale
            ref[:, h * HEAD_DIM:(h + 1) * HEAD_DIM] = rot
    mv_ref[...] = proj(C_MV, MOBA_WIDTH)


def _in_proj(x, nw, w_all, conv_w, gparams, cos_t, sa_t, sb_t, *, tm):
    B, T, D = x.shape
    tok = lambda width: pl.BlockSpec((None, tm, width), lambda b, i: (b, i, 0))
    const = lambda shape: pl.BlockSpec(shape, lambda b, i: (0,) * len(shape))
    table = pl.BlockSpec((tm, LANES), lambda b, i: (i, 0))
    o512 = jax.ShapeDtypeStruct((B, T, GDN_WIDTH), F32)
    return pl.pallas_call(
        functools.partial(_in_proj_body, tm=tm),
        grid=(B, T // tm),
        in_specs=[tok(D), const((1, D)), const((D, C_END)), const((GDN_CONV, 3 * GDN_WIDTH)),
                  const((2, LANES)), table, table, table],
        out_specs=[tok(GDN_WIDTH)] * 4 + [tok(LANES)] + [tok(MOBA_WIDTH)] * 3,
        out_shape=[o512] * 4 + [jax.ShapeDtypeStruct((B, T, LANES), F32)] + [o512] * 3,
        scratch_shapes=[pltpu.VMEM((tm + CONV_HALO, 3 * GDN_WIDTH), F32)],
        compiler_params=pltpu.CompilerParams(
            dimension_semantics=("arbitrary", "arbitrary"), vmem_limit_bytes=VMEM_LIMIT),
        name="in_proj",
    )(x, nw, w_all, conv_w, gparams, cos_t, sa_t, sb_t)


def _gdn_body(q_ref, k_ref, v_ref, z_ref, bg_ref, nw_ref, o_ref, s_ref, *, T):
    C = GDN_CHUNK
    ri = lax.broadcasted_iota(jnp.int32, (C, C), 0)
    ci = lax.broadcasted_iota(jnp.int32, (C, C), 1)
    causal = ci <= ri
    strict = ci < ri
    eye = (ri == ci).astype(F32)
    s_ref[...] = jnp.zeros_like(s_ref)

    def chunk(n, carry):
        r0 = pl.multiple_of(n * C, C)
        bg = bg_ref[pl.ds(r0, C), :]
        for h in range(GDN_HEADS):
            cols = slice(h * HEAD_DIM, (h + 1) * HEAD_DIM)
            q = q_ref[pl.ds(r0, C), cols]
            k = k_ref[pl.ds(r0, C), cols]
            v = v_ref[pl.ds(r0, C), cols]
            beta = bg[:, h:h + 1]
            gb = jnp.broadcast_to(bg[:, GDN_HEADS + h:GDN_HEADS + h + 1], (C, C))
            gc_row = jnp.sum(jnp.where(ri <= ci, gb, 0.0), axis=0, keepdims=True)
            g_row = jnp.sum(jnp.where(ri == ci, gb, 0.0), axis=0, keepdims=True)
            gc_col = jnp.sum(jnp.where(causal, jnp.broadcast_to(g_row, (C, C)), 0.0),
                             axis=1, keepdims=True)
            gc_last = gc_row[:, C - 1:C]
            decay = jnp.exp(jnp.where(causal, gc_col - gc_row, -jnp.inf))

            kb = k * beta
            low = jnp.where(strict, _dot_nt(kb, k) * decay, 0.0)
            intra = jnp.where(causal, _dot_nt(q, k) * decay, 0.0)
            inv = eye - low
            pw = low
            for _ in range(5):
                pw = _dot(pw, pw)
                inv = inv + _dot(inv, pw)
            u = _dot(inv, v * beta)
            w = _dot(inv, kb * jnp.exp(gc_col))

            state = s_ref[h]
            v_new = u - _dot(w, state)
            o = _dot(q * jnp.exp(gc_col), state) + _dot(intra, v_new)
            s_ref[h] = state * jnp.exp(gc_last) + _dot_tn(k * jnp.exp(gc_last - gc_col), v_new)

            z = z_ref[pl.ds(r0, C), cols]
            o_ref[pl.ds(r0, C), cols] = _rms(o, nw_ref[...]) * (z * _sigmoid(z))
        return carry

    lax.fori_loop(0, T // C, chunk, 0)


def _gdn(gq, gk, gv, gz, bg, nw):
    B, T, _ = gq.shape
    tok = lambda width: pl.BlockSpec((None, T, width), lambda b: (b, 0, 0))
    return pl.pallas_call(
        functools.partial(_gdn_body, T=T),
        grid=(B,),
        in_specs=[tok(GDN_WIDTH)] * 4 + [tok(LANES), pl.BlockSpec((1, HEAD_DIM), lambda b: (0, 0))],
        out_specs=tok(GDN_WIDTH),
        out_shape=jax.ShapeDtypeStruct((B, T, GDN_WIDTH), F32),
        scratch_shapes=[pltpu.VMEM((GDN_HEADS, HEAD_DIM, HEAD_DIM), F32)],
        compiler_params=pltpu.CompilerParams(
            dimension_semantics=("arbitrary",), vmem_limit_bytes=VMEM_LIMIT),
        name="gdn",
    )(gq, gk, gv, gz, bg, nw)


def _moba_body(q_ref, k_ref, v_ref, o_ref, *, T):
    BS = MOBA_BLOCK
    nb = T // BS
    neg = -jnp.inf
    means = [jnp.mean(k_ref[j * BS:(j + 1) * BS, :], axis=0, keepdims=True) for j in range(nb)]
    kmean = jnp.concatenate(means + [jnp.zeros((LANES - nb, HEAD_DIM), F32)], axis=0)
    lane = lax.broadcasted_iota(jnp.int32, (BS, LANES), 1)
    ri = lax.broadcasted_iota(jnp.int32, (BS, BS), 0)
    ci = lax.broadcasted_iota(jnp.int32, (BS, BS), 1)

    for i in range(nb):
        q = q_ref[i * BS:(i + 1) * BS, :]
        sel = None
        if i > MOBA_TOPK:
            gate = lax.dot_general(q, kmean, (((1,), (1,)), ((), ())),
                                   precision=lax.Precision.HIGHEST, preferred_element_type=F32)
            gate = jnp.where(lane < i, gate, neg)
            sel = jnp.zeros((BS, LANES), F32)
            for _ in range(MOBA_TOPK):
                top = jnp.max(gate, axis=-1, keepdims=True)
                first = jnp.min(jnp.where(gate == top, lane, LANES), axis=-1, keepdims=True)
                pick = lane == first
                sel = jnp.where(pick, 1.0, sel)
                gate = jnp.where(pick, neg, gate)
        scores = []
        for j in range(i + 1):
            s = _dot_nt(q, k_ref[j * BS:(j + 1) * BS, :])
            if j == i:
                s = jnp.where(ci <= ri, s, neg)
            elif sel is not None:
                s = jnp.where(sel[:, j:j + 1] > 0.5, s, neg)
            scores.append(s)
        m = scores[0].max(axis=-1, keepdims=True)
        for s in scores[1:]:
            m = jnp.maximum(m, s.max(axis=-1, keepdims=True))
        acc = jnp.zeros((BS, HEAD_DIM), F32)
        den = jnp.zeros((BS, 1), F32)
        for j, s in enumerate(scores):
            p = jnp.exp(s - m)
            den = den + p.sum(axis=-1, keepdims=True)
            acc = acc + _dot(p, v_ref[j * BS:(j + 1) * BS, :])
        o_ref[i * BS:(i + 1) * BS, :] = acc / den


def _moba(mq, mk, mv):
    B, T, _ = mq.shape
    assert T % MOBA_BLOCK == 0 and T // MOBA_BLOCK <= LANES
    head = pl.BlockSpec((None, T, HEAD_DIM), lambda b, h: (b, 0, h))
    return pl.pallas_call(
        functools.partial(_moba_body, T=T),
        grid=(B, MOBA_HEADS),
        in_specs=[head] * 3,
        out_specs=head,
        out_shape=jax.ShapeDtypeStruct((B, T, MOBA_WIDTH), F32),
        compiler_params=pltpu.CompilerParams(
            dimension_semantics=("arbitrary", "arbitrary"), vmem_limit_bytes=VMEM_LIMIT),
        name="moba",
    )(mq, mk, mv)


def _mlp_body(og_ref, om_ref, x_ref, p_ref, wo_ref, n1_ref, n2_ref, wu_ref, wd_ref, n3_ref,
              wg_ref, wp_ref, o_ref, *, ff_chunk):
    mix = (_dot(og_ref[...], wo_ref[0:GDN_WIDTH, :])
           + _dot(om_ref[...], wo_ref[GDN_WIDTH:GDN_WIDTH + MOBA_WIDTH, :]))
    h = x_ref[...] + _rms(mix, n1_ref[...])
    a = _rms(h, n2_ref[...]).astype(BF16)
    f = None
    for c0 in range(0, D_FF, ff_chunk):
        up = jnp.dot(a, wu_ref[:, c0:c0 + ff_chunk], preferred_element_type=F32)
        r = jnp.square(jnp.maximum(up, 0.0))
        part = _dot(r, wd_ref[c0:c0 + ff_chunk, :])
        f = part if f is None else f + part
    h = h + _rms(f, n3_ref[...])
    gate = _sigmoid(_dot(h, wg_ref[...]))
    o_ref[...] = h + gate * _dot(p_ref[...], wp_ref[...])


def _mlp(og, om, x, p, wo, n1, n2, wu, wd, n3, wg, wp, *, tm):
    B, T, D = x.shape
    tok = lambda width: pl.BlockSpec((None, tm, width), lambda b, i: (b, i, 0))
    const = lambda shape: pl.BlockSpec(shape, lambda b, i: (0,) * len(shape),
                                       pipeline_mode=pl.Buffered(1))
    return pl.pallas_call(
        functools.partial(_mlp_body, ff_chunk=1024),
        grid=(B, T // tm),
        in_specs=[tok(GDN_WIDTH), tok(MOBA_WIDTH), tok(D), tok(PLE_DIM),
                  const((D, D)), const((1, D)), const((1, D)), const((D, D_FF)),
                  const((D_FF, D)), const((1, D)), const((D, D)), const((PLE_DIM, D))],
        out_specs=tok(D),
        out_shape=jax.ShapeDtypeStruct((B, T, D), F32),
        compiler_params=pltpu.CompilerParams(
            dimension_semantics=("arbitrary", "arbitrary"), vmem_limit_bytes=VMEM_LIMIT),
        name="mlp",
    )(og, om, x, p, wo, n1, n2, wu, wd, n3, wg, wp)


def _rope_tables(T):
    half = ROPE_DIMS // 2
    inv_freq = ROPE_THETA ** (-jnp.arange(half, dtype=F32) * (2.0 / ROPE_DIMS))
    ang = jnp.arange(T, dtype=jnp.int32).astype(F32)[:, None] * inv_freq[None, :]
    cos, sin = jnp.cos(ang), jnp.sin(ang)
    zeros = jnp.zeros((T, HEAD_DIM - ROPE_DIMS), F32)
    z_half = jnp.zeros((T, half), F32)
    cos_t = jnp.concatenate([cos, cos, jnp.ones((T, HEAD_DIM - ROPE_DIMS), F32)], axis=-1)
    sa_t = jnp.concatenate([-sin, z_half, zeros], axis=-1)
    sb_t = jnp.concatenate([z_half, sin, zeros], axis=-1)
    return cos_t, sa_t, sb_t


def _layer(h, p_i, w_in, conv_w, a_log, dt_bias, gdn_norm_w, w_out, attn_pre_norm,
           attn_post_norm, mlp_pre_norm, mlp_post_norm, w_up, w_down, w_ple, w_ple_gate, tables):
    B, T, D = h.shape
    tm = min(512, T)
    assert T % tm == 0 and T % MOBA_BLOCK == 0 and T % GDN_CHUNK == 0
    gw = 4 * GDN_WIDTH
    nh = 2 * GDN_HEADS
    pad = jnp.zeros((D, LANES - nh), w_in.dtype)
    w_all = jnp.concatenate([w_in[:, :gw], w_in[:, gw + nh:], w_in[:, gw:gw + nh], pad],
                            axis=-1).astype(BF16)
    gparams = jnp.zeros((2, LANES), F32)
    gparams = gparams.at[0, GDN_HEADS:nh].set(a_log.astype(F32))
    gparams = gparams.at[1, GDN_HEADS:nh].set(dt_bias.astype(F32))
    row = lambda v: v.reshape(1, -1).astype(F32)

    gq, gk, gv, gz, bg, mq, mk, mv = _in_proj(
        h, row(attn_pre_norm), w_all, conv_w.astype(F32), gparams, *tables, tm=tm)
    o_gdn = _gdn(gq, gk, gv, gz, bg, row(gdn_norm_w))
    o_moba = _moba(mq, mk, mv)
    return _mlp(o_gdn, o_moba, h, p_i, w_out.astype(BF16), row(attn_post_norm),
                row(mlp_pre_norm), w_up.astype(BF16), w_down.astype(BF16),
                row(mlp_post_norm), w_ple_gate.astype(BF16), w_ple.astype(BF16), tm=tm)


def kernel(x, p, w_in, conv_w, a_log, dt_bias, gdn_norm_w, w_out, attn_pre_norm, attn_post_norm,
           mlp_pre_norm, mlp_post_norm, w_up, w_down, w_ple, w_ple_gate):
    tables = _rope_tables(x.shape[1])
    h = x
    for i in range(w_in.shape[0]):
        h = _layer(h, p[i], w_in[i], conv_w[i], a_log[i], dt_bias[i], gdn_norm_w[i], w_out[i],
                   attn_pre_norm[i], attn_post_norm[i], mlp_pre_norm[i], mlp_post_norm[i],
                   w_up[i], w_down[i], w_ple[i], w_ple_gate[i], tables)
    return h
```

```python
import functools

import jax
import jax.numpy as jnp
from jax import lax
from jax.experimental import pallas as pl
from jax.experimental.pallas import tpu as pltpu

F32 = jnp.float32
BF16 = jnp.bfloat16

D_MODEL = 1024
HEAD_DIM = 128
GDN_HEADS = 4
GDN_WIDTH = GDN_HEADS * HEAD_DIM
GDN_CONV = 4
GDN_CHUNK = 64
GDN_GROUP = 2
MOBA_HEADS = 4
MOBA_WIDTH = MOBA_HEADS * HEAD_DIM
MOBA_BLOCK = 256
MOBA_TOPK = 3
ROPE_DIMS = 32
ROPE_THETA = 500000.0
D_FF = 4 * D_MODEL
PLE_DIM = 256
RMS_EPS = 1e-6
LANES = 128
SUBLANES = 8
CONV_HALO = 8
QK_SCALE = HEAD_DIM ** -0.5
VMEM_LIMIT = 56 * 1024 * 1024

C_GQKV = 0
C_GZ = 3 * GDN_WIDTH
C_MQ = 4 * GDN_WIDTH
C_MK = C_MQ + MOBA_WIDTH
C_MV = C_MK + MOBA_WIDTH
C_BA = C_MV + MOBA_WIDTH
C_END = C_BA + LANES


def _rms(x, w):
    return x * lax.rsqrt(jnp.mean(x * x, axis=-1, keepdims=True) + RMS_EPS) * w


def _sigmoid(x):
    return 1.0 / (1.0 + jnp.exp(-x))


def _dot(a, b):
    return jnp.dot(a.astype(BF16), b.astype(BF16), preferred_element_type=F32)


def _dot_nt(a, b):
    return lax.dot_general(a.astype(BF16), b.astype(BF16), (((1,), (1,)), ((), ())),
                           preferred_element_type=F32)


def _dot_tn(a, b):
    return lax.dot_general(a.astype(BF16), b.astype(BF16), (((0,), (0,)), ((), ())),
                           preferred_element_type=F32)


def _in_proj_body(x_ref, nw_ref, w_ref, cw_ref, gp_ref, cos_ref, sa_ref, sb_ref,
                  gq_ref, gk_ref, gv_ref, gz_ref, bg_ref, mq_ref, mk_ref, mv_ref,
                  cbuf, *, tm):
    i = pl.program_id(1)
    u = _rms(x_ref[...], nw_ref[...]).astype(BF16)

    def proj(c0, width):
        return jnp.dot(u, w_ref[:, c0:c0 + width], preferred_element_type=F32)

    @pl.when(i == 0)
    def _():
        cbuf[0:CONV_HALO, :] = jnp.zeros((CONV_HALO, 3 * GDN_WIDTH), F32)

    @pl.when(i > 0)
    def _():
        cbuf[0:CONV_HALO, :] = cbuf[tm:tm + CONV_HALO, :]

    cbuf[CONV_HALO:CONV_HALO + tm, :] = proj(C_GQKV, 3 * GDN_WIDTH)

    outs = (gq_ref, gk_ref, gv_ref)
    for s in range(3 * GDN_HEADS):
        c0 = s * HEAD_DIM
        acc = None
        for j in range(GDN_CONV):
            r0 = CONV_HALO - (GDN_CONV - 1) + j
            term = cw_ref[j:j + 1, c0:c0 + HEAD_DIM] * cbuf[r0:r0 + tm, c0:c0 + HEAD_DIM]
            acc = term if acc is None else acc + term
        y = acc * _sigmoid(acc)
        which, h = divmod(s, GDN_HEADS)
        if which < 2:
            y = y * lax.rsqrt(jnp.sum(y * y, axis=-1, keepdims=True) + RMS_EPS)
            if which == 0:
                y = y * QK_SCALE
        outs[which][:, h * HEAD_DIM:(h + 1) * HEAD_DIM] = y.astype(BF16)

    gz_ref[...] = proj(C_GZ, GDN_WIDTH).astype(BF16)

    ba = proj(C_BA, LANES)
    lane = lax.broadcasted_iota(jnp.int32, (tm, LANES), 1)
    xg = ba + gp_ref[1:2, :]
    softplus = jnp.maximum(xg, 0.0) + jnp.log1p(jnp.exp(-jnp.abs(xg)))
    g = -jnp.exp(gp_ref[0:1, :]) * softplus
    bg_ref[...] = jnp.where(lane < GDN_HEADS, _sigmoid(ba),
                            jnp.where(lane < 2 * GDN_HEADS, g, 0.0))

    cos, sa, sb = cos_ref[...], sa_ref[...], sb_ref[...]
    half = ROPE_DIMS // 2
    for ref, c_base, scale in ((mq_ref, C_MQ, QK_SCALE), (mk_ref, C_MK, None)):
        for h in range(MOBA_HEADS):
            xr = proj(c_base + h * HEAD_DIM, HEAD_DIM)
            rot = (xr * cos + pltpu.roll(xr, LANES - half, 1) * sa
                   + pltpu.roll(xr, half, 1) * sb)
            if scale is not None:
                rot = rot * scale
            ref[:, h * HEAD_DIM:(h + 1) * HEAD_DIM] = rot.astype(BF16)
    mv_ref[...] = proj(C_MV, MOBA_WIDTH).astype(BF16)


def _in_proj(x, nw, w_all, conv_w, gparams, cos_t, sa_t, sb_t, *, tm):
    B, T, D = x.shape
    tok = lambda width: pl.BlockSpec((None, tm, width), lambda b, i: (b, i, 0))
    const = lambda shape: pl.BlockSpec(shape, lambda b, i: (0,) * len(shape))
    table = pl.BlockSpec((tm, LANES), lambda b, i: (i, 0))
    o512 = jax.ShapeDtypeStruct((B, T, GDN_WIDTH), BF16)
    return pl.pallas_call(
        functools.partial(_in_proj_body, tm=tm),
        grid=(B, T // tm),
        in_specs=[tok(D), const((1, D)), const((D, C_END)), const((GDN_CONV, 3 * GDN_WIDTH)),
                  const((2, LANES)), table, table, table],
        out_specs=[tok(GDN_WIDTH)] * 4 + [tok(LANES)] + [tok(MOBA_WIDTH)] * 3,
        out_shape=[o512] * 4 + [jax.ShapeDtypeStruct((B, T, LANES), F32)] + [o512] * 3,
        scratch_shapes=[pltpu.VMEM((tm + CONV_HALO, 3 * GDN_WIDTH), F32)],
        compiler_params=pltpu.CompilerParams(
            dimension_semantics=("arbitrary", "arbitrary"), vmem_limit_bytes=VMEM_LIMIT),
        name="in_proj",
    )(x, nw, w_all, conv_w, gparams, cos_t, sa_t, sb_t)


def _gdn_body(q_ref, k_ref, v_ref, z_ref, bg_ref, nw_ref, o_ref,
              s_ref, mneg_ref, n_ref, o1_ref, o2_ref, a_ref, *, T, seg):
    C = GDN_CHUNK
    H = GDN_HEADS
    ri = lax.broadcasted_iota(jnp.int32, (C, C), 0)
    ci = lax.broadcasted_iota(jnp.int32, (C, C), 1)
    causal = ci <= ri
    strict = ci < ri
    eye = (ri == ci).astype(F32)
    cps = seg // C
    cols = [slice(h * HEAD_DIM, (h + 1) * HEAD_DIM) for h in range(H)]
    s_ref[...] = jnp.zeros_like(s_ref)

    def precompute(base, gi):
        inst = [(c, h) for c in range(GDN_GROUP) for h in range(H)]
        rng = range(len(inst))
        lcs = [gi * GDN_GROUP + c for c in range(GDN_GROUP)]
        r0s = [pl.multiple_of(base + lc * C, C) for lc in lcs]
        bgs = [bg_ref[pl.ds(r0, C), :] for r0 in r0s]
        q = [q_ref[pl.ds(r0s[c], C), cols[h]].astype(F32) for c, h in inst]
        k = [k_ref[pl.ds(r0s[c], C), cols[h]].astype(F32) for c, h in inst]
        v = [v_ref[pl.ds(r0s[c], C), cols[h]].astype(F32) for c, h in inst]
        beta = [bgs[c][:, h:h + 1] for c, h in inst]
        gc_col, gc_last, decay = [], [], []
        for c, h in inst:
            gb = jnp.broadcast_to(bgs[c][:, H + h:H + h + 1], (C, C))
            row = jnp.sum(jnp.where(ri <= ci, gb, 0.0), axis=0, keepdims=True)
            g_row = jnp.sum(jnp.where(ri == ci, gb, 0.0), axis=0, keepdims=True)
            col = jnp.sum(jnp.where(causal, jnp.broadcast_to(g_row, (C, C)), 0.0),
                          axis=1, keepdims=True)
            gc_col.append(col)
            gc_last.append(row[:, C - 1:C])
            decay.append(jnp.exp(jnp.where(causal, col - row, -jnp.inf)))
        kb = [k[t] * beta[t] for t in rng]
        eg = [jnp.exp(gc_col[t]) for t in rng]
        kk = [_dot_nt(kb[t], k[t]) for t in rng]
        qk = [_dot_nt(q[t], k[t]) for t in rng]
        low = [jnp.where(strict, kk[t] * decay[t], 0.0) for t in rng]
        intra = [jnp.where(causal, qk[t] * decay[t], 0.0) for t in rng]
        inv = [eye - low[t] for t in rng]
        pw = low
        for _ in range(5):
            pw = [_dot(pw[t], pw[t]) for t in rng]
            inv = [inv[t] + _dot(inv[t], pw[t]) for t in rng]
        wu = [_dot(inv[t], jnp.concatenate([kb[t] * eg[t], v[t] * beta[t]], axis=-1))
              for t in rng]
        kd = [k[t] * jnp.exp(gc_last[t] - gc_col[t]) for t in rng]
        mn = [_dot_tn(kd[t], wu[t]) for t in rng]
        io = [_dot(intra[t], wu[t]) for t in rng]
        for t, (c, h) in enumerate(inst):
            lr0 = pl.multiple_of(lcs[c] * C, C)
            mneg_ref[lcs[c], h] = (-mn[t][:, :HEAD_DIM]).astype(BF16)
            n_ref[lcs[c], h] = mn[t][:, HEAD_DIM:]
            o1_ref[pl.ds(lr0, C), cols[h]] = (q[t] * eg[t] - io[t][:, :HEAD_DIM]).astype(BF16)
            o2_ref[pl.ds(lr0, C), cols[h]] = io[t][:, HEAD_DIM:]
            a_ref[lcs[c], h] = jnp.broadcast_to(jnp.exp(gc_last[t]), (SUBLANES, HEAD_DIM))

    def recur(base, lc):
        r0 = pl.multiple_of(base + lc * C, C)
        lr0 = pl.multiple_of(lc * C, C)
        hs = range(H)
        state = [s_ref[h] for h in hs]
        sb = [state[h].astype(BF16) for h in hs]
        ms = [jnp.dot(mneg_ref[lc, h], sb[h], preferred_element_type=F32) for h in hs]
        os_ = [jnp.dot(o1_ref[pl.ds(lr0, C), cols[h]], sb[h], preferred_element_type=F32)
               for h in hs]
        for h in hs:
            s_ref[h] = state[h] * a_ref[lc, h, 0:1, :] + ms[h] + n_ref[lc, h]
        for h in hs:
            o = os_[h] + o2_ref[pl.ds(lr0, C), cols[h]]
            z = z_ref[pl.ds(r0, C), cols[h]].astype(F32)
            o_ref[pl.ds(r0, C), cols[h]] = (_rms(o, nw_ref[...])
                                            * (z * _sigmoid(z))).astype(o_ref.dtype)

    for sg in range(T // seg):
        base = sg * seg

        def a_step(gi, carry, base=base):
            precompute(base, gi)
            return carry

        def b_step(lc, carry, base=base):
            recur(base, lc)
            return carry

        lax.fori_loop(0, cps // GDN_GROUP, a_step, 0)
        lax.fori_loop(0, cps, b_step, 0)


def _gdn(gq, gk, gv, gz, bg, nw):
    B, T, _ = gq.shape
    seg = next(s for s in (1024, 512, 256, GDN_CHUNK * GDN_GROUP) if T % s == 0)
    cps = seg // GDN_CHUNK
    tok = lambda width: pl.BlockSpec((None, T, width), lambda b: (b, 0, 0))
    return pl.pallas_call(
        functools.partial(_gdn_body, T=T, seg=seg),
        grid=(B,),
        in_specs=[tok(GDN_WIDTH)] * 4 + [tok(LANES), pl.BlockSpec((1, HEAD_DIM), lambda b: (0, 0))],
        out_specs=tok(GDN_WIDTH),
        out_shape=jax.ShapeDtypeStruct((B, T, GDN_WIDTH), BF16),
        scratch_shapes=[
            pltpu.VMEM((GDN_HEADS, HEAD_DIM, HEAD_DIM), F32),
            pltpu.VMEM((cps, GDN_HEADS, HEAD_DIM, HEAD_DIM), BF16),
            pltpu.VMEM((cps, GDN_HEADS, HEAD_DIM, HEAD_DIM), F32),
            pltpu.VMEM((seg, GDN_WIDTH), BF16),
            pltpu.VMEM((seg, GDN_WIDTH), F32),
            pltpu.VMEM((cps, GDN_HEADS, SUBLANES, HEAD_DIM), F32),
        ],
        compiler_params=pltpu.CompilerParams(
            dimension_semantics=("arbitrary",), vmem_limit_bytes=VMEM_LIMIT),
        name="gdn",
    )(gq, gk, gv, gz, bg, nw)


def _moba_body(q_ref, k_ref, v_ref, o_ref, *, T):
    BS = MOBA_BLOCK
    nb = T // BS
    neg = -jnp.inf
    means = [jnp.mean(k_ref[j * BS:(j + 1) * BS, :].astype(F32), axis=0, keepdims=True)
             for j in range(nb)]
    kmean = jnp.concatenate(means + [jnp.zeros((LANES - nb, HEAD_DIM), F32)], axis=0)
    lane = lax.broadcasted_iota(jnp.int32, (BS, LANES), 1)
    ri = lax.broadcasted_iota(jnp.int32, (BS, BS), 0)
    ci = lax.broadcasted_iota(jnp.int32, (BS, BS), 1)

    for i in range(nb):
        q = q_ref[i * BS:(i + 1) * BS, :]
        sel = None
        if i > MOBA_TOPK:
            gate = lax.dot_general(q.astype(F32), kmean, (((1,), (1,)), ((), ())),
                                   precision=lax.Precision.HIGHEST, preferred_element_type=F32)
            gate = jnp.where(lane < i, gate, neg)
            sel = jnp.zeros((BS, LANES), F32)
            for _ in range(MOBA_TOPK):
                top = jnp.max(gate, axis=-1, keepdims=True)
                first = jnp.min(jnp.where(gate == top, lane, LANES), axis=-1, keepdims=True)
                pick = lane == first
                sel = jnp.where(pick, 1.0, sel)
                gate = jnp.where(pick, neg, gate)
        scores = []
        for j in range(i + 1):
            s = _dot_nt(q, k_ref[j * BS:(j + 1) * BS, :])
            if j == i:
                s = jnp.where(ci <= ri, s, neg)
            elif sel is not None:
                s = jnp.where(sel[:, j:j + 1] > 0.5, s, neg)
            scores.append(s)
        m = scores[0].max(axis=-1, keepdims=True)
        for s in scores[1:]:
            m = jnp.maximum(m, s.max(axis=-1, keepdims=True))
        acc = jnp.zeros((BS, HEAD_DIM), F32)
        den = jnp.zeros((BS, 1), F32)
        for j, s in enumerate(scores):
            p = jnp.exp(s - m)
            den = den + p.sum(axis=-1, keepdims=True)
            acc = acc + _dot(p, v_ref[j * BS:(j + 1) * BS, :])
        o_ref[i * BS:(i + 1) * BS, :] = (acc / den).astype(o_ref.dtype)


def _moba(mq, mk, mv):
    B, T, _ = mq.shape
    assert T % MOBA_BLOCK == 0 and T // MOBA_BLOCK <= LANES
    head = pl.BlockSpec((None, T, HEAD_DIM), lambda b, h: (b, 0, h))
    return pl.pallas_call(
        functools.partial(_moba_body, T=T),
        grid=(B, MOBA_HEADS),
        in_specs=[head] * 3,
        out_specs=head,
        out_shape=jax.ShapeDtypeStruct((B, T, MOBA_WIDTH), BF16),
        compiler_params=pltpu.CompilerParams(
            dimension_semantics=("arbitrary", "arbitrary"), vmem_limit_bytes=VMEM_LIMIT),
        name="moba",
    )(mq, mk, mv)


def _mlp_body(og_ref, om_ref, x_ref, p_ref, wo_ref, n1_ref, n2_ref, wu_ref, wd_ref, n3_ref,
              wg_ref, wp_ref, o_ref, *, ff_chunk):
    mix = (_dot(og_ref[...], wo_ref[0:GDN_WIDTH, :])
           + _dot(om_ref[...], wo_ref[GDN_WIDTH:GDN_WIDTH + MOBA_WIDTH, :]))
    h = x_ref[...] + _rms(mix, n1_ref[...])
    a = _rms(h, n2_ref[...]).astype(BF16)
    f = None
    for c0 in range(0, D_FF, ff_chunk):
        up = jnp.dot(a, wu_ref[:, c0:c0 + ff_chunk], preferred_element_type=F32)
        r = jnp.square(jnp.maximum(up, 0.0))
        part = _dot(r, wd_ref[c0:c0 + ff_chunk, :])
        f = part if f is None else f + part
    h = h + _rms(f, n3_ref[...])
    gate = _sigmoid(_dot(h, wg_ref[...]))
    o_ref[...] = h + gate * _dot(p_ref[...], wp_ref[...])


def _mlp(og, om, x, p, wo, n1, n2, wu, wd, n3, wg, wp, *, tm):
    B, T, D = x.shape
    tok = lambda width: pl.BlockSpec((None, tm, width), lambda b, i: (b, i, 0))
    const = lambda shape: pl.BlockSpec(shape, lambda b, i: (0,) * len(shape),
                                       pipeline_mode=pl.Buffered(1))
    return pl.pallas_call(
        functools.partial(_mlp_body, ff_chunk=1024),
        grid=(B, T // tm),
        in_specs=[tok(GDN_WIDTH), tok(MOBA_WIDTH), tok(D), tok(PLE_DIM),
                  const((D, D)), const((1, D)), const((1, D)), const((D, D_FF)),
                  const((D_FF, D)), const((1, D)), const((D, D)), const((PLE_DIM, D))],
        out_specs=tok(D),
        out_shape=jax.ShapeDtypeStruct((B, T, D), F32),
        compiler_params=pltpu.CompilerParams(
            dimension_semantics=("arbitrary", "arbitrary"), vmem_limit_bytes=VMEM_LIMIT),
        name="mlp",
    )(og, om, x, p, wo, n1, n2, wu, wd, n3, wg, wp)


def _rope_tables(T):
    half = ROPE_DIMS // 2
    inv_freq = ROPE_THETA ** (-jnp.arange(half, dtype=F32) * (2.0 / ROPE_DIMS))
    ang = jnp.arange(T, dtype=jnp.int32).astype(F32)[:, None] * inv_freq[None, :]
    cos, sin = jnp.cos(ang), jnp.sin(ang)
    zeros = jnp.zeros((T, HEAD_DIM - ROPE_DIMS), F32)
    z_half = jnp.zeros((T, half), F32)
    cos_t = jnp.concatenate([cos, cos, jnp.ones((T, HEAD_DIM - ROPE_DIMS), F32)], axis=-1)
    sa_t = jnp.concatenate([-sin, z_half, zeros], axis=-1)
    sb_t = jnp.concatenate([z_half, sin, zeros], axis=-1)
    return cos_t, sa_t, sb_t


def _layer(h, p_i, w_in, conv_w, a_log, dt_bias, gdn_norm_w, w_out, attn_pre_norm,
           attn_post_norm, mlp_pre_norm, mlp_post_norm, w_up, w_down, w_ple, w_ple_gate, tables):
    B, T, D = h.shape
    tm = min(512, T)
    assert T % tm == 0 and T % MOBA_BLOCK == 0 and T % GDN_CHUNK == 0
    gw = 4 * GDN_WIDTH
    nh = 2 * GDN_HEADS
    pad = jnp.zeros((D, LANES - nh), w_in.dtype)
    w_all = jnp.concatenate([w_in[:, :gw], w_in[:, gw + nh:], w_in[:, gw:gw + nh], pad],
                            axis=-1).astype(BF16)
    gparams = jnp.zeros((2, LANES), F32)
    gparams = gparams.at[0, GDN_HEADS:nh].set(a_log.astype(F32))
    gparams = gparams.at[1, GDN_HEADS:nh].set(dt_bias.astype(F32))
    row = lambda v: v.reshape(1, -1).astype(F32)

    gq, gk, gv, gz, bg, mq, mk, mv = _in_proj(
        h, row(attn_pre_norm), w_all, conv_w.astype(F32), gparams, *tables, tm=tm)
    o_gdn = _gdn(gq, gk, gv, gz, bg, row(gdn_norm_w))
    o_moba = _moba(mq, mk, mv)
    return _mlp(o_gdn, o_moba, h, p_i, w_out.astype(BF16), row(attn_post_norm),
                row(mlp_pre_norm), w_up.astype(BF16), w_down.astype(BF16),
                row(mlp_post_norm), w_ple_gate.astype(BF16), w_ple.astype(BF16), tm=tm)


def kernel(x, p, w_in, conv_w, a_log, dt_bias, gdn_norm_w, w_out, attn_pre_norm, attn_post_norm,
           mlp_pre_norm, mlp_post_norm, w_up, w_down, w_ple, w_ple_gate):
    tables = _rope_tables(x.shape[1])
    h = x
    for i in range(w_in.shape[0]):
        h = _layer(h, p[i], w_in[i], conv_w[i], a_log[i], dt_bias[i], gdn_norm_w[i], w_out[i],
                   attn_pre_norm[i], attn_post_norm[i], mlp_pre_norm[i], mlp_post_norm[i],
                   w_up[i], w_down[i], w_ple[i], w_ple_gate[i], tables)
    return h
```

```python
import functools

import jax
import jax.numpy as jnp
from jax import lax
from jax.experimental import pallas as pl
from jax.experimental.pallas import tpu as pltpu

F32 = jnp.float32
BF16 = jnp.bfloat16

D_MODEL = 1024
HEAD_DIM = 128
GDN_HEADS = 4
GDN_WIDTH = GDN_HEADS * HEAD_DIM
GDN_CONV = 4
GDN_CHUNK = 64
GDN_GROUP = 4
MOBA_HEADS = 4
MOBA_WIDTH = MOBA_HEADS * HEAD_DIM
MOBA_BLOCK = 256
MOBA_TOPK = 3
ROPE_DIMS = 32
ROPE_THETA = 500000.0
D_FF = 4 * D_MODEL
PLE_DIM = 256
RMS_EPS = 1e-6
LANES = 128
SUBLANES = 8
CONV_HALO = 8
QK_SCALE = HEAD_DIM ** -0.5
VMEM_LIMIT = 56 * 1024 * 1024

C_GQKV = 0
C_GZ = 3 * GDN_WIDTH
C_MQ = 4 * GDN_WIDTH
C_MK = C_MQ + MOBA_WIDTH
C_MV = C_MK + MOBA_WIDTH
C_BA = C_MV + MOBA_WIDTH
C_END = C_BA + LANES


def _rms(x, w):
    return x * lax.rsqrt(jnp.mean(x * x, axis=-1, keepdims=True) + RMS_EPS) * w


def _sigmoid(x):
    return 1.0 / (1.0 + jnp.exp(-x))


def _dot(a, b):
    return jnp.dot(a.astype(BF16), b.astype(BF16), preferred_element_type=F32)


def _dot_nt(a, b):
    return lax.dot_general(a.astype(BF16), b.astype(BF16), (((1,), (1,)), ((), ())),
                           preferred_element_type=F32)


def _dot_tn(a, b):
    return lax.dot_general(a.astype(BF16), b.astype(BF16), (((0,), (0,)), ((), ())),
                           preferred_element_type=F32)


def _in_proj_body(x_ref, nw_ref, w_ref, cw_ref, gp_ref, cos_ref, sa_ref, sb_ref,
                  gq_ref, gk_ref, gv_ref, gz_ref, bg_ref, mq_ref, mk_ref, mv_ref,
                  cbuf, *, tm):
    i = pl.program_id(1)
    u = _rms(x_ref[...], nw_ref[...]).astype(BF16)

    def proj(c0, width):
        return jnp.dot(u, w_ref[:, c0:c0 + width], preferred_element_type=F32)

    @pl.when(i == 0)
    def _():
        cbuf[0:CONV_HALO, :] = jnp.zeros((CONV_HALO, 3 * GDN_WIDTH), F32)

    @pl.when(i > 0)
    def _():
        cbuf[0:CONV_HALO, :] = cbuf[tm:tm + CONV_HALO, :]

    cbuf[CONV_HALO:CONV_HALO + tm, :] = proj(C_GQKV, 3 * GDN_WIDTH)

    outs = (gq_ref, gk_ref, gv_ref)
    for s in range(3 * GDN_HEADS):
        c0 = s * HEAD_DIM
        xs = cbuf[:, c0:c0 + HEAD_DIM]
        acc = cw_ref[0:1, c0:c0 + HEAD_DIM] * xs
        for j in range(1, GDN_CONV):
            acc = pltpu.roll(acc, 1, 0) + cw_ref[j:j + 1, c0:c0 + HEAD_DIM] * xs
        acc = acc[CONV_HALO:, :]
        y = acc * _sigmoid(acc)
        which, h = divmod(s, GDN_HEADS)
        if which < 2:
            y = y * lax.rsqrt(jnp.sum(y * y, axis=-1, keepdims=True) + RMS_EPS)
            if which == 0:
                y = y * QK_SCALE
        outs[which][:, h * HEAD_DIM:(h + 1) * HEAD_DIM] = y.astype(BF16)

    gz_ref[...] = proj(C_GZ, GDN_WIDTH).astype(BF16)

    ba = proj(C_BA, LANES)
    lane = lax.broadcasted_iota(jnp.int32, (tm, LANES), 1)
    xg = ba + gp_ref[1:2, :]
    softplus = jnp.maximum(xg, 0.0) + jnp.log1p(jnp.exp(-jnp.abs(xg)))
    g = -jnp.exp(gp_ref[0:1, :]) * softplus
    bg_ref[...] = jnp.where(lane < GDN_HEADS, _sigmoid(ba),
                            jnp.where(lane < 2 * GDN_HEADS, g, 0.0))

    cos, sa, sb = cos_ref[...], sa_ref[...], sb_ref[...]
    half = ROPE_DIMS // 2
    for ref, c_base, scale in ((mq_ref, C_MQ, QK_SCALE), (mk_ref, C_MK, None)):
        for h in range(MOBA_HEADS):
            xr = proj(c_base + h * HEAD_DIM, HEAD_DIM)
            rot = (xr * cos + pltpu.roll(xr, LANES - half, 1) * sa
                   + pltpu.roll(xr, half, 1) * sb)
            if scale is not None:
                rot = rot * scale
            ref[:, h * HEAD_DIM:(h + 1) * HEAD_DIM] = rot.astype(BF16)
    mv_ref[...] = proj(C_MV, MOBA_WIDTH).astype(BF16)


def _in_proj(x, nw, w_all, conv_w, gparams, cos_t, sa_t, sb_t, *, tm):
    B, T, D = x.shape
    tok = lambda width: pl.BlockSpec((None, tm, width), lambda b, i: (b, i, 0))
    const = lambda shape: pl.BlockSpec(shape, lambda b, i: (0,) * len(shape))
    table = pl.BlockSpec((tm, LANES), lambda b, i: (i, 0))
    o512 = jax.ShapeDtypeStruct((B, T, GDN_WIDTH), BF16)
    return pl.pallas_call(
        functools.partial(_in_proj_body, tm=tm),
        grid=(B, T // tm),
        in_specs=[tok(D), const((1, D)), const((D, C_END)), const((GDN_CONV, 3 * GDN_WIDTH)),
                  const((2, LANES)), table, table, table],
        out_specs=[tok(GDN_WIDTH)] * 4 + [tok(LANES)] + [tok(MOBA_WIDTH)] * 3,
        out_shape=[o512] * 4 + [jax.ShapeDtypeStruct((B, T, LANES), F32)] + [o512] * 3,
        scratch_shapes=[pltpu.VMEM((tm + CONV_HALO, 3 * GDN_WIDTH), F32)],
        compiler_params=pltpu.CompilerParams(
            dimension_semantics=("arbitrary", "arbitrary"), vmem_limit_bytes=VMEM_LIMIT),
        name="in_proj",
    )(x, nw, w_all, conv_w, gparams, cos_t, sa_t, sb_t)


def _gdn_body(q_ref, k_ref, v_ref, z_ref, bg_ref, nw_ref, o_ref,
              s_ref, mneg_ref, n_ref, o1_ref, o2_ref, a_ref, snap_ref, *, T, seg):
    C = GDN_CHUNK
    H = GDN_HEADS
    ri = lax.broadcasted_iota(jnp.int32, (C, C), 0)
    ci = lax.broadcasted_iota(jnp.int32, (C, C), 1)
    causal = ci <= ri
    strict = ci < ri
    eye = (ri == ci).astype(F32)
    cps = seg // C
    cols = [slice(h * HEAD_DIM, (h + 1) * HEAD_DIM) for h in range(H)]
    s_ref[...] = jnp.zeros_like(s_ref)

    def precompute(base, gi):
        inst = [(c, h) for c in range(GDN_GROUP) for h in range(H)]
        rng = range(len(inst))
        lcs = [gi * GDN_GROUP + c for c in range(GDN_GROUP)]
        r0s = [pl.multiple_of(base + lc * C, C) for lc in lcs]
        bgs = [bg_ref[pl.ds(r0, C), :] for r0 in r0s]
        q = [q_ref[pl.ds(r0s[c], C), cols[h]].astype(F32) for c, h in inst]
        k = [k_ref[pl.ds(r0s[c], C), cols[h]].astype(F32) for c, h in inst]
        v = [v_ref[pl.ds(r0s[c], C), cols[h]].astype(F32) for c, h in inst]
        beta = [bgs[c][:, h:h + 1] for c, h in inst]
        gc_col, gc_last, decay = [], [], []
        for c, h in inst:
            gb = jnp.broadcast_to(bgs[c][:, H + h:H + h + 1], (C, C))
            row = jnp.sum(jnp.where(ri <= ci, gb, 0.0), axis=0, keepdims=True)
            g_row = jnp.sum(jnp.where(ri == ci, gb, 0.0), axis=0, keepdims=True)
            col = jnp.sum(jnp.where(causal, jnp.broadcast_to(g_row, (C, C)), 0.0),
                          axis=1, keepdims=True)
            gc_col.append(col)
            gc_last.append(row[:, C - 1:C])
            decay.append(jnp.exp(jnp.where(causal, col - row, -jnp.inf)))
        kb = [k[t] * beta[t] for t in rng]
        eg = [jnp.exp(gc_col[t]) for t in rng]
        kk = [_dot_nt(kb[t], k[t]) for t in rng]
        qk = [_dot_nt(q[t], k[t]) for t in rng]
        low = [jnp.where(strict, kk[t] * decay[t], 0.0) for t in rng]
        intra = [jnp.where(causal, qk[t] * decay[t], 0.0) for t in rng]
        inv = [eye - low[t] for t in rng]
        pw = low
        for _ in range(5):
            pw = [_dot(pw[t], pw[t]) for t in rng]
            inv = [inv[t] + _dot(inv[t], pw[t]) for t in rng]
        wu = [_dot(inv[t], jnp.concatenate([kb[t] * eg[t], v[t] * beta[t]], axis=-1))
              for t in rng]
        kd = [k[t] * jnp.exp(gc_last[t] - gc_col[t]) for t in rng]
        mn = [_dot_tn(kd[t], wu[t]) for t in rng]
        io = [_dot(intra[t], wu[t]) for t in rng]
        for t, (c, h) in enumerate(inst):
            lr0 = pl.multiple_of(lcs[c] * C, C)
            mneg_ref[lcs[c], h] = (-mn[t][:, :HEAD_DIM]).astype(BF16)
            n_ref[lcs[c], h] = mn[t][:, HEAD_DIM:]
            o1_ref[pl.ds(lr0, C), cols[h]] = (q[t] * eg[t] - io[t][:, :HEAD_DIM]).astype(BF16)
            o2_ref[pl.ds(lr0, C), cols[h]] = io[t][:, HEAD_DIM:]
            a_ref[lcs[c], h] = jnp.broadcast_to(jnp.exp(gc_last[t]), (SUBLANES, HEAD_DIM))

    def recur(lc):
        hs = range(H)
        state = [s_ref[h] for h in hs]
        sb = [state[h].astype(BF16) for h in hs]
        ms = [jnp.dot(mneg_ref[lc, h], sb[h], preferred_element_type=F32) for h in hs]
        for h in hs:
            snap_ref[lc, h] = sb[h]
            s_ref[h] = state[h] * a_ref[lc, h, 0:1, :] + ms[h] + n_ref[lc, h]

    def emit(base, gi):
        inst = [(gi * GDN_GROUP + c, h) for c in range(GDN_GROUP) for h in range(H)]
        lr0s = [pl.multiple_of(lc * C, C) for lc, _ in inst]
        r0s = [pl.multiple_of(base + lc * C, C) for lc, _ in inst]
        os_ = [jnp.dot(o1_ref[pl.ds(lr0s[t], C), cols[h]], snap_ref[lc, h],
                       preferred_element_type=F32) for t, (lc, h) in enumerate(inst)]
        for t, (lc, h) in enumerate(inst):
            o = os_[t] + o2_ref[pl.ds(lr0s[t], C), cols[h]]
            z = z_ref[pl.ds(r0s[t], C), cols[h]].astype(F32)
            o_ref[pl.ds(r0s[t], C), cols[h]] = (_rms(o, nw_ref[...])
                                                * (z * _sigmoid(z))).astype(o_ref.dtype)

    for sg in range(T // seg):
        base = sg * seg

        def a_step(gi, carry, base=base):
            precompute(base, gi)
            return carry

        def b_step(lc, carry):
            recur(lc)
            return carry

        def c_step(gi, carry, base=base):
            emit(base, gi)
            return carry

        lax.fori_loop(0, cps // GDN_GROUP, a_step, 0)
        lax.fori_loop(0, cps, b_step, 0)
        lax.fori_loop(0, cps // GDN_GROUP, c_step, 0)


def _gdn(gq, gk, gv, gz, bg, nw):
    B, T, _ = gq.shape
    seg = next(s for s in (1024, 512, 256, GDN_CHUNK * GDN_GROUP) if T % s == 0)
    cps = seg // GDN_CHUNK
    tok = lambda width: pl.BlockSpec((None, T, width), lambda b: (b, 0, 0))
    return pl.pallas_call(
        functools.partial(_gdn_body, T=T, seg=seg),
        grid=(B,),
        in_specs=[tok(GDN_WIDTH)] * 4 + [tok(LANES), pl.BlockSpec((1, HEAD_DIM), lambda b: (0, 0))],
        out_specs=tok(GDN_WIDTH),
        out_shape=jax.ShapeDtypeStruct((B, T, GDN_WIDTH), BF16),
        scratch_shapes=[
            pltpu.VMEM((GDN_HEADS, HEAD_DIM, HEAD_DIM), F32),
            pltpu.VMEM((cps, GDN_HEADS, HEAD_DIM, HEAD_DIM), BF16),
            pltpu.VMEM((cps, GDN_HEADS, HEAD_DIM, HEAD_DIM), F32),
            pltpu.VMEM((seg, GDN_WIDTH), BF16),
            pltpu.VMEM((seg, GDN_WIDTH), F32),
            pltpu.VMEM((cps, GDN_HEADS, SUBLANES, HEAD_DIM), F32),
            pltpu.VMEM((cps, GDN_HEADS, HEAD_DIM, HEAD_DIM), BF16),
        ],
        compiler_params=pltpu.CompilerParams(
            dimension_semantics=("arbitrary",), vmem_limit_bytes=VMEM_LIMIT),
        name="gdn",
    )(gq, gk, gv, gz, bg, nw)


def _moba_body(q_ref, k_ref, v_ref, o_ref, vt_ref, *, T):
    BS = MOBA_BLOCK
    nb = T // BS
    nrow = -(-nb // SUBLANES) * SUBLANES
    neg = -jnp.inf
    means = [jnp.mean(k_ref[j * BS:(j + 1) * BS, :].astype(F32), axis=0, keepdims=True)
             for j in range(nb)]
    if nrow > nb:
        means.append(jnp.zeros((nrow - nb, HEAD_DIM), F32))
    kmean = jnp.concatenate(means, axis=0)
    for j in range(nb):
        vt_ref[:, j * BS:(j + 1) * BS] = v_ref[j * BS:(j + 1) * BS, :].T
    blk = lax.broadcasted_iota(jnp.int32, (nrow, BS), 0)
    key = lax.broadcasted_iota(jnp.int32, (BS, BS), 0)
    qry = lax.broadcasted_iota(jnp.int32, (BS, BS), 1)

    for i in range(nb):
        q = q_ref[i * BS:(i + 1) * BS, :]
        sel = None
        if i > MOBA_TOPK:
            gate = lax.dot_general(kmean, q.astype(F32), (((1,), (1,)), ((), ())),
                                   precision=lax.Precision.HIGHEST, preferred_element_type=F32)
            gate = jnp.where(blk < i, gate, neg)
            rank = jnp.zeros((nrow, BS), F32)
            for jp in range(i):
                gj = gate[jp:jp + 1, :]
                beats = (gj > gate) | ((gj == gate) & (blk > jp))
                rank = rank + jnp.where(beats, 1.0, 0.0)
            sel = jnp.where((rank < MOBA_TOPK) & (blk < i), 1.0, 0.0)
        scores = []
        for j in range(i + 1):
            s = _dot_nt(k_ref[j * BS:(j + 1) * BS, :], q)
            if j == i:
                s = jnp.where(key <= qry, s, neg)
            elif sel is not None:
                s = jnp.where(sel[j:j + 1, :] > 0.5, s, neg)
            scores.append(s)
        m = scores[0].max(axis=0, keepdims=True)
        for s in scores[1:]:
            m = jnp.maximum(m, s.max(axis=0, keepdims=True))
        acc = jnp.zeros((HEAD_DIM, BS), F32)
        den = jnp.zeros((1, BS), F32)
        for j, s in enumerate(scores):
            p = jnp.exp(s - m)
            den = den + p.sum(axis=0, keepdims=True)
            acc = acc + _dot(vt_ref[:, j * BS:(j + 1) * BS], p)
        o_ref[i * BS:(i + 1) * BS, :] = (acc / den).T.astype(o_ref.dtype)


def _moba(mq, mk, mv):
    B, T, _ = mq.shape
    assert T % MOBA_BLOCK == 0
    head = pl.BlockSpec((None, T, HEAD_DIM), lambda b, h: (b, 0, h))
    return pl.pallas_call(
        functools.partial(_moba_body, T=T),
        grid=(B, MOBA_HEADS),
        in_specs=[head] * 3,
        out_specs=head,
        out_shape=jax.ShapeDtypeStruct((B, T, MOBA_WIDTH), BF16),
        scratch_shapes=[pltpu.VMEM((HEAD_DIM, T), BF16)],
        compiler_params=pltpu.CompilerParams(
            dimension_semantics=("arbitrary", "arbitrary"), vmem_limit_bytes=VMEM_LIMIT),
        name="moba",
    )(mq, mk, mv)


def _mlp_body(og_ref, om_ref, x_ref, p_ref, wo_ref, n1_ref, n2_ref, wu_ref, wd_ref, n3_ref,
              wg_ref, wp_ref, o_ref, *, ff_chunk):
    mix = (_dot(og_ref[...], wo_ref[0:GDN_WIDTH, :])
           + _dot(om_ref[...], wo_ref[GDN_WIDTH:GDN_WIDTH + MOBA_WIDTH, :]))
    h = x_ref[...] + _rms(mix, n1_ref[...])
    a = _rms(h, n2_ref[...]).astype(BF16)
    f = None
    for c0 in range(0, D_FF, ff_chunk):
        up = jnp.dot(a, wu_ref[:, c0:c0 + ff_chunk], preferred_element_type=F32)
        r = jnp.square(jnp.maximum(up, 0.0))
        part = _dot(r, wd_ref[c0:c0 + ff_chunk, :])
        f = part if f is None else f + part
    h = h + _rms(f, n3_ref[...])
    gate = _sigmoid(_dot(h, wg_ref[...]))
    o_ref[...] = h + gate * _dot(p_ref[...], wp_ref[...])


def _mlp(og, om, x, p, wo, n1, n2, wu, wd, n3, wg, wp, *, tm):
    B, T, D = x.shape
    tok = lambda width: pl.BlockSpec((None, tm, width), lambda b, i: (b, i, 0))
    const = lambda shape: pl.BlockSpec(shape, lambda b, i: (0,) * len(shape),
                                       pipeline_mode=pl.Buffered(1))
    return pl.pallas_call(
        functools.partial(_mlp_body, ff_chunk=1024),
        grid=(B, T // tm),
        in_specs=[tok(GDN_WIDTH), tok(MOBA_WIDTH), tok(D), tok(PLE_DIM),
                  const((D, D)), const((1, D)), const((1, D)), const((D, D_FF)),
                  const((D_FF, D)), const((1, D)), const((D, D)), const((PLE_DIM, D))],
        out_specs=tok(D),
        out_shape=jax.ShapeDtypeStruct((B, T, D), F32),
        compiler_params=pltpu.CompilerParams(
            dimension_semantics=("arbitrary", "arbitrary"), vmem_limit_bytes=VMEM_LIMIT),
        name="mlp",
    )(og, om, x, p, wo, n1, n2, wu, wd, n3, wg, wp)


def _rope_tables(T):
    half = ROPE_DIMS // 2
    inv_freq = ROPE_THETA ** (-jnp.arange(half, dtype=F32) * (2.0 / ROPE_DIMS))
    ang = jnp.arange(T, dtype=jnp.int32).astype(F32)[:, None] * inv_freq[None, :]
    cos, sin = jnp.cos(ang), jnp.sin(ang)
    zeros = jnp.zeros((T, HEAD_DIM - ROPE_DIMS), F32)
    z_half = jnp.zeros((T, half), F32)
    cos_t = jnp.concatenate([cos, cos, jnp.ones((T, HEAD_DIM - ROPE_DIMS), F32)], axis=-1)
    sa_t = jnp.concatenate([-sin, z_half, zeros], axis=-1)
    sb_t = jnp.concatenate([z_half, sin, zeros], axis=-1)
    return cos_t, sa_t, sb_t


def _layer(h, p_i, w_in, conv_w, a_log, dt_bias, gdn_norm_w, w_out, attn_pre_norm,
           attn_post_norm, mlp_pre_norm, mlp_post_norm, w_up, w_down, w_ple, w_ple_gate, tables):
    B, T, D = h.shape
    tm = min(512, T)
    assert T % tm == 0 and T % MOBA_BLOCK == 0 and T % GDN_CHUNK == 0
    gw = 4 * GDN_WIDTH
    nh = 2 * GDN_HEADS
    pad = jnp.zeros((D, LANES - nh), w_in.dtype)
    w_all = jnp.concatenate([w_in[:, :gw], w_in[:, gw + nh:], w_in[:, gw:gw + nh], pad],
                            axis=-1).astype(BF16)
    gparams = jnp.zeros((2, LANES), F32)
    gparams = gparams.at[0, GDN_HEADS:nh].set(a_log.astype(F32))
    gparams = gparams.at[1, GDN_HEADS:nh].set(dt_bias.astype(F32))
    row = lambda v: v.reshape(1, -1).astype(F32)

    gq, gk, gv, gz, bg, mq, mk, mv = _in_proj(
        h, row(attn_pre_norm), w_all, conv_w.astype(F32), gparams, *tables, tm=tm)
    o_gdn = _gdn(gq, gk, gv, gz, bg, row(gdn_norm_w))
    o_moba = _moba(mq, mk, mv)
    return _mlp(o_gdn, o_moba, h, p_i, w_out.astype(BF16), row(attn_post_norm),
                row(mlp_pre_norm), w_up.astype(BF16), w_down.astype(BF16),
                row(mlp_post_norm), w_ple_gate.astype(BF16), w_ple.astype(BF16), tm=tm)


def kernel(x, p, w_in, conv_w, a_log, dt_bias, gdn_norm_w, w_out, attn_pre_norm, attn_post_norm,
           mlp_pre_norm, mlp_post_norm, w_up, w_down, w_ple, w_ple_gate):
    tables = _rope_tables(x.shape[1])
    h = x
    for i in range(w_in.shape[0]):
        h = _layer(h, p[i], w_in[i], conv_w[i], a_log[i], dt_bias[i], gdn_norm_w[i], w_out[i],
                   attn_pre_norm[i], attn_post_norm[i], mlp_pre_norm[i], mlp_post_norm[i],
                   w_up[i], w_down[i], w_ple[i], w_ple_gate[i], tables)
    return h
```

```python
import functools
import math

import jax
import jax.numpy as jnp
from jax import lax
from jax.experimental import pallas as pl
from jax.experimental.pallas import tpu as pltpu

F32 = jnp.float32
BF16 = jnp.bfloat16

D_MODEL = 1024
HEAD_DIM = 128
GDN_HEADS = 4
GDN_WIDTH = GDN_HEADS * HEAD_DIM
GDN_CONV = 4
GDN_CHUNK = 64
GDN_GROUP = 4
MOBA_HEADS = 4
MOBA_WIDTH = MOBA_HEADS * HEAD_DIM
MOBA_BLOCK = 256
MOBA_TOPK = 3
ROPE_DIMS = 32
ROPE_THETA = 500000.0
D_FF = 4 * D_MODEL
PLE_DIM = 256
RMS_EPS = 1e-6
LANES = 128
SUBLANES = 8
BF16_ROWS = 16
CONV_HALO = 8
QK_SCALE = HEAD_DIM ** -0.5
MOBA_Q_SCALE = QK_SCALE * math.log2(math.e)
VMEM_LIMIT = 56 * 1024 * 1024

C_GQKV = 0
C_GZ = 3 * GDN_WIDTH
C_MQ = 4 * GDN_WIDTH
C_MK = C_MQ + MOBA_WIDTH
C_MV = C_MK + MOBA_WIDTH
C_BA = C_MV + MOBA_WIDTH
C_END = C_BA + LANES


def _rms(x, w):
    return x * lax.rsqrt(jnp.mean(x * x, axis=-1, keepdims=True) + RMS_EPS) * w


def _sigmoid(x):
    return 1.0 / (1.0 + jnp.exp(-x))


def _dot(a, b):
    return jnp.dot(a.astype(BF16), b.astype(BF16), preferred_element_type=F32)


def _dot_nt(a, b):
    return lax.dot_general(a.astype(BF16), b.astype(BF16), (((1,), (1,)), ((), ())),
                           preferred_element_type=F32)


def _dot_tn(a, b):
    return lax.dot_general(a.astype(BF16), b.astype(BF16), (((0,), (0,)), ((), ())),
                           preferred_element_type=F32)


def _in_proj_body(x_ref, nw_ref, w_ref, cw_ref, gp_ref, cos_ref, sa_ref, sb_ref,
                  gq_ref, gk_ref, gv_ref, gz_ref, bg_ref, mq_ref, mk_ref, mv_ref,
                  cbuf, *, tm):
    i = pl.program_id(1)
    u = _rms(x_ref[...], nw_ref[...]).astype(BF16)

    def proj(c0, width):
        return jnp.dot(u, w_ref[:, c0:c0 + width], preferred_element_type=F32)

    @pl.when(i == 0)
    def _():
        cbuf[0:CONV_HALO, :] = jnp.zeros((CONV_HALO, 3 * GDN_WIDTH), F32)

    @pl.when(i > 0)
    def _():
        cbuf[0:CONV_HALO, :] = cbuf[tm:tm + CONV_HALO, :]

    cbuf[CONV_HALO:CONV_HALO + tm, :] = proj(C_GQKV, 3 * GDN_WIDTH)

    outs = (gq_ref, gk_ref, gv_ref)
    for s in range(3 * GDN_HEADS):
        c0 = s * HEAD_DIM
        cw = [cw_ref[j:j + 1, c0:c0 + HEAD_DIM] for j in range(GDN_CONV)]
        xs = cbuf[:, c0:c0 + HEAD_DIM]
        x1 = pltpu.roll(xs, 1, 0)
        acc = (cw[3] * xs + cw[2] * x1) + pltpu.roll(cw[1] * xs + cw[0] * x1, 2, 0)
        acc = acc[CONV_HALO:, :]
        y = acc * _sigmoid(acc)
        which, h = divmod(s, GDN_HEADS)
        if which < 2:
            y = y * lax.rsqrt(jnp.sum(y * y, axis=-1, keepdims=True) + RMS_EPS)
            if which == 0:
                y = y * QK_SCALE
        outs[which][:, h * HEAD_DIM:(h + 1) * HEAD_DIM] = y.astype(BF16)

    gz_ref[...] = proj(C_GZ, GDN_WIDTH).astype(BF16)

    ba = proj(C_BA, LANES)
    lane = lax.broadcasted_iota(jnp.int32, (tm, LANES), 1)
    xg = ba + gp_ref[1:2, :]
    softplus = jnp.maximum(xg, 0.0) + jnp.log1p(jnp.exp(-jnp.abs(xg)))
    g = -jnp.exp(gp_ref[0:1, :]) * softplus
    bg_ref[...] = jnp.where(lane < GDN_HEADS, _sigmoid(ba),
                            jnp.where(lane < 2 * GDN_HEADS, g, 0.0))

    cos, sa, sb = cos_ref[...], sa_ref[...], sb_ref[...]
    half = ROPE_DIMS // 2
    for ref, c_base, scale in ((mq_ref, C_MQ, MOBA_Q_SCALE), (mk_ref, C_MK, None)):
        xall = proj(c_base, MOBA_WIDTH)
        for h in range(MOBA_HEADS):
            xr = xall[:, h * HEAD_DIM:(h + 1) * HEAD_DIM]
            rot = (xr * cos + pltpu.roll(xr, LANES - half, 1) * sa
                   + pltpu.roll(xr, half, 1) * sb)
            if scale is not None:
                rot = rot * scale
            ref[:, h * HEAD_DIM:(h + 1) * HEAD_DIM] = rot.astype(BF16)
    mv_ref[...] = proj(C_MV, MOBA_WIDTH).astype(BF16)


def _in_proj(x, nw, w_all, conv_w, gparams, cos_t, sa_t, sb_t, *, tm):
    B, T, D = x.shape
    tok = lambda width: pl.BlockSpec((None, tm, width), lambda b, i: (b, i, 0))
    const = lambda shape: pl.BlockSpec(shape, lambda b, i: (0,) * len(shape))
    table = pl.BlockSpec((tm, LANES), lambda b, i: (i, 0))
    o512 = jax.ShapeDtypeStruct((B, T, GDN_WIDTH), BF16)
    return pl.pallas_call(
        functools.partial(_in_proj_body, tm=tm),
        grid=(B, T // tm),
        in_specs=[tok(D), const((1, D)), const((D, C_END)), const((GDN_CONV, 3 * GDN_WIDTH)),
                  const((2, LANES)), table, table, table],
        out_specs=[tok(GDN_WIDTH)] * 4 + [tok(LANES)] + [tok(MOBA_WIDTH)] * 3,
        out_shape=[o512] * 4 + [jax.ShapeDtypeStruct((B, T, LANES), F32)] + [o512] * 3,
        scratch_shapes=[pltpu.VMEM((tm + CONV_HALO, 3 * GDN_WIDTH), F32)],
        compiler_params=pltpu.CompilerParams(
            dimension_semantics=("arbitrary", "arbitrary"), vmem_limit_bytes=VMEM_LIMIT),
        name="in_proj",
    )(x, nw, w_all, conv_w, gparams, cos_t, sa_t, sb_t)


def _gdn_body(q_ref, k_ref, v_ref, z_ref, bg_ref, nw_ref, o_ref,
              s_ref, mneg_ref, n_ref, o1_ref, o2_ref, a_ref, snap_ref, *, T, seg):
    C = GDN_CHUNK
    H = GDN_HEADS
    ri = lax.broadcasted_iota(jnp.int32, (C, C), 0)
    ci = lax.broadcasted_iota(jnp.int32, (C, C), 1)
    causal = ci <= ri
    strict = ci < ri
    eye = (ri == ci).astype(F32)
    cps = seg // C
    cols = [slice(h * HEAD_DIM, (h + 1) * HEAD_DIM) for h in range(H)]
    s_ref[...] = jnp.zeros_like(s_ref)

    def precompute(base, gi):
        inst = [(c, h) for c in range(GDN_GROUP) for h in range(H)]
        rng = range(len(inst))
        lcs = [gi * GDN_GROUP + c for c in range(GDN_GROUP)]
        r0s = [pl.multiple_of(base + lc * C, C) for lc in lcs]
        bgs = [bg_ref[pl.ds(r0, C), :] for r0 in r0s]
        q = [q_ref[pl.ds(r0s[c], C), cols[h]].astype(F32) for c, h in inst]
        k = [k_ref[pl.ds(r0s[c], C), cols[h]].astype(F32) for c, h in inst]
        v = [v_ref[pl.ds(r0s[c], C), cols[h]].astype(F32) for c, h in inst]
        beta = [bgs[c][:, h:h + 1] for c, h in inst]
        gc_col, gc_last, decay = [], [], []
        for c, h in inst:
            gb = jnp.broadcast_to(bgs[c][:, H + h:H + h + 1], (C, C))
            row = jnp.sum(jnp.where(ri <= ci, gb, 0.0), axis=0, keepdims=True)
            g_row = jnp.sum(jnp.where(ri == ci, gb, 0.0), axis=0, keepdims=True)
            col = jnp.sum(jnp.where(causal, jnp.broadcast_to(g_row, (C, C)), 0.0),
                          axis=1, keepdims=True)
            gc_col.append(col)
            gc_last.append(row[:, C - 1:C])
            decay.append(jnp.exp(jnp.where(causal, col - row, -jnp.inf)))
        kb = [k[t] * beta[t] for t in rng]
        eg = [jnp.exp(gc_col[t]) for t in rng]
        kk = [_dot_nt(kb[t], k[t]) for t in rng]
        qk = [_dot_nt(q[t], k[t]) for t in rng]
        low = [jnp.where(strict, kk[t] * decay[t], 0.0) for t in rng]
        intra = [jnp.where(causal, qk[t] * decay[t], 0.0) for t in rng]
        inv = [eye - low[t] for t in rng]
        pw = low
        for _ in range(5):
            pw = [_dot(pw[t], pw[t]) for t in rng]
            inv = [inv[t] + _dot(inv[t], pw[t]) for t in rng]
        wu = [_dot(inv[t], jnp.concatenate([kb[t] * eg[t], v[t] * beta[t]], axis=-1))
              for t in rng]
        kd = [k[t] * jnp.exp(gc_last[t] - gc_col[t]) for t in rng]
        mn = [_dot_tn(kd[t], wu[t]) for t in rng]
        io = [_dot(intra[t], wu[t]) for t in rng]
        for t, (c, h) in enumerate(inst):
            lr0 = pl.multiple_of(lcs[c] * C, C)
            mneg_ref[lcs[c], h] = (-mn[t][:, :HEAD_DIM]).astype(BF16)
            n_ref[lcs[c], h] = mn[t][:, HEAD_DIM:]
            o1_ref[pl.ds(lr0, C), cols[h]] = (q[t] * eg[t] - io[t][:, :HEAD_DIM]).astype(BF16)
            o2_ref[pl.ds(lr0, C), cols[h]] = io[t][:, HEAD_DIM:]
            a_ref[lcs[c], h] = jnp.broadcast_to(jnp.exp(gc_last[t]), (SUBLANES, HEAD_DIM))

    def recur(lc):
        hs = range(H)
        state = [s_ref[h] for h in hs]
        sb = [state[h].astype(BF16) for h in hs]
        ms = [jnp.dot(mneg_ref[lc, h], sb[h], preferred_element_type=F32) for h in hs]
        for h in hs:
            snap_ref[lc, h] = sb[h]
            s_ref[h] = state[h] * a_ref[lc, h, 0:1, :] + ms[h] + n_ref[lc, h]

    def emit(base, gi):
        inst = [(gi * GDN_GROUP + c, h) for c in range(GDN_GROUP) for h in range(H)]
        lr0s = [pl.multiple_of(lc * C, C) for lc, _ in inst]
        r0s = [pl.multiple_of(base + lc * C, C) for lc, _ in inst]
        os_ = [jnp.dot(o1_ref[pl.ds(lr0s[t], C), cols[h]], snap_ref[lc, h],
                       preferred_element_type=F32) for t, (lc, h) in enumerate(inst)]
        for t, (lc, h) in enumerate(inst):
            o = os_[t] + o2_ref[pl.ds(lr0s[t], C), cols[h]]
            z = z_ref[pl.ds(r0s[t], C), cols[h]].astype(F32)
            o_ref[pl.ds(r0s[t], C), cols[h]] = (_rms(o, nw_ref[...])
                                                * (z * _sigmoid(z))).astype(o_ref.dtype)

    for sg in range(T // seg):
        base = sg * seg

        def a_step(gi, carry, base=base):
            precompute(base, gi)
            return carry

        def b_step(lc, carry):
            recur(lc)
            return carry

        def c_step(gi, carry, base=base):
            emit(base, gi)
            return carry

        lax.fori_loop(0, cps // GDN_GROUP, a_step, 0)
        lax.fori_loop(0, cps, b_step, 0)
        lax.fori_loop(0, cps // GDN_GROUP, c_step, 0)


def _gdn(gq, gk, gv, gz, bg, nw):
    B, T, _ = gq.shape
    seg = next(s for s in (1024, 512, 256, GDN_CHUNK * GDN_GROUP) if T % s == 0)
    cps = seg // GDN_CHUNK
    tok = lambda width: pl.BlockSpec((None, T, width), lambda b: (b, 0, 0))
    return pl.pallas_call(
        functools.partial(_gdn_body, T=T, seg=seg),
        grid=(B,),
        in_specs=[tok(GDN_WIDTH)] * 4 + [tok(LANES), pl.BlockSpec((1, HEAD_DIM), lambda b: (0, 0))],
        out_specs=tok(GDN_WIDTH),
        out_shape=jax.ShapeDtypeStruct((B, T, GDN_WIDTH), BF16),
        scratch_shapes=[
            pltpu.VMEM((GDN_HEADS, HEAD_DIM, HEAD_DIM), F32),
            pltpu.VMEM((cps, GDN_HEADS, HEAD_DIM, HEAD_DIM), BF16),
            pltpu.VMEM((cps, GDN_HEADS, HEAD_DIM, HEAD_DIM), F32),
            pltpu.VMEM((seg, GDN_WIDTH), BF16),
            pltpu.VMEM((seg, GDN_WIDTH), F32),
            pltpu.VMEM((cps, GDN_HEADS, SUBLANES, HEAD_DIM), F32),
            pltpu.VMEM((cps, GDN_HEADS, HEAD_DIM, HEAD_DIM), BF16),
        ],
        compiler_params=pltpu.CompilerParams(
            dimension_semantics=("arbitrary",), vmem_limit_bytes=VMEM_LIMIT),
        name="gdn",
    )(gq, gk, gv, gz, bg, nw)


def _moba_body(q_ref, k_ref, v_ref, o_ref, vt_ref, *, T):
    BS = MOBA_BLOCK
    nb = T // BS
    nrow = -(-nb // SUBLANES) * SUBLANES
    neg = -jnp.inf
    means = [jnp.mean(k_ref[j * BS:(j + 1) * BS, :].astype(F32), axis=0, keepdims=True)
             for j in range(nb)]
    if nrow > nb:
        means.append(jnp.zeros((nrow - nb, HEAD_DIM), F32))
    kmean = jnp.concatenate(means, axis=0)
    for j in range(nb):
        vt_ref[0:HEAD_DIM, j * BS:(j + 1) * BS] = v_ref[j * BS:(j + 1) * BS, :].T
    vt_ref[HEAD_DIM:, :] = jnp.ones((BF16_ROWS, T), BF16)
    blk = lax.broadcasted_iota(jnp.int32, (nrow, BS), 0)
    key = lax.broadcasted_iota(jnp.int32, (BS, BS), 0)
    qry = lax.broadcasted_iota(jnp.int32, (BS, BS), 1)

    def block_scores(i):
        q = q_ref[i * BS:(i + 1) * BS, :]
        sel = None
        if i > MOBA_TOPK:
            gate = lax.dot_general(kmean, q.astype(F32), (((1,), (1,)), ((), ())),
                                   precision=lax.Precision.HIGHEST, preferred_element_type=F32)
            gate = jnp.where(blk < i, gate, neg)
            rank = jnp.zeros((nrow, BS), F32)
            for jp in range(i):
                gj = gate[jp:jp + 1, :]
                beats = (gj > gate) | ((gj == gate) & (blk > jp))
                rank = rank + jnp.where(beats, 1.0, 0.0)
            sel = jnp.where((rank < MOBA_TOPK) & (blk < i), 1.0, 0.0)
        scores = []
        for j in range(i + 1):
            s = _dot_nt(k_ref[j * BS:(j + 1) * BS, :], q)
            if j == i:
                s = jnp.where(key <= qry, s, neg)
            elif sel is not None:
                s = jnp.where(sel[j:j + 1, :] > 0.5, s, neg)
            scores.append(s)
        return scores

    def block_output(i, scores):
        m = scores[0].max(axis=0, keepdims=True)
        for s in scores[1:]:
            m = jnp.maximum(m, s.max(axis=0, keepdims=True))
        acc = jnp.zeros((HEAD_DIM + BF16_ROWS, BS), F32)
        for j, s in enumerate(scores):
            acc = acc + _dot(vt_ref[:, j * BS:(j + 1) * BS], jnp.exp2(s - m))
        out = acc[:HEAD_DIM, :] / acc[HEAD_DIM:HEAD_DIM + 1, :]
        o_ref[i * BS:(i + 1) * BS, :] = out.T.astype(o_ref.dtype)

    scores = block_scores(0)
    for i in range(nb):
        following = block_scores(i + 1) if i + 1 < nb else None
        block_output(i, scores)
        scores = following


def _moba(mq, mk, mv):
    B, T, _ = mq.shape
    assert T % MOBA_BLOCK == 0
    head = pl.BlockSpec((None, T, HEAD_DIM), lambda b, h: (b, 0, h))
    return pl.pallas_call(
        functools.partial(_moba_body, T=T),
        grid=(B, MOBA_HEADS),
        in_specs=[head] * 3,
        out_specs=head,
        out_shape=jax.ShapeDtypeStruct((B, T, MOBA_WIDTH), BF16),
        scratch_shapes=[pltpu.VMEM((HEAD_DIM + BF16_ROWS, T), BF16)],
        compiler_params=pltpu.CompilerParams(
            dimension_semantics=("arbitrary", "arbitrary"), vmem_limit_bytes=VMEM_LIMIT),
        name="moba",
    )(mq, mk, mv)


def _mlp_body(og_ref, om_ref, x_ref, p_ref, wo_ref, n1_ref, n2_ref, wu_ref, wd_ref, n3_ref,
              wg_ref, wp_ref, o_ref, *, ff_chunk):
    mix = (_dot(og_ref[...], wo_ref[0:GDN_WIDTH, :])
           + _dot(om_ref[...], wo_ref[GDN_WIDTH:GDN_WIDTH + MOBA_WIDTH, :]))
    h = x_ref[...] + _rms(mix, n1_ref[...])
    a = _rms(h, n2_ref[...]).astype(BF16)
    f = None
    for c0 in range(0, D_FF, ff_chunk):
        up = jnp.dot(a, wu_ref[:, c0:c0 + ff_chunk], preferred_element_type=F32)
        r = jnp.square(jnp.maximum(up, 0.0))
        part = _dot(r, wd_ref[c0:c0 + ff_chunk, :])
        f = part if f is None else f + part
    h = h + _rms(f, n3_ref[...])
    gate = _sigmoid(_dot(h, wg_ref[...]))
    o_ref[...] = h + gate * _dot(p_ref[...], wp_ref[...])


def _mlp(og, om, x, p, wo, n1, n2, wu, wd, n3, wg, wp, *, tm):
    B, T, D = x.shape
    tok = lambda width: pl.BlockSpec((None, tm, width), lambda b, i: (b, i, 0))
    const = lambda shape: pl.BlockSpec(shape, lambda b, i: (0,) * len(shape),
                                       pipeline_mode=pl.Buffered(1))
    return pl.pallas_call(
        functools.partial(_mlp_body, ff_chunk=1024),
        grid=(B, T // tm),
        in_specs=[tok(GDN_WIDTH), tok(MOBA_WIDTH), tok(D), tok(PLE_DIM),
                  const((D, D)), const((1, D)), const((1, D)), const((D, D_FF)),
                  const((D_FF, D)), const((1, D)), const((D, D)), const((PLE_DIM, D))],
        out_specs=tok(D),
        out_shape=jax.ShapeDtypeStruct((B, T, D), F32),
        compiler_params=pltpu.CompilerParams(
            dimension_semantics=("arbitrary", "arbitrary"), vmem_limit_bytes=VMEM_LIMIT),
        name="mlp",
    )(og, om, x, p, wo, n1, n2, wu, wd, n3, wg, wp)


def _rope_tables(T):
    half = ROPE_DIMS // 2
    inv_freq = ROPE_THETA ** (-jnp.arange(half, dtype=F32) * (2.0 / ROPE_DIMS))
    ang = jnp.arange(T, dtype=jnp.int32).astype(F32)[:, None] * inv_freq[None, :]
    cos, sin = jnp.cos(ang), jnp.sin(ang)
    zeros = jnp.zeros((T, HEAD_DIM - ROPE_DIMS), F32)
    z_half = jnp.zeros((T, half), F32)
    cos_t = jnp.concatenate([cos, cos, jnp.ones((T, HEAD_DIM - ROPE_DIMS), F32)], axis=-1)
    sa_t = jnp.concatenate([-sin, z_half, zeros], axis=-1)
    sb_t = jnp.concatenate([z_half, sin, zeros], axis=-1)
    return cos_t, sa_t, sb_t


def _layer(h, p_i, w_in, conv_w, a_log, dt_bias, gdn_norm_w, w_out, attn_pre_norm,
           attn_post_norm, mlp_pre_norm, mlp_post_norm, w_up, w_down, w_ple, w_ple_gate, tables):
    B, T, D = h.shape
    tm = min(512, T)
    assert T % tm == 0 and T % MOBA_BLOCK == 0 and T % GDN_CHUNK == 0
    gw = 4 * GDN_WIDTH
    nh = 2 * GDN_HEADS
    pad = jnp.zeros((D, LANES - nh), w_in.dtype)
    w_all = jnp.concatenate([w_in[:, :gw], w_in[:, gw + nh:], w_in[:, gw:gw + nh], pad],
                            axis=-1).astype(BF16)
    gparams = jnp.zeros((2, LANES), F32)
    gparams = gparams.at[0, GDN_HEADS:nh].set(a_log.astype(F32))
    gparams = gparams.at[1, GDN_HEADS:nh].set(dt_bias.astype(F32))
    row = lambda v: v.reshape(1, -1).astype(F32)

    gq, gk, gv, gz, bg, mq, mk, mv = _in_proj(
        h, row(attn_pre_norm), w_all, conv_w.astype(F32), gparams, *tables, tm=tm)
    o_gdn = _gdn(gq, gk, gv, gz, bg, row(gdn_norm_w))
    o_moba = _moba(mq, mk, mv)
    return _mlp(o_gdn, o_moba, h, p_i, w_out.astype(BF16), row(attn_post_norm),
                row(mlp_pre_norm), w_up.astype(BF16), w_down.astype(BF16),
                row(mlp_post_norm), w_ple_gate.astype(BF16), w_ple.astype(BF16), tm=tm)


def kernel(x, p, w_in, conv_w, a_log, dt_bias, gdn_norm_w, w_out, attn_pre_norm, attn_post_norm,
           mlp_pre_norm, mlp_post_norm, w_up, w_down, w_ple, w_ple_gate):
    tables = _rope_tables(x.shape[1])
    h = x
    for i in range(w_in.shape[0]):
        h = _layer(h, p[i], w_in[i], conv_w[i], a_log[i], dt_bias[i], gdn_norm_w[i], w_out[i],
                   attn_pre_norm[i], attn_post_norm[i], mlp_pre_norm[i], mlp_post_norm[i],
                   w_up[i], w_down[i], w_ple[i], w_ple_gate[i], tables)
    return h
```

```python
import functools
import math

import jax
import jax.numpy as jnp
from jax import lax
from jax.experimental import pallas as pl
from jax.experimental.pallas import tpu as pltpu

F32 = jnp.float32
BF16 = jnp.bfloat16

D_MODEL = 1024
HEAD_DIM = 128
GDN_HEADS = 4
GDN_WIDTH = GDN_HEADS * HEAD_DIM
GDN_CONV = 4
GDN_CHUNK = 64
GDN_GROUP = 4
MOBA_HEADS = 4
MOBA_WIDTH = MOBA_HEADS * HEAD_DIM
MOBA_BLOCK = 256
MOBA_TOPK = 3
ROPE_DIMS = 32
ROPE_THETA = 500000.0
D_FF = 4 * D_MODEL
PLE_DIM = 256
RMS_EPS = 1e-6
LANES = 128
SUBLANES = 8
BF16_ROWS = 16
CONV_HALO = 8
REPACK_ROWS = 128
QK_SCALE = HEAD_DIM ** -0.5
MOBA_Q_SCALE = QK_SCALE * math.log2(math.e)
VMEM_LIMIT = 56 * 1024 * 1024

C_GQKV = 0
C_GZ = 3 * GDN_WIDTH
C_MQ = 4 * GDN_WIDTH
C_MK = C_MQ + MOBA_WIDTH
C_MV = C_MK + MOBA_WIDTH
C_BA = C_MV + MOBA_WIDTH
C_END = C_BA + LANES


def _rms(x, w):
    return x * lax.rsqrt(jnp.mean(x * x, axis=-1, keepdims=True) + RMS_EPS) * w


def _sigmoid(x):
    return 1.0 / (1.0 + jnp.exp(-x))


def _dot(a, b):
    return jnp.dot(a.astype(BF16), b.astype(BF16), preferred_element_type=F32)


def _dot_nt(a, b):
    return lax.dot_general(a.astype(BF16), b.astype(BF16), (((1,), (1,)), ((), ())),
                           preferred_element_type=F32)


def _dot_tn(a, b):
    return lax.dot_general(a.astype(BF16), b.astype(BF16), (((0,), (0,)), ((), ())),
                           preferred_element_type=F32)


def _in_proj_body(x_ref, nw_ref, win_ref, cw_ref, gp_ref, cos_ref, sa_ref, sb_ref,
                  gq_ref, gk_ref, gv_ref, gz_ref, bg_ref, mq_ref, mk_ref, mv_ref,
                  halo, w_ref, *, tm):
    i = pl.program_id(1)

    @pl.when((pl.program_id(0) == 0) & (i == 0))
    def _():
        n_gate = 2 * GDN_HEADS
        lane = lax.broadcasted_iota(jnp.int32, (REPACK_ROWS, LANES), 1)
        for r0 in range(0, D_MODEL, REPACK_ROWS):
            rows = slice(r0, r0 + REPACK_ROWS)
            w_ref[rows, 0:C_MQ] = win_ref[rows, 0:C_MQ].astype(BF16)
            w_ref[rows, C_MQ:C_BA] = win_ref[rows, C_MQ + n_gate:C_BA + n_gate].astype(BF16)
            gate_cols = win_ref[rows, C_MQ:C_MQ + LANES]
            w_ref[rows, C_BA:C_END] = jnp.where(lane < n_gate, gate_cols, 0.0).astype(BF16)

    u = _rms(x_ref[...], nw_ref[...]).astype(BF16)

    def proj(c0, width):
        return jnp.dot(u, w_ref[:, c0:c0 + width], preferred_element_type=F32)

    @pl.when(i == 0)
    def _():
        halo[...] = jnp.zeros_like(halo)

    pair_w = 2 * HEAD_DIM

    def conv_pair(p):
        pair = proj(C_GQKV + p * pair_w, pair_w)
        for s in (2 * p, 2 * p + 1):
            c0 = s * HEAD_DIM
            cur = pair[:, (s % 2) * HEAD_DIM:(s % 2 + 1) * HEAD_DIM]
            xs = jnp.concatenate([halo[:, c0:c0 + HEAD_DIM], cur], axis=0)
            halo[:, c0:c0 + HEAD_DIM] = cur[tm - CONV_HALO:, :]
            cw = [cw_ref[j:j + 1, c0:c0 + HEAD_DIM] for j in range(GDN_CONV)]
            x1 = pltpu.roll(xs, 1, 0)
            acc = (cw[3] * xs + cw[2] * x1) + pltpu.roll(cw[1] * xs + cw[0] * x1, 2, 0)
            acc = acc[CONV_HALO:, :]
            y = acc * _sigmoid(acc)
            which, h = divmod(s, GDN_HEADS)
            if which < 2:
                y = y * lax.rsqrt(jnp.sum(y * y, axis=-1, keepdims=True) + RMS_EPS)
                if which == 0:
                    y = y * QK_SCALE
            (gq_ref, gk_ref, gv_ref)[which][:, h * HEAD_DIM:(h + 1) * HEAD_DIM] = y.astype(BF16)

    def plain_pair(ref, c_base, p):
        ref[:, p * pair_w:(p + 1) * pair_w] = proj(c_base + p * pair_w, pair_w).astype(BF16)

    def rotary_pair(ref, c_base, scale, p):
        cos, sa, sb = cos_ref[...], sa_ref[...], sb_ref[...]
        half = ROPE_DIMS // 2
        pair = proj(c_base + p * pair_w, pair_w)
        for h in (2 * p, 2 * p + 1):
            xr = pair[:, (h % 2) * HEAD_DIM:(h % 2 + 1) * HEAD_DIM]
            rot = (xr * cos + pltpu.roll(xr, LANES - half, 1) * sa
                   + pltpu.roll(xr, half, 1) * sb)
            if scale is not None:
                rot = rot * scale
            ref[:, h * HEAD_DIM:(h + 1) * HEAD_DIM] = rot.astype(BF16)

    def gates():
        ba = proj(C_BA, LANES)
        lane = lax.broadcasted_iota(jnp.int32, (tm, LANES), 1)
        xg = ba + gp_ref[1:2, :]
        softplus = jnp.maximum(xg, 0.0) + jnp.log1p(jnp.exp(-jnp.abs(xg)))
        g = -jnp.exp(gp_ref[0:1, :]) * softplus
        bg_ref[...] = jnp.where(lane < GDN_HEADS, _sigmoid(ba),
                                jnp.where(lane < 2 * GDN_HEADS, g, 0.0))

    convs = [functools.partial(conv_pair, p) for p in range(3 * GDN_WIDTH // pair_w)]
    light = ([functools.partial(plain_pair, gz_ref, C_GZ, p) for p in range(GDN_WIDTH // pair_w)]
             + [functools.partial(plain_pair, mv_ref, C_MV, p) for p in range(MOBA_WIDTH // pair_w)]
             + [functools.partial(rotary_pair, mq_ref, C_MQ, MOBA_Q_SCALE, p)
                for p in range(MOBA_WIDTH // pair_w)]
             + [functools.partial(rotary_pair, mk_ref, C_MK, None, p)
                for p in range(MOBA_WIDTH // pair_w)]
             + [gates])
    done = 0
    for n, conv in enumerate(convs, start=1):
        conv()
        upto = len(light) * n // len(convs)
        for stage in light[done:upto]:
            stage()
        done = upto


def _in_proj(x, nw, w_in, conv_w, gparams, cos_t, sa_t, sb_t, *, tm):
    B, T, D = x.shape
    tok = lambda width: pl.BlockSpec((None, tm, width), lambda b, i: (b, i, 0))
    const = lambda shape: pl.BlockSpec(shape, lambda b, i: (0,) * len(shape))
    table = pl.BlockSpec((tm, LANES), lambda b, i: (i, 0))
    o512 = jax.ShapeDtypeStruct((B, T, GDN_WIDTH), BF16)
    return pl.pallas_call(
        functools.partial(_in_proj_body, tm=tm),
        grid=(B, T // tm),
        in_specs=[tok(D), const((1, D)),
                  pl.BlockSpec(w_in.shape, lambda b, i: (0, 0), pipeline_mode=pl.Buffered(1)),
                  const((GDN_CONV, 3 * GDN_WIDTH)),
                  const((2, LANES)), table, table, table],
        out_specs=[tok(GDN_WIDTH)] * 4 + [tok(LANES)] + [tok(MOBA_WIDTH)] * 3,
        out_shape=[o512] * 4 + [jax.ShapeDtypeStruct((B, T, LANES), F32)] + [o512] * 3,
        scratch_shapes=[pltpu.VMEM((CONV_HALO, 3 * GDN_WIDTH), F32),
                        pltpu.VMEM((D, C_END), BF16)],
        compiler_params=pltpu.CompilerParams(
            dimension_semantics=("arbitrary", "arbitrary"), vmem_limit_bytes=VMEM_LIMIT),
        name="in_proj",
    )(x, nw, w_in, conv_w, gparams, cos_t, sa_t, sb_t)


def _gdn_body(q_ref, k_ref, v_ref, z_ref, bg_ref, nw_ref, o_ref,
              s_ref, mneg_ref, n_ref, o1_ref, o2_ref, a_ref, snap_ref, *, T, seg):
    C = GDN_CHUNK
    H = GDN_HEADS
    ri = lax.broadcasted_iota(jnp.int32, (C, C), 0)
    ci = lax.broadcasted_iota(jnp.int32, (C, C), 1)
    causal = ci <= ri
    strict = ci < ri
    eye = (ri == ci).astype(F32)
    cps = seg // C
    cols = [slice(h * HEAD_DIM, (h + 1) * HEAD_DIM) for h in range(H)]
    s_ref[...] = jnp.zeros_like(s_ref)

    def precompute(base, gi):
        inst = [(c, h) for c in range(GDN_GROUP) for h in range(H)]
        rng = range(len(inst))
        lcs = [gi * GDN_GROUP + c for c in range(GDN_GROUP)]
        r0s = [pl.multiple_of(base + lc * C, C) for lc in lcs]
        bgs = [bg_ref[pl.ds(r0, C), :] for r0 in r0s]
        q = [q_ref[pl.ds(r0s[c], C), cols[h]].astype(F32) for c, h in inst]
        k = [k_ref[pl.ds(r0s[c], C), cols[h]].astype(F32) for c, h in inst]
        v = [v_ref[pl.ds(r0s[c], C), cols[h]].astype(F32) for c, h in inst]
        beta = [bgs[c][:, h:h + 1] for c, h in inst]
        gc_col, gc_last, decay = [], [], []
        for c, h in inst:
            gb = jnp.broadcast_to(bgs[c][:, H + h:H + h + 1], (C, C))
            row = jnp.sum(jnp.where(ri <= ci, gb, 0.0), axis=0, keepdims=True)
            g_row = jnp.sum(jnp.where(ri == ci, gb, 0.0), axis=0, keepdims=True)
            col = jnp.sum(jnp.where(causal, jnp.broadcast_to(g_row, (C, C)), 0.0),
                          axis=1, keepdims=True)
            gc_col.append(col)
            gc_last.append(row[:, C - 1:C])
            decay.append(jnp.exp(jnp.where(causal, col - row, -jnp.inf)))
        kb = [k[t] * beta[t] for t in rng]
        eg = [jnp.exp(gc_col[t]) for t in rng]
        kq = [_dot_nt(jnp.concatenate([kb[t], q[t]], axis=0), k[t]) for t in rng]
        low = [jnp.where(strict, kq[t][:C] * decay[t], 0.0) for t in rng]
        intra = [jnp.where(causal, kq[t][C:] * decay[t], 0.0) for t in rng]
        inv = [eye - low[t] for t in rng]
        pw = [_dot(low[t], low[t]) for t in rng]
        for _ in range(4):
            r = [_dot(jnp.concatenate([pw[t], inv[t]], axis=0), pw[t]) for t in rng]
            inv = [inv[t] + r[t][C:] for t in rng]
            pw = [r[t][:C] for t in rng]
        inv = [inv[t] + _dot(inv[t], pw[t]) for t in rng]
        wu = [_dot(inv[t], jnp.concatenate([kb[t] * eg[t], v[t] * beta[t]], axis=-1))
              for t in rng]
        kd = [k[t] * jnp.exp(gc_last[t] - gc_col[t]) for t in rng]
        r = [_dot(jnp.concatenate([kd[t].T, intra[t]], axis=0), wu[t]) for t in rng]
        mn = [r[t][:HEAD_DIM] for t in rng]
        io = [r[t][HEAD_DIM:] for t in rng]
        for t, (c, h) in enumerate(inst):
            lr0 = pl.multiple_of(lcs[c] * C, C)
            mneg_ref[lcs[c], h] = (-mn[t][:, :HEAD_DIM]).astype(BF16)
            n_ref[lcs[c], h] = mn[t][:, HEAD_DIM:]
            o1_ref[pl.ds(lr0, C), cols[h]] = (q[t] * eg[t] - io[t][:, :HEAD_DIM]).astype(BF16)
            o2_ref[pl.ds(lr0, C), cols[h]] = io[t][:, HEAD_DIM:]
            a_ref[lcs[c], h] = jnp.broadcast_to(jnp.exp(gc_last[t]), (SUBLANES, HEAD_DIM))

    def recur(lc):
        hs = range(H)
        state = [s_ref[h] for h in hs]
        sb = [state[h].astype(BF16) for h in hs]
        ms = [jnp.dot(mneg_ref[lc, h], sb[h], preferred_element_type=F32) for h in hs]
        for h in hs:
            snap_ref[lc, h] = sb[h]
            s_ref[h] = state[h] * a_ref[lc, h, 0:1, :] + ms[h] + n_ref[lc, h]

    def emit(base, gi):
        inst = [(gi * GDN_GROUP + c, h) for c in range(GDN_GROUP) for h in range(H)]
        lr0s = [pl.multiple_of(lc * C, C) for lc, _ in inst]
        r0s = [pl.multiple_of(base + lc * C, C) for lc, _ in inst]
        os_ = [jnp.dot(o1_ref[pl.ds(lr0s[t], C), cols[h]], snap_ref[lc, h],
                       preferred_element_type=F32) for t, (lc, h) in enumerate(inst)]
        for t, (lc, h) in enumerate(inst):
            o = os_[t] + o2_ref[pl.ds(lr0s[t], C), cols[h]]
            z = z_ref[pl.ds(r0s[t], C), cols[h]].astype(F32)
            o_ref[pl.ds(r0s[t], C), cols[h]] = (_rms(o, nw_ref[...])
                                                * (z * _sigmoid(z))).astype(o_ref.dtype)

    for sg in range(T // seg):
        base = sg * seg

        def a_step(gi, carry, base=base):
            precompute(base, gi)
            return carry

        def b_step(lc, carry):
            recur(lc)
            return carry

        def c_step(gi, carry, base=base):
            emit(base, gi)
            return carry

        lax.fori_loop(0, cps // GDN_GROUP, a_step, 0)
        lax.fori_loop(0, cps, b_step, 0)
        lax.fori_loop(0, cps // GDN_GROUP, c_step, 0)


def _gdn(gq, gk, gv, gz, bg, nw):
    B, T, _ = gq.shape
    seg = next(s for s in (1024, 512, 256, GDN_CHUNK * GDN_GROUP) if T % s == 0)
    cps = seg // GDN_CHUNK
    tok = lambda width: pl.BlockSpec((None, T, width), lambda b: (b, 0, 0))
    return pl.pallas_call(
        functools.partial(_gdn_body, T=T, seg=seg),
        grid=(B,),
        in_specs=[tok(GDN_WIDTH)] * 4 + [tok(LANES), pl.BlockSpec((1, HEAD_DIM), lambda b: (0, 0))],
        out_specs=tok(GDN_WIDTH),
        out_shape=jax.ShapeDtypeStruct((B, T, GDN_WIDTH), BF16),
        scratch_shapes=[
            pltpu.VMEM((GDN_HEADS, HEAD_DIM, HEAD_DIM), F32),
            pltpu.VMEM((cps, GDN_HEADS, HEAD_DIM, HEAD_DIM), BF16),
            pltpu.VMEM((cps, GDN_HEADS, HEAD_DIM, HEAD_DIM), F32),
            pltpu.VMEM((seg, GDN_WIDTH), BF16),
            pltpu.VMEM((seg, GDN_WIDTH), F32),
            pltpu.VMEM((cps, GDN_HEADS, SUBLANES, HEAD_DIM), F32),
            pltpu.VMEM((cps, GDN_HEADS, HEAD_DIM, HEAD_DIM), BF16),
        ],
        compiler_params=pltpu.CompilerParams(
            dimension_semantics=("arbitrary",), vmem_limit_bytes=VMEM_LIMIT),
        name="gdn",
    )(gq, gk, gv, gz, bg, nw)


def _moba_body(q_ref, k_ref, v_ref, o_ref, vt_ref, *, T):
    BS = MOBA_BLOCK
    nb = T // BS
    nrow = -(-nb // SUBLANES) * SUBLANES
    neg = -jnp.inf
    means = [jnp.mean(k_ref[j * BS:(j + 1) * BS, :].astype(F32), axis=0, keepdims=True)
             for j in range(nb)]
    if nrow > nb:
        means.append(jnp.zeros((nrow - nb, HEAD_DIM), F32))
    kmean = jnp.concatenate(means, axis=0)
    for j in range(nb):
        vt_ref[0:HEAD_DIM, j * BS:(j + 1) * BS] = v_ref[j * BS:(j + 1) * BS, :].T
    vt_ref[HEAD_DIM:, :] = jnp.ones((BF16_ROWS, T), BF16)
    blk = lax.broadcasted_iota(jnp.int32, (nrow, BS), 0)
    key = lax.broadcasted_iota(jnp.int32, (BS, BS), 0)
    qry = lax.broadcasted_iota(jnp.int32, (BS, BS), 1)

    def block_scores(i):
        q = q_ref[i * BS:(i + 1) * BS, :]
        sel = None
        if i > MOBA_TOPK:
            gate = lax.dot_general(kmean, q.astype(F32), (((1,), (1,)), ((), ())),
                                   precision=lax.Precision.HIGHEST, preferred_element_type=F32)
            gate = jnp.where(blk < i, gate, neg)
            rank = jnp.zeros((nrow, BS), F32)
            for jp in range(i):
                gj = gate[jp:jp + 1, :]
                beats = (gj > gate) | ((gj == gate) & (blk > jp))
                rank = rank + jnp.where(beats, 1.0, 0.0)
            sel = jnp.where((rank < MOBA_TOPK) & (blk < i), 1.0, 0.0)
        scores = []
        for j in range(i + 1):
            s = _dot_nt(k_ref[j * BS:(j + 1) * BS, :], q)
            if j == i:
                s = jnp.where(key <= qry, s, neg)
            elif sel is not None:
                s = jnp.where(sel[j:j + 1, :] > 0.5, s, neg)
            scores.append(s)
        return scores

    def block_output(i, scores):
        m = scores[0].max(axis=0, keepdims=True)
        for s in scores[1:]:
            m = jnp.maximum(m, s.max(axis=0, keepdims=True))
        acc = jnp.zeros((HEAD_DIM + BF16_ROWS, BS), F32)
        for j, s in enumerate(scores):
            acc = acc + _dot(vt_ref[:, j * BS:(j + 1) * BS], jnp.exp2(s - m))
        out = acc[:HEAD_DIM, :] / acc[HEAD_DIM:HEAD_DIM + 1, :]
        o_ref[i * BS:(i + 1) * BS, :] = out.T.astype(o_ref.dtype)

    scores = block_scores(0)
    for i in range(nb):
        following = block_scores(i + 1) if i + 1 < nb else None
        block_output(i, scores)
        scores = following


def _moba(mq, mk, mv):
    B, T, _ = mq.shape
    assert T % MOBA_BLOCK == 0
    head = pl.BlockSpec((None, T, HEAD_DIM), lambda b, h: (b, 0, h))
    return pl.pallas_call(
        functools.partial(_moba_body, T=T),
        grid=(B, MOBA_HEADS),
        in_specs=[head] * 3,
        out_specs=head,
        out_shape=jax.ShapeDtypeStruct((B, T, MOBA_WIDTH), BF16),
        scratch_shapes=[pltpu.VMEM((HEAD_DIM + BF16_ROWS, T), BF16)],
        compiler_params=pltpu.CompilerParams(
            dimension_semantics=("arbitrary", "arbitrary"), vmem_limit_bytes=VMEM_LIMIT),
        name="moba",
    )(mq, mk, mv)


def _mlp_body(og_ref, om_ref, x_ref, p_ref, wo_ref, n1_ref, n2_ref, wu_ref, wd_ref, n3_ref,
              wg_ref, wp_ref, o_ref, *, ff_chunk):
    mix = (_dot(og_ref[...], wo_ref[0:GDN_WIDTH, :])
           + _dot(om_ref[...], wo_ref[GDN_WIDTH:GDN_WIDTH + MOBA_WIDTH, :]))
    h = x_ref[...] + _rms(mix, n1_ref[...])
    a = _rms(h, n2_ref[...]).astype(BF16)
    f = None
    for c0 in range(0, D_FF, ff_chunk):
        up = jnp.dot(a, wu_ref[:, c0:c0 + ff_chunk], preferred_element_type=F32)
        r = jnp.square(jnp.maximum(up, 0.0))
        part = _dot(r, wd_ref[c0:c0 + ff_chunk, :])
        f = part if f is None else f + part
    h = h + _rms(f, n3_ref[...])
    gate = _sigmoid(_dot(h, wg_ref[...]))
    o_ref[...] = h + gate * _dot(p_ref[...], wp_ref[...])


def _mlp(og, om, x, p, wo, n1, n2, wu, wd, n3, wg, wp, *, tm):
    B, T, D = x.shape
    tok = lambda width: pl.BlockSpec((None, tm, width), lambda b, i: (b, i, 0))
    const = lambda shape: pl.BlockSpec(shape, lambda b, i: (0,) * len(shape),
                                       pipeline_mode=pl.Buffered(1))
    return pl.pallas_call(
        functools.partial(_mlp_body, ff_chunk=1024),
        grid=(B, T // tm),
        in_specs=[tok(GDN_WIDTH), tok(MOBA_WIDTH), tok(D), tok(PLE_DIM),
                  const((D, D)), const((1, D)), const((1, D)), const((D, D_FF)),
                  const((D_FF, D)), const((1, D)), const((D, D)), const((PLE_DIM, D))],
        out_specs=tok(D),
        out_shape=jax.ShapeDtypeStruct((B, T, D), F32),
        compiler_params=pltpu.CompilerParams(
            dimension_semantics=("arbitrary", "arbitrary"), vmem_limit_bytes=VMEM_LIMIT),
        name="mlp",
    )(og, om, x, p, wo, n1, n2, wu, wd, n3, wg, wp)


def _rope_tables(T):
    half = ROPE_DIMS // 2
    inv_freq = ROPE_THETA ** (-jnp.arange(half, dtype=F32) * (2.0 / ROPE_DIMS))
    ang = jnp.arange(T, dtype=jnp.int32).astype(F32)[:, None] * inv_freq[None, :]
    cos, sin = jnp.cos(ang), jnp.sin(ang)
    zeros = jnp.zeros((T, HEAD_DIM - ROPE_DIMS), F32)
    z_half = jnp.zeros((T, half), F32)
    cos_t = jnp.concatenate([cos, cos, jnp.ones((T, HEAD_DIM - ROPE_DIMS), F32)], axis=-1)
    sa_t = jnp.concatenate([-sin, z_half, zeros], axis=-1)
    sb_t = jnp.concatenate([z_half, sin, zeros], axis=-1)
    return cos_t, sa_t, sb_t


def _layer(h, p_i, w_in, conv_w, a_log, dt_bias, gdn_norm_w, w_out, attn_pre_norm,
           attn_post_norm, mlp_pre_norm, mlp_post_norm, w_up, w_down, w_ple, w_ple_gate, tables):
    B, T, D = h.shape
    tm = min(512, T)
    assert T % tm == 0 and T % MOBA_BLOCK == 0 and T % GDN_CHUNK == 0
    nh = 2 * GDN_HEADS
    gparams = jnp.zeros((2, LANES), F32)
    gparams = gparams.at[0, GDN_HEADS:nh].set(a_log.astype(F32))
    gparams = gparams.at[1, GDN_HEADS:nh].set(dt_bias.astype(F32))
    row = lambda v: v.reshape(1, -1).astype(F32)

    gq, gk, gv, gz, bg, mq, mk, mv = _in_proj(
        h, row(attn_pre_norm), w_in.astype(F32), conv_w.astype(F32), gparams, *tables, tm=tm)
    o_gdn = _gdn(gq, gk, gv, gz, bg, row(gdn_norm_w))
    o_moba = _moba(mq, mk, mv)
    return _mlp(o_gdn, o_moba, h, p_i, w_out.astype(BF16), row(attn_post_norm),
                row(mlp_pre_norm), w_up.astype(BF16), w_down.astype(BF16),
                row(mlp_post_norm), w_ple_gate.astype(BF16), w_ple.astype(BF16), tm=tm)


def kernel(x, p, w_in, conv_w, a_log, dt_bias, gdn_norm_w, w_out, attn_pre_norm, attn_post_norm,
           mlp_pre_norm, mlp_post_norm, w_up, w_down, w_ple, w_ple_gate):
    tables = _rope_tables(x.shape[1])
    h = x
    for i in range(w_in.shape[0]):
        h = _layer(h, p[i], w_in[i], conv_w[i], a_log[i], dt_bias[i], gdn_norm_w[i], w_out[i],
                   attn_pre_norm[i], attn_post_norm[i], mlp_pre_norm[i], mlp_post_norm[i],
                   w_up[i], w_down[i], w_ple[i], w_ple_gate[i], tables)
    return h
```

```python
import functools
import math

import jax
import jax.numpy as jnp
from jax import lax
from jax.experimental import pallas as pl
from jax.experimental.pallas import tpu as pltpu

F32 = jnp.float32
BF16 = jnp.bfloat16

D_MODEL = 1024
HEAD_DIM = 128
GDN_HEADS = 4
GDN_WIDTH = GDN_HEADS * HEAD_DIM
GDN_CONV = 4
GDN_CHUNK = 64
GDN_GROUP = 4
GDN_RING = 3
MOBA_HEADS = 4
MOBA_WIDTH = MOBA_HEADS * HEAD_DIM
MOBA_BLOCK = 256
MOBA_TOPK = 3
ROPE_DIMS = 32
ROPE_THETA = 500000.0
D_FF = 4 * D_MODEL
PLE_DIM = 256
RMS_EPS = 1e-6
LANES = 128
SUBLANES = 8
BF16_ROWS = 16
CONV_HALO = 8
REPACK_ROWS = 128
QK_SCALE = HEAD_DIM ** -0.5
MOBA_Q_SCALE = QK_SCALE * math.log2(math.e)
VMEM_LIMIT = 56 * 1024 * 1024

C_GQKV = 0
C_GZ = 3 * GDN_WIDTH
C_MQ = 4 * GDN_WIDTH
C_MK = C_MQ + MOBA_WIDTH
C_MV = C_MK + MOBA_WIDTH
C_BA = C_MV + MOBA_WIDTH
C_END = C_BA + LANES


def _rms(x, w):
    return x * lax.rsqrt(jnp.mean(x * x, axis=-1, keepdims=True) + RMS_EPS) * w


def _sigmoid(x):
    return 1.0 / (1.0 + jnp.exp(-x))


def _aligned(x, m):
    return x if isinstance(x, int) else pl.multiple_of(x, m)


def _dot(a, b):
    return jnp.dot(a.astype(BF16), b.astype(BF16), preferred_element_type=F32)


def _dot_nt(a, b):
    return lax.dot_general(a.astype(BF16), b.astype(BF16), (((1,), (1,)), ((), ())),
                           preferred_element_type=F32)


def _in_proj_body(x_ref, nw_ref, win_ref, cw_ref, gp_ref, cos_ref, sa_ref, sb_ref,
                  gq_ref, gk_ref, gv_ref, gz_ref, bg_ref, mq_ref, mk_ref, mv_ref,
                  halo, w_ref, *, tm):
    i = pl.program_id(1)

    @pl.when((pl.program_id(0) == 0) & (i == 0))
    def _():
        n_gate = 2 * GDN_HEADS
        lane = lax.broadcasted_iota(jnp.int32, (REPACK_ROWS, LANES), 1)
        for r0 in range(0, D_MODEL, REPACK_ROWS):
            rows = slice(r0, r0 + REPACK_ROWS)
            w_ref[rows, 0:C_MQ] = win_ref[rows, 0:C_MQ].astype(BF16)
            w_ref[rows, C_MQ:C_BA] = win_ref[rows, C_MQ + n_gate:C_BA + n_gate].astype(BF16)
            gate_cols = win_ref[rows, C_MQ:C_MQ + LANES]
            w_ref[rows, C_BA:C_END] = jnp.where(lane < n_gate, gate_cols, 0.0).astype(BF16)

    u = _rms(x_ref[...], nw_ref[...]).astype(BF16)

    def proj(c0, width):
        return jnp.dot(u, w_ref[:, c0:c0 + width], preferred_element_type=F32)

    @pl.when(i == 0)
    def _():
        halo[...] = jnp.zeros_like(halo)

    pair_w = 2 * HEAD_DIM

    def conv_pair(p):
        pair = proj(C_GQKV + p * pair_w, pair_w)
        for s in (2 * p, 2 * p + 1):
            c0 = s * HEAD_DIM
            cur = pair[:, (s % 2) * HEAD_DIM:(s % 2 + 1) * HEAD_DIM]
            xs = jnp.concatenate([halo[:, c0:c0 + HEAD_DIM], cur], axis=0)
            halo[:, c0:c0 + HEAD_DIM] = cur[tm - CONV_HALO:, :]
            cw = [cw_ref[j:j + 1, c0:c0 + HEAD_DIM] for j in range(GDN_CONV)]
            x1 = pltpu.roll(xs, 1, 0)
            acc = (cw[3] * xs + cw[2] * x1) + pltpu.roll(cw[1] * xs + cw[0] * x1, 2, 0)
            acc = acc[CONV_HALO:, :]
            y = acc * _sigmoid(acc)
            which, h = divmod(s, GDN_HEADS)
            if which < 2:
                y = y * lax.rsqrt(jnp.sum(y * y, axis=-1, keepdims=True) + RMS_EPS)
                if which == 0:
                    y = y * QK_SCALE
            (gq_ref, gk_ref, gv_ref)[which][:, h * HEAD_DIM:(h + 1) * HEAD_DIM] = y.astype(BF16)

    def plain_pair(ref, c_base, p):
        ref[:, p * pair_w:(p + 1) * pair_w] = proj(c_base + p * pair_w, pair_w).astype(BF16)

    def rotary_pair(ref, c_base, scale, p):
        cos, sa, sb = cos_ref[...], sa_ref[...], sb_ref[...]
        half = ROPE_DIMS // 2
        pair = proj(c_base + p * pair_w, pair_w)
        for h in (2 * p, 2 * p + 1):
            xr = pair[:, (h % 2) * HEAD_DIM:(h % 2 + 1) * HEAD_DIM]
            rot = (xr * cos + pltpu.roll(xr, LANES - half, 1) * sa
                   + pltpu.roll(xr, half, 1) * sb)
            if scale is not None:
                rot = rot * scale
            ref[:, h * HEAD_DIM:(h + 1) * HEAD_DIM] = rot.astype(BF16)

    def gates():
        ba = proj(C_BA, LANES)
        lane = lax.broadcasted_iota(jnp.int32, (tm, LANES), 1)
        xg = ba + gp_ref[1:2, :]
        softplus = jnp.maximum(xg, 0.0) + jnp.log1p(jnp.exp(-jnp.abs(xg)))
        g = -jnp.exp(gp_ref[0:1, :]) * softplus
        bg_ref[...] = jnp.where(lane < GDN_HEADS, _sigmoid(ba),
                                jnp.where(lane < 2 * GDN_HEADS, g, 0.0))

    convs = [functools.partial(conv_pair, p) for p in range(3 * GDN_WIDTH // pair_w)]
    light = ([functools.partial(plain_pair, gz_ref, C_GZ, p) for p in range(GDN_WIDTH // pair_w)]
             + [functools.partial(plain_pair, mv_ref, C_MV, p) for p in range(MOBA_WIDTH // pair_w)]
             + [functools.partial(rotary_pair, mq_ref, C_MQ, MOBA_Q_SCALE, p)
                for p in range(MOBA_WIDTH // pair_w)]
             + [functools.partial(rotary_pair, mk_ref, C_MK, None, p)
                for p in range(MOBA_WIDTH // pair_w)]
             + [gates])
    done = 0
    for n, conv in enumerate(convs, start=1):
        conv()
        upto = len(light) * n // len(convs)
        for stage in light[done:upto]:
            stage()
        done = upto


def _in_proj(x, nw, w_in, layer, conv_w, gparams, cos_t, sa_t, sb_t, *, tm):
    B, T, D = x.shape
    tok = lambda width: pl.BlockSpec((None, tm, width), lambda b, i: (b, i, 0))
    const = lambda shape: pl.BlockSpec(shape, lambda b, i: (0,) * len(shape))
    table = pl.BlockSpec((tm, LANES), lambda b, i: (i, 0))
    o512 = jax.ShapeDtypeStruct((B, T, GDN_WIDTH), BF16)
    return pl.pallas_call(
        functools.partial(_in_proj_body, tm=tm),
        grid=(B, T // tm),
        in_specs=[tok(D), const((1, D)),
                  pl.BlockSpec((None,) + w_in.shape[1:], lambda b, i: (layer, 0, 0),
                               pipeline_mode=pl.Buffered(1)),
                  const((GDN_CONV, 3 * GDN_WIDTH)),
                  const((2, LANES)), table, table, table],
        out_specs=[tok(GDN_WIDTH)] * 4 + [tok(LANES)] + [tok(MOBA_WIDTH)] * 3,
        out_shape=[o512] * 4 + [jax.ShapeDtypeStruct((B, T, LANES), F32)] + [o512] * 3,
        scratch_shapes=[pltpu.VMEM((CONV_HALO, 3 * GDN_WIDTH), F32),
                        pltpu.VMEM((D, C_END), BF16)],
        compiler_params=pltpu.CompilerParams(
            dimension_semantics=("arbitrary", "arbitrary"), vmem_limit_bytes=VMEM_LIMIT),
        name="in_proj",
    )(x, nw, w_in, conv_w, gparams, cos_t, sa_t, sb_t)


def _gdn_body(q_ref, k_ref, v_ref, z_ref, bg_ref, nw_ref, o_ref,
              s_ref, mneg_ref, n_ref, o1_ref, o2_ref, a_ref, snap_ref, *, T):
    C = GDN_CHUNK
    H = GDN_HEADS
    G = GDN_GROUP
    rows_per_trip = G * C
    ntrips = T // rows_per_trip
    ri = lax.broadcasted_iota(jnp.int32, (C, C), 0)
    ci = lax.broadcasted_iota(jnp.int32, (C, C), 1)
    causal = ci <= ri
    strict = ci < ri
    eye = (ri == ci).astype(F32)
    cols = [slice(h * HEAD_DIM, (h + 1) * HEAD_DIM) for h in range(H)]
    s_ref[...] = jnp.zeros_like(s_ref)

    def phase_a(row0, slot):
        inst = [(c, h) for c in range(G) for h in range(H)]
        rng = range(len(inst))
        r0s = [_aligned(row0 + c * C, C) for c in range(G)]
        bgs = [bg_ref[pl.ds(r0, C), :] for r0 in r0s]
        q = [q_ref[pl.ds(r0s[c], C), cols[h]].astype(F32) for c, h in inst]
        k = [k_ref[pl.ds(r0s[c], C), cols[h]].astype(F32) for c, h in inst]
        v = [v_ref[pl.ds(r0s[c], C), cols[h]].astype(F32) for c, h in inst]
        beta = [bgs[c][:, h:h + 1] for c, h in inst]
        gc_col, gc_last, decay = [], [], []
        for c, h in inst:
            gb = jnp.broadcast_to(bgs[c][:, H + h:H + h + 1], (C, C))
            row = jnp.sum(jnp.where(ri <= ci, gb, 0.0), axis=0, keepdims=True)
            g_row = jnp.sum(jnp.where(ri == ci, gb, 0.0), axis=0, keepdims=True)
            col = jnp.sum(jnp.where(causal, jnp.broadcast_to(g_row, (C, C)), 0.0),
                          axis=1, keepdims=True)
            gc_col.append(col)
            gc_last.append(row[:, C - 1:C])
            decay.append(jnp.exp(jnp.where(causal, col - row, -jnp.inf)))
        kb = [k[t] * beta[t] for t in rng]
        eg = [jnp.exp(gc_col[t]) for t in rng]
        kq = [_dot_nt(jnp.concatenate([kb[t], q[t]], axis=0), k[t]) for t in rng]
        yield
        low = [jnp.where(strict, kq[t][:C] * decay[t], 0.0) for t in rng]
        intra = [jnp.where(causal, kq[t][C:] * decay[t], 0.0) for t in rng]
        inv = [eye - low[t] for t in rng]
        pw = [_dot(low[t], low[t]) for t in rng]
        yield
        for _ in range(4):
            r = [_dot(jnp.concatenate([pw[t], inv[t]], axis=0), pw[t]) for t in rng]
            inv = [inv[t] + r[t][C:] for t in rng]
            pw = [r[t][:C] for t in rng]
            yield
        inv = [inv[t] + _dot(inv[t], pw[t]) for t in rng]
        yield
        wu = [_dot(inv[t], jnp.concatenate([kb[t] * eg[t], v[t] * beta[t]], axis=-1))
              for t in rng]
        yield
        kd = [k[t] * jnp.exp(gc_last[t] - gc_col[t]) for t in rng]
        r = [_dot(jnp.concatenate([kd[t].T, intra[t]], axis=0), wu[t]) for t in rng]
        mn = [r[t][:HEAD_DIM] for t in rng]
        io = [r[t][HEAD_DIM:] for t in rng]
        for t, (c, h) in enumerate(inst):
            rows = slice(c * C, (c + 1) * C)
            mneg_ref[slot, c, h] = (-mn[t][:, :HEAD_DIM]).astype(BF16)
            n_ref[slot, c, h] = mn[t][:, HEAD_DIM:]
            o1_ref[slot, rows, cols[h]] = (q[t] * eg[t] - io[t][:, :HEAD_DIM]).astype(BF16)
            o2_ref[slot, rows, cols[h]] = io[t][:, HEAD_DIM:]
            a_ref[slot, c, h] = jnp.broadcast_to(jnp.exp(gc_last[t]), (SUBLANES, HEAD_DIM))
        yield

    def phase_b_step(slot, c):
        hs = range(H)
        state = [s_ref[h] for h in hs]
        sb = [state[h].astype(BF16) for h in hs]
        ms = [jnp.dot(mneg_ref[slot, c, h], sb[h], preferred_element_type=F32) for h in hs]
        for h in hs:
            snap_ref[slot, c, h] = sb[h]
            s_ref[h] = state[h] * a_ref[slot, c, h, 0:1, :] + ms[h] + n_ref[slot, c, h]

    def phase_c(row0, slot):
        inst = [(c, h) for c in range(G) for h in range(H)]
        os_ = [jnp.dot(o1_ref[slot, c * C:(c + 1) * C, cols[h]], snap_ref[slot, c, h],
                       preferred_element_type=F32) for c, h in inst]
        for t, (c, h) in enumerate(inst):
            r0 = _aligned(row0 + c * C, C)
            o = os_[t] + o2_ref[slot, c * C:(c + 1) * C, cols[h]]
            z = z_ref[pl.ds(r0, C), cols[h]].astype(F32)
            o_ref[pl.ds(r0, C), cols[h]] = (_rms(o, nw_ref[...])
                                            * (z * _sigmoid(z))).astype(o_ref.dtype)

    def trip(g, do_a, do_b, do_c):
        row0 = lambda d: _aligned((g - d) * rows_per_trip, rows_per_trip)
        slot = lambda d: (g - d) % GDN_RING
        if do_c:
            phase_c(row0(2), slot(2))
        b_steps = [functools.partial(phase_b_step, slot(1), c) for c in range(G)] if do_b else []
        if do_a:
            for n, _ in enumerate(phase_a(row0(0), slot(0))):
                if n % 2 == 0 and b_steps:
                    b_steps.pop(0)()
        for step in b_steps:
            step()

    def steady(g, carry):
        trip(g, True, True, True)
        return carry

    all_on = lambda g: g < ntrips and 1 <= g <= ntrips and 2 <= g <= ntrips + 1
    g = 0
    while g < ntrips + 2:
        if all_on(g):
            last = g
            while all_on(last + 1):
                last += 1
            lax.fori_loop(g, last + 1, steady, 0)
            g = last + 1
        else:
            trip(g, g < ntrips, 1 <= g <= ntrips, 2 <= g <= ntrips + 1)
            g += 1


def _gdn(gq, gk, gv, gz, bg, nw):
    B, T, _ = gq.shape
    rows_per_trip = GDN_GROUP * GDN_CHUNK
    assert T % rows_per_trip == 0
    tok = lambda width: pl.BlockSpec((None, T, width), lambda b: (b, 0, 0))
    per_chunk = (GDN_RING, GDN_GROUP, GDN_HEADS)
    return pl.pallas_call(
        functools.partial(_gdn_body, T=T),
        grid=(B,),
        in_specs=[tok(GDN_WIDTH)] * 4 + [tok(LANES), pl.BlockSpec((1, HEAD_DIM), lambda b: (0, 0))],
        out_specs=tok(GDN_WIDTH),
        out_shape=jax.ShapeDtypeStruct((B, T, GDN_WIDTH), BF16),
        scratch_shapes=[
            pltpu.VMEM((GDN_HEADS, HEAD_DIM, HEAD_DIM), F32),
            pltpu.VMEM(per_chunk + (HEAD_DIM, HEAD_DIM), BF16),
            pltpu.VMEM(per_chunk + (HEAD_DIM, HEAD_DIM), F32),
            pltpu.VMEM((GDN_RING, rows_per_trip, GDN_WIDTH), BF16),
            pltpu.VMEM((GDN_RING, rows_per_trip, GDN_WIDTH), F32),
            pltpu.VMEM(per_chunk + (SUBLANES, HEAD_DIM), F32),
            pltpu.VMEM(per_chunk + (HEAD_DIM, HEAD_DIM), BF16),
        ],
        compiler_params=pltpu.CompilerParams(
            dimension_semantics=("arbitrary",), vmem_limit_bytes=VMEM_LIMIT),
        name="gdn",
    )(gq, gk, gv, gz, bg, nw)


def _moba_body(q_ref, k_ref, v_ref, o_ref, vt_ref, *, T):
    BS = MOBA_BLOCK
    nb = T // BS
    nrow = -(-nb // SUBLANES) * SUBLANES
    neg = -jnp.inf
    means = [jnp.mean(k_ref[j * BS:(j + 1) * BS, :].astype(F32), axis=0, keepdims=True)
             for j in range(nb)]
    if nrow > nb:
        means.append(jnp.zeros((nrow - nb, HEAD_DIM), F32))
    kmean = jnp.concatenate(means, axis=0)
    for j in range(nb):
        vt_ref[0:HEAD_DIM, j * BS:(j + 1) * BS] = v_ref[j * BS:(j + 1) * BS, :].T
    vt_ref[HEAD_DIM:, :] = jnp.ones((BF16_ROWS, T), BF16)
    blk = lax.broadcasted_iota(jnp.int32, (nrow, BS), 0)
    key = lax.broadcasted_iota(jnp.int32, (BS, BS), 0)
    qry = lax.broadcasted_iota(jnp.int32, (BS, BS), 1)

    def block_scores(i):
        q = q_ref[i * BS:(i + 1) * BS, :]
        sel = None
        if i > MOBA_TOPK:
            gate = lax.dot_general(kmean, q.astype(F32), (((1,), (1,)), ((), ())),
                                   precision=lax.Precision.HIGHEST, preferred_element_type=F32)
            gate = jnp.where(blk < i, gate, neg)
            rank = jnp.zeros((nrow, BS), F32)
            for jp in range(i):
                gj = gate[jp:jp + 1, :]
                beats = (gj > gate) | ((gj == gate) & (blk > jp))
                rank = rank + jnp.where(beats, 1.0, 0.0)
            sel = jnp.where((rank < MOBA_TOPK) & (blk < i), 1.0, 0.0)
        scores = []
        for j in range(i + 1):
            s = _dot_nt(k_ref[j * BS:(j + 1) * BS, :], q)
            if j == i:
                s = jnp.where(key <= qry, s, neg)
            elif sel is not None:
                s = jnp.where(sel[j:j + 1, :] > 0.5, s, neg)
            scores.append(s)
        return scores

    def block_output(i, scores):
        m = scores[0].max(axis=0, keepdims=True)
        for s in scores[1:]:
            m = jnp.maximum(m, s.max(axis=0, keepdims=True))
        acc = jnp.zeros((HEAD_DIM + BF16_ROWS, BS), F32)
        for j, s in enumerate(scores):
            acc = acc + _dot(vt_ref[:, j * BS:(j + 1) * BS], jnp.exp2(s - m))
        out = acc[:HEAD_DIM, :] / acc[HEAD_DIM:HEAD_DIM + 1, :]
        o_ref[i * BS:(i + 1) * BS, :] = out.T.astype(o_ref.dtype)

    scores = block_scores(0)
    for i in range(nb):
        following = block_scores(i + 1) if i + 1 < nb else None
        block_output(i, scores)
        scores = following


def _moba(mq, mk, mv):
    B, T, _ = mq.shape
    assert T % MOBA_BLOCK == 0
    head = pl.BlockSpec((None, T, HEAD_DIM), lambda b, h: (b, 0, h))
    return pl.pallas_call(
        functools.partial(_moba_body, T=T),
        grid=(B, MOBA_HEADS),
        in_specs=[head] * 3,
        out_specs=head,
        out_shape=jax.ShapeDtypeStruct((B, T, MOBA_WIDTH), BF16),
        scratch_shapes=[pltpu.VMEM((HEAD_DIM + BF16_ROWS, T), BF16)],
        compiler_params=pltpu.CompilerParams(
            dimension_semantics=("arbitrary", "arbitrary"), vmem_limit_bytes=VMEM_LIMIT),
        name="moba",
    )(mq, mk, mv)


def _mlp_body(og_ref, om_ref, x_ref, p_ref, wo_ref, n1_ref, n2_ref, wu_ref, wd_ref, n3_ref,
              wg_ref, wp_ref, o_ref, *, ff_chunk):
    mix = (_dot(og_ref[...], wo_ref[0:GDN_WIDTH, :])
           + _dot(om_ref[...], wo_ref[GDN_WIDTH:GDN_WIDTH + MOBA_WIDTH, :]))
    h = x_ref[...] + _rms(mix, n1_ref[...])
    a = _rms(h, n2_ref[...]).astype(BF16)
    f = None
    for c0 in range(0, D_FF, ff_chunk):
        up = jnp.dot(a, wu_ref[:, c0:c0 + ff_chunk], preferred_element_type=F32)
        r = jnp.square(jnp.maximum(up, 0.0))
        part = _dot(r, wd_ref[c0:c0 + ff_chunk, :])
        f = part if f is None else f + part
    h = h + _rms(f, n3_ref[...])
    gate = _sigmoid(_dot(h, wg_ref[...]))
    o_ref[...] = h + gate * _dot(p_ref[...], wp_ref[...])


def _mlp(og, om, x, p, wo, n1, n2, wu, wd, n3, wg, wp, *, tm):
    B, T, D = x.shape
    tok = lambda width: pl.BlockSpec((None, tm, width), lambda b, i: (b, i, 0))
    const = lambda shape: pl.BlockSpec(shape, lambda b, i: (0,) * len(shape),
                                       pipeline_mode=pl.Buffered(1))
    return pl.pallas_call(
        functools.partial(_mlp_body, ff_chunk=1024),
        grid=(B, T // tm),
        in_specs=[tok(GDN_WIDTH), tok(MOBA_WIDTH), tok(D), tok(PLE_DIM),
                  const((D, D)), const((1, D)), const((1, D)), const((D, D_FF)),
                  const((D_FF, D)), const((1, D)), const((D, D)), const((PLE_DIM, D))],
        out_specs=tok(D),
        out_shape=jax.ShapeDtypeStruct((B, T, D), F32),
        compiler_params=pltpu.CompilerParams(
            dimension_semantics=("arbitrary", "arbitrary"), vmem_limit_bytes=VMEM_LIMIT),
        name="mlp",
    )(og, om, x, p, wo, n1, n2, wu, wd, n3, wg, wp)


def _rope_tables(T):
    half = ROPE_DIMS // 2
    inv_freq = ROPE_THETA ** (-jnp.arange(half, dtype=F32) * (2.0 / ROPE_DIMS))
    ang = jnp.arange(T, dtype=jnp.int32).astype(F32)[:, None] * inv_freq[None, :]
    cos, sin = jnp.cos(ang), jnp.sin(ang)
    zeros = jnp.zeros((T, HEAD_DIM - ROPE_DIMS), F32)
    z_half = jnp.zeros((T, half), F32)
    cos_t = jnp.concatenate([cos, cos, jnp.ones((T, HEAD_DIM - ROPE_DIMS), F32)], axis=-1)
    sa_t = jnp.concatenate([-sin, z_half, zeros], axis=-1)
    sb_t = jnp.concatenate([z_half, sin, zeros], axis=-1)
    return cos_t, sa_t, sb_t


def _layer(h, p_i, w_in, layer, conv_w, a_log, dt_bias, gdn_norm_w, w_out, attn_pre_norm,
           attn_post_norm, mlp_pre_norm, mlp_post_norm, w_up, w_down, w_ple, w_ple_gate, tables):
    B, T, D = h.shape
    tm = min(512, T)
    assert T % tm == 0 and T % MOBA_BLOCK == 0 and T % GDN_CHUNK == 0
    nh = 2 * GDN_HEADS
    gparams = jnp.zeros((2, LANES), F32)
    gparams = gparams.at[0, GDN_HEADS:nh].set(a_log.astype(F32))
    gparams = gparams.at[1, GDN_HEADS:nh].set(dt_bias.astype(F32))
    row = lambda v: v.reshape(1, -1).astype(F32)

    gq, gk, gv, gz, bg, mq, mk, mv = _in_proj(
        h, row(attn_pre_norm), w_in.astype(F32), layer, conv_w.astype(F32), gparams, *tables,
        tm=tm)
    o_gdn = _gdn(gq, gk, gv, gz, bg, row(gdn_norm_w))
    o_moba = _moba(mq, mk, mv)
    return _mlp(o_gdn, o_moba, h, p_i, w_out.astype(BF16), row(attn_post_norm),
                row(mlp_pre_norm), w_up.astype(BF16), w_down.astype(BF16),
                row(mlp_post_norm), w_ple_gate.astype(BF16), w_ple.astype(BF16), tm=tm)


def kernel(x, p, w_in, conv_w, a_log, dt_bias, gdn_norm_w, w_out, attn_pre_norm, attn_post_norm,
           mlp_pre_norm, mlp_post_norm, w_up, w_down, w_ple, w_ple_gate):
    tables = _rope_tables(x.shape[1])
    h = x
    for i in range(w_in.shape[0]):
        h = _layer(h, p[i], w_in, i, conv_w[i], a_log[i], dt_bias[i], gdn_norm_w[i], w_out[i],
                   attn_pre_norm[i], attn_post_norm[i], mlp_pre_norm[i], mlp_post_norm[i],
                   w_up[i], w_down[i], w_ple[i], w_ple_gate[i], tables)
    return h
```

```python
import functools
import math

import jax
import jax.numpy as jnp
from jax import lax
from jax.experimental import pallas as pl
from jax.experimental.pallas import tpu as pltpu

F32 = jnp.float32
BF16 = jnp.bfloat16

D_MODEL = 1024
HEAD_DIM = 128
GDN_HEADS = 4
GDN_WIDTH = GDN_HEADS * HEAD_DIM
GDN_CONV = 4
GDN_CHUNK = 64
GDN_GROUP = 4
GDN_RING = 3
MOBA_HEADS = 4
MOBA_WIDTH = MOBA_HEADS * HEAD_DIM
MOBA_BLOCK = 256
MOBA_TOPK = 3
MOBA_HEADS_PER_STEP = 2
ROPE_DIMS = 32
ROPE_THETA = 500000.0
D_FF = 4 * D_MODEL
PLE_DIM = 256
RMS_EPS = 1e-6
LANES = 128
SUBLANES = 8
BF16_ROWS = 16
CONV_HALO = 8
REPACK_ROWS = 128
QK_SCALE = HEAD_DIM ** -0.5
MOBA_Q_SCALE = QK_SCALE * math.log2(math.e)
VMEM_LIMIT = 56 * 1024 * 1024

C_GQKV = 0
C_GZ = 3 * GDN_WIDTH
C_MQ = 4 * GDN_WIDTH
C_MK = C_MQ + MOBA_WIDTH
C_MV = C_MK + MOBA_WIDTH
C_BA = C_MV + MOBA_WIDTH
C_END = C_BA + LANES


def _rms(x, w):
    return x * lax.rsqrt(jnp.mean(x * x, axis=-1, keepdims=True) + RMS_EPS) * w


def _sigmoid(x):
    return 1.0 / (1.0 + jnp.exp(-x))


def _aligned(x, m):
    return x if isinstance(x, int) else pl.multiple_of(x, m)


def _dot(a, b):
    return jnp.dot(a.astype(BF16), b.astype(BF16), preferred_element_type=F32)


def _dot_nt(a, b):
    return lax.dot_general(a.astype(BF16), b.astype(BF16), (((1,), (1,)), ((), ())),
                           preferred_element_type=F32)


def _in_proj_body(x_ref, nw_ref, win_ref, cw_ref, gp_ref, cos_ref, sa_ref, sb_ref,
                  gq_ref, gk_ref, gv_ref, gz_ref, bg_ref, mq_ref, mk_ref, mv_ref,
                  halo, w_ref, *, tm):
    i = pl.program_id(1)

    @pl.when((pl.program_id(0) == 0) & (i == 0))
    def _():
        n_gate = 2 * GDN_HEADS
        lane = lax.broadcasted_iota(jnp.int32, (REPACK_ROWS, LANES), 1)
        for r0 in range(0, D_MODEL, REPACK_ROWS):
            rows = slice(r0, r0 + REPACK_ROWS)
            w_ref[rows, 0:C_MQ] = win_ref[rows, 0:C_MQ].astype(BF16)
            w_ref[rows, C_MQ:C_BA] = win_ref[rows, C_MQ + n_gate:C_BA + n_gate].astype(BF16)
            gate_cols = win_ref[rows, C_MQ:C_MQ + LANES]
            w_ref[rows, C_BA:C_END] = jnp.where(lane < n_gate, gate_cols, 0.0).astype(BF16)

    u = _rms(x_ref[...], nw_ref[...]).astype(BF16)

    def proj(c0, width):
        return jnp.dot(u, w_ref[:, c0:c0 + width], preferred_element_type=F32)

    @pl.when(i == 0)
    def _():
        halo[...] = jnp.zeros_like(halo)

    pair_w = 2 * HEAD_DIM

    def conv_pair(p):
        pair = proj(C_GQKV + p * pair_w, pair_w)
        for s in (2 * p, 2 * p + 1):
            c0 = s * HEAD_DIM
            cur = pair[:, (s % 2) * HEAD_DIM:(s % 2 + 1) * HEAD_DIM]
            xs = jnp.concatenate([halo[:, c0:c0 + HEAD_DIM], cur], axis=0)
            halo[:, c0:c0 + HEAD_DIM] = cur[tm - CONV_HALO:, :]
            cw = [cw_ref[j:j + 1, c0:c0 + HEAD_DIM] for j in range(GDN_CONV)]
            x1 = pltpu.roll(xs, 1, 0)
            acc = (cw[3] * xs + cw[2] * x1) + pltpu.roll(cw[1] * xs + cw[0] * x1, 2, 0)
            acc = acc[CONV_HALO:, :]
            y = acc * _sigmoid(acc)
            which, h = divmod(s, GDN_HEADS)
            if which < 2:
                inv_norm = lax.rsqrt(jnp.sum(y * y, axis=-1, keepdims=True) + RMS_EPS)
                y = y * (inv_norm * QK_SCALE if which == 0 else inv_norm)
            (gq_ref, gk_ref, gv_ref)[which][:, h * HEAD_DIM:(h + 1) * HEAD_DIM] = y.astype(BF16)

    def plain_pair(ref, c_base, p):
        ref[:, p * pair_w:(p + 1) * pair_w] = proj(c_base + p * pair_w, pair_w).astype(BF16)

    def rotary_pair(ref, c_base, scale, p):
        cos, sa, sb = cos_ref[...], sa_ref[...], sb_ref[...]
        half = ROPE_DIMS // 2
        pair = proj(c_base + p * pair_w, pair_w)
        for h in (2 * p, 2 * p + 1):
            xr = pair[:, (h % 2) * HEAD_DIM:(h % 2 + 1) * HEAD_DIM]
            rot = (xr * cos + pltpu.roll(xr, LANES - half, 1) * sa
                   + pltpu.roll(xr, half, 1) * sb)
            if scale is not None:
                rot = rot * scale
            ref[:, h * HEAD_DIM:(h + 1) * HEAD_DIM] = rot.astype(BF16)

    def gates():
        ba = proj(C_BA, LANES)
        lane = lax.broadcasted_iota(jnp.int32, (tm, LANES), 1)
        xg = ba + gp_ref[1:2, :]
        e = jnp.exp(-jnp.abs(xg))
        r = 1.0 / (1.0 + e)
        sig = jnp.where(xg >= 0.0, r, e * r)
        g = -jnp.exp(gp_ref[0:1, :]) * (jnp.maximum(xg, 0.0) + jnp.log1p(e))
        bg_ref[...] = jnp.where(lane < GDN_HEADS, sig,
                                jnp.where(lane < 2 * GDN_HEADS, g, 0.0))

    convs = [functools.partial(conv_pair, p) for p in range(3 * GDN_WIDTH // pair_w)]
    light = ([functools.partial(plain_pair, gz_ref, C_GZ, p) for p in range(GDN_WIDTH // pair_w)]
             + [functools.partial(plain_pair, mv_ref, C_MV, p) for p in range(MOBA_WIDTH // pair_w)]
             + [functools.partial(rotary_pair, mq_ref, C_MQ, MOBA_Q_SCALE, p)
                for p in range(MOBA_WIDTH // pair_w)]
             + [functools.partial(rotary_pair, mk_ref, C_MK, None, p)
                for p in range(MOBA_WIDTH // pair_w)]
             + [gates])
    done = 0
    for n, conv in enumerate(convs, start=1):
        conv()
        upto = len(light) * n // len(convs)
        for stage in light[done:upto]:
            stage()
        done = upto


def _in_proj(x, nw, w_in, layer, conv_w, gparams, cos_t, sa_t, sb_t, *, tm):
    B, T, D = x.shape
    tok = lambda width: pl.BlockSpec((None, tm, width), lambda b, i: (b, i, 0))
    const = lambda shape: pl.BlockSpec(shape, lambda b, i: (0,) * len(shape))
    table = pl.BlockSpec((tm, LANES), lambda b, i: (i, 0))
    o512 = jax.ShapeDtypeStruct((B, T, GDN_WIDTH), BF16)
    return pl.pallas_call(
        functools.partial(_in_proj_body, tm=tm),
        grid=(B, T // tm),
        in_specs=[tok(D), const((1, D)),
                  pl.BlockSpec((None,) + w_in.shape[1:], lambda b, i: (layer, 0, 0),
                               pipeline_mode=pl.Buffered(1)),
                  const((GDN_CONV, 3 * GDN_WIDTH)),
                  const((2, LANES)), table, table, table],
        out_specs=[tok(GDN_WIDTH)] * 4 + [tok(LANES)] + [tok(MOBA_WIDTH)] * 3,
        out_shape=[o512] * 4 + [jax.ShapeDtypeStruct((B, T, LANES), F32)] + [o512] * 3,
        scratch_shapes=[pltpu.VMEM((CONV_HALO, 3 * GDN_WIDTH), F32),
                        pltpu.VMEM((D, C_END), BF16)],
        compiler_params=pltpu.CompilerParams(
            dimension_semantics=("arbitrary", "arbitrary"), vmem_limit_bytes=VMEM_LIMIT),
        name="in_proj",
    )(x, nw, w_in, conv_w, gparams, cos_t, sa_t, sb_t)


def _gdn_body(q_ref, k_ref, v_ref, z_ref, bg_ref, nw_ref, o_ref,
              s_ref, mneg_ref, n_ref, o1_ref, o2_ref, a_ref, snap_ref, *, T):
    C = GDN_CHUNK
    H = GDN_HEADS
    G = GDN_GROUP
    rows_per_trip = G * C
    ntrips = T // rows_per_trip
    ri = lax.broadcasted_iota(jnp.int32, (C, C), 0)
    ci = lax.broadcasted_iota(jnp.int32, (C, C), 1)
    causal = ci <= ri
    strict = ci < ri
    eye = (ri == ci).astype(F32)
    cols = [slice(h * HEAD_DIM, (h + 1) * HEAD_DIM) for h in range(H)]
    s_ref[...] = jnp.zeros_like(s_ref)

    def phase_a(row0, slot):
        inst = [(c, h) for c in range(G) for h in range(H)]
        rng = range(len(inst))
        r0s = [_aligned(row0 + c * C, C) for c in range(G)]
        bgs = [bg_ref[pl.ds(r0, C), :] for r0 in r0s]
        q = [q_ref[pl.ds(r0s[c], C), cols[h]].astype(F32) for c, h in inst]
        k = [k_ref[pl.ds(r0s[c], C), cols[h]].astype(F32) for c, h in inst]
        v = [v_ref[pl.ds(r0s[c], C), cols[h]].astype(F32) for c, h in inst]
        beta = [bgs[c][:, h:h + 1] for c, h in inst]
        gc_col, gc_last, decay = [], [], []
        for c, h in inst:
            gb = jnp.broadcast_to(bgs[c][:, H + h:H + h + 1], (C, C))
            row = jnp.sum(jnp.where(ri <= ci, gb, 0.0), axis=0, keepdims=True)
            g_row = jnp.sum(jnp.where(ri == ci, gb, 0.0), axis=0, keepdims=True)
            col = jnp.sum(jnp.where(causal, jnp.broadcast_to(g_row, (C, C)), 0.0),
                          axis=1, keepdims=True)
            gc_col.append(col)
            gc_last.append(row[:, C - 1:C])
            decay.append(jnp.exp(jnp.where(causal, col - row, -jnp.inf)))
        kb = [k[t] * beta[t] for t in rng]
        eg = [jnp.exp(gc_col[t]) for t in rng]
        kq = [_dot_nt(jnp.concatenate([kb[t], q[t]], axis=0), k[t]) for t in rng]
        yield
        low = [jnp.where(strict, kq[t][:C] * decay[t], 0.0) for t in rng]
        intra = [jnp.where(causal, kq[t][C:] * decay[t], 0.0) for t in rng]
        inv = [eye - low[t] for t in rng]
        pw = [_dot(low[t], low[t]) for t in rng]
        yield
        for _ in range(4):
            r = [_dot(jnp.concatenate([pw[t], inv[t]], axis=0), pw[t]) for t in rng]
            inv = [inv[t] + r[t][C:] for t in rng]
            pw = [r[t][:C] for t in rng]
            yield
        inv = [inv[t] + _dot(inv[t], pw[t]) for t in rng]
        yield
        wu = [_dot(inv[t], jnp.concatenate([kb[t] * eg[t], v[t] * beta[t]], axis=-1))
              for t in rng]
        yield
        kd = [k[t] * jnp.exp(gc_last[t] - gc_col[t]) for t in rng]
        r = [_dot(jnp.concatenate([kd[t].T, intra[t]], axis=0), wu[t]) for t in rng]
        mn = [r[t][:HEAD_DIM] for t in rng]
        io = [r[t][HEAD_DIM:] for t in rng]
        for t, (c, h) in enumerate(inst):
            rows = slice(c * C, (c + 1) * C)
            mneg_ref[slot, c, h] = (-mn[t][:, :HEAD_DIM]).astype(BF16)
            n_ref[slot, c, h] = mn[t][:, HEAD_DIM:]
            o1_ref[slot, rows, cols[h]] = (q[t] * eg[t] - io[t][:, :HEAD_DIM]).astype(BF16)
            o2_ref[slot, rows, cols[h]] = io[t][:, HEAD_DIM:]
            a_ref[slot, c, h] = jnp.broadcast_to(jnp.exp(gc_last[t]), (SUBLANES, HEAD_DIM))
        yield

    def phase_b_step(slot, c):
        hs = range(H)
        state = [s_ref[h] for h in hs]
        sb = [state[h].astype(BF16) for h in hs]
        ms = [jnp.dot(mneg_ref[slot, c, h], sb[h], preferred_element_type=F32) for h in hs]
        for h in hs:
            snap_ref[slot, c, h] = sb[h]
            s_ref[h] = state[h] * a_ref[slot, c, h, 0:1, :] + ms[h] + n_ref[slot, c, h]

    def phase_c(row0, slot):
        inst = [(c, h) for c in range(G) for h in range(H)]
        os_ = [jnp.dot(o1_ref[slot, c * C:(c + 1) * C, cols[h]], snap_ref[slot, c, h],
                       preferred_element_type=F32) for c, h in inst]
        for t, (c, h) in enumerate(inst):
            r0 = _aligned(row0 + c * C, C)
            o = os_[t] + o2_ref[slot, c * C:(c + 1) * C, cols[h]]
            z = z_ref[pl.ds(r0, C), cols[h]].astype(F32)
            o_ref[pl.ds(r0, C), cols[h]] = (_rms(o, nw_ref[...])
                                            * (z * _sigmoid(z))).astype(o_ref.dtype)

    def trip(g, do_a, do_b, do_c):
        row0 = lambda d: _aligned((g - d) * rows_per_trip, rows_per_trip)
        slot = lambda d: (g - d) % GDN_RING
        if do_c:
            phase_c(row0(2), slot(2))
        b_steps = [functools.partial(phase_b_step, slot(1), c) for c in range(G)] if do_b else []
        if do_a:
            for n, _ in enumerate(phase_a(row0(0), slot(0))):
                if n % 2 == 0 and b_steps:
                    b_steps.pop(0)()
        for step in b_steps:
            step()

    def steady(g, carry):
        trip(g, True, True, True)
        return carry

    all_on = lambda g: g < ntrips and 1 <= g <= ntrips and 2 <= g <= ntrips + 1
    g = 0
    while g < ntrips + 2:
        if all_on(g):
            last = g
            while all_on(last + 1):
                last += 1
            lax.fori_loop(g, last + 1, steady, 0)
            g = last + 1
        else:
            trip(g, g < ntrips, 1 <= g <= ntrips, 2 <= g <= ntrips + 1)
            g += 1


def _gdn(gq, gk, gv, gz, bg, nw):
    B, T, _ = gq.shape
    rows_per_trip = GDN_GROUP * GDN_CHUNK
    assert T % rows_per_trip == 0
    tok = lambda width: pl.BlockSpec((None, T, width), lambda b: (b, 0, 0))
    per_chunk = (GDN_RING, GDN_GROUP, GDN_HEADS)
    return pl.pallas_call(
        functools.partial(_gdn_body, T=T),
        grid=(B,),
        in_specs=[tok(GDN_WIDTH)] * 4 + [tok(LANES), pl.BlockSpec((1, HEAD_DIM), lambda b: (0, 0))],
        out_specs=tok(GDN_WIDTH),
        out_shape=jax.ShapeDtypeStruct((B, T, GDN_WIDTH), BF16),
        scratch_shapes=[
            pltpu.VMEM((GDN_HEADS, HEAD_DIM, HEAD_DIM), F32),
            pltpu.VMEM(per_chunk + (HEAD_DIM, HEAD_DIM), BF16),
            pltpu.VMEM(per_chunk + (HEAD_DIM, HEAD_DIM), F32),
            pltpu.VMEM((GDN_RING, rows_per_trip, GDN_WIDTH), BF16),
            pltpu.VMEM((GDN_RING, rows_per_trip, GDN_WIDTH), F32),
            pltpu.VMEM(per_chunk + (SUBLANES, HEAD_DIM), F32),
            pltpu.VMEM(per_chunk + (HEAD_DIM, HEAD_DIM), BF16),
        ],
        compiler_params=pltpu.CompilerParams(
            dimension_semantics=("arbitrary",), vmem_limit_bytes=VMEM_LIMIT),
        name="gdn",
    )(gq, gk, gv, gz, bg, nw)


def _moba_body(q_ref, k_ref, v_ref, o_ref, vt_ref, *, T):
    BS = MOBA_BLOCK
    nb = T // BS
    nrow = -(-nb // SUBLANES) * SUBLANES
    neg = -jnp.inf
    heads = [slice(h * HEAD_DIM, (h + 1) * HEAD_DIM) for h in range(MOBA_HEADS_PER_STEP)]
    blk = lax.broadcasted_iota(jnp.int32, (nrow, BS), 0)
    key = lax.broadcasted_iota(jnp.int32, (BS, BS), 0)
    qry = lax.broadcasted_iota(jnp.int32, (BS, BS), 1)

    kmean = []
    for n, hd in enumerate(heads):
        means = [jnp.mean(k_ref[j * BS:(j + 1) * BS, hd].astype(F32), axis=0, keepdims=True)
                 for j in range(nb)]
        if nrow > nb:
            means.append(jnp.zeros((nrow - nb, HEAD_DIM), F32))
        kmean.append(jnp.concatenate(means, axis=0))
        for j in range(nb):
            vt_ref[n, 0:HEAD_DIM, j * BS:(j + 1) * BS] = v_ref[j * BS:(j + 1) * BS, hd].T
        vt_ref[n, HEAD_DIM:, :] = jnp.ones((BF16_ROWS, T), BF16)

    def block_scores(n, i):
        q = q_ref[i * BS:(i + 1) * BS, heads[n]]
        sel = None
        if i > MOBA_TOPK:
            gate = lax.dot_general(kmean[n], q.astype(F32), (((1,), (1,)), ((), ())),
                                   precision=lax.Precision.HIGHEST, preferred_element_type=F32)
            gate = jnp.where(blk < i, gate, neg)
            rank = jnp.zeros((nrow, BS), F32)
            for jp in range(i):
                gj = gate[jp:jp + 1, :]
                beats = (gj > gate) | ((gj == gate) & (blk > jp))
                rank = rank + jnp.where(beats, 1.0, 0.0)
            sel = jnp.where((rank < MOBA_TOPK) & (blk < i), 1.0, 0.0)
        scores = []
        for j in range(i + 1):
            s = _dot_nt(k_ref[j * BS:(j + 1) * BS, heads[n]], q)
            if j == i:
                s = jnp.where(key <= qry, s, neg)
            elif sel is not None:
                s = jnp.where(sel[j:j + 1, :] > 0.5, s, neg)
            scores.append(s)
        return scores

    def block_output(n, i, scores):
        m = scores[0].max(axis=0, keepdims=True)
        for s in scores[1:]:
            m = jnp.maximum(m, s.max(axis=0, keepdims=True))
        acc = jnp.zeros((HEAD_DIM + BF16_ROWS, BS), F32)
        for j, s in enumerate(scores):
            acc = acc + _dot(vt_ref[n, :, j * BS:(j + 1) * BS], jnp.exp2(s - m))
        out = acc[:HEAD_DIM, :] / acc[HEAD_DIM:HEAD_DIM + 1, :]
        o_ref[i * BS:(i + 1) * BS, heads[n]] = out.T.astype(o_ref.dtype)

    hs = range(MOBA_HEADS_PER_STEP)
    scores = [block_scores(n, 0) for n in hs]
    for i in range(nb):
        following = [block_scores(n, i + 1) for n in hs] if i + 1 < nb else None
        for n in hs:
            block_output(n, i, scores[n])
        scores = following


def _moba(mq, mk, mv):
    B, T, _ = mq.shape
    assert T % MOBA_BLOCK == 0 and MOBA_HEADS % MOBA_HEADS_PER_STEP == 0
    width = MOBA_HEADS_PER_STEP * HEAD_DIM
    head = pl.BlockSpec((None, T, width), lambda b, h: (b, 0, h))
    return pl.pallas_call(
        functools.partial(_moba_body, T=T),
        grid=(B, MOBA_HEADS // MOBA_HEADS_PER_STEP),
        in_specs=[head] * 3,
        out_specs=head,
        out_shape=jax.ShapeDtypeStruct((B, T, MOBA_WIDTH), BF16),
        scratch_shapes=[pltpu.VMEM((MOBA_HEADS_PER_STEP, HEAD_DIM + BF16_ROWS, T), BF16)],
        compiler_params=pltpu.CompilerParams(
            dimension_semantics=("arbitrary", "arbitrary"), vmem_limit_bytes=VMEM_LIMIT),
        name="moba",
    )(mq, mk, mv)


def _mlp_body(og_ref, om_ref, x_ref, p_ref, wo_ref, n1_ref, n2_ref, wu_ref, wd_ref, n3_ref,
              wg_ref, wp_ref, o_ref, *, ff_chunk):
    tm = x_ref.shape[0]
    halves = [slice(r0, r0 + tm // 2) for r0 in (0, tm // 2)]
    mix = [_dot(og_ref[r, :], wo_ref[0:GDN_WIDTH, :])
           + _dot(om_ref[r, :], wo_ref[GDN_WIDTH:GDN_WIDTH + MOBA_WIDTH, :]) for r in halves]
    h = [x_ref[r, :] + _rms(mix[n], n1_ref[...]) for n, r in enumerate(halves)]
    a = [_rms(h[n], n2_ref[...]).astype(BF16) for n in range(2)]
    f = [None, None]
    for c0 in range(0, D_FF, ff_chunk):
        up = [jnp.dot(a[n], wu_ref[:, c0:c0 + ff_chunk], preferred_element_type=F32)
              for n in range(2)]
        for n in range(2):
            part = _dot(jnp.square(jnp.maximum(up[n], 0.0)), wd_ref[c0:c0 + ff_chunk, :])
            f[n] = part if f[n] is None else f[n] + part
    h = [h[n] + _rms(f[n], n3_ref[...]) for n in range(2)]
    gate = [_dot(h[n], wg_ref[...]) for n in range(2)]
    ple = [_dot(p_ref[r, :], wp_ref[...]) for r in halves]
    for n, r in enumerate(halves):
        o_ref[r, :] = h[n] + _sigmoid(gate[n]) * ple[n]


def _mlp(og, om, x, p, wo, n1, n2, wu, wd, n3, wg, wp, *, tm):
    B, T, D = x.shape
    tok = lambda width: pl.BlockSpec((None, tm, width), lambda b, i: (b, i, 0))
    const = lambda shape: pl.BlockSpec(shape, lambda b, i: (0,) * len(shape),
                                       pipeline_mode=pl.Buffered(1))
    return pl.pallas_call(
        functools.partial(_mlp_body, ff_chunk=1024),
        grid=(B, T // tm),
        in_specs=[tok(GDN_WIDTH), tok(MOBA_WIDTH), tok(D), tok(PLE_DIM),
                  const((D, D)), const((1, D)), const((1, D)), const((D, D_FF)),
                  const((D_FF, D)), const((1, D)), const((D, D)), const((PLE_DIM, D))],
        out_specs=tok(D),
        out_shape=jax.ShapeDtypeStruct((B, T, D), F32),
        compiler_params=pltpu.CompilerParams(
            dimension_semantics=("arbitrary", "arbitrary"), vmem_limit_bytes=VMEM_LIMIT),
        name="mlp",
    )(og, om, x, p, wo, n1, n2, wu, wd, n3, wg, wp)


def _rope_tables(T):
    half = ROPE_DIMS // 2
    inv_freq = ROPE_THETA ** (-jnp.arange(half, dtype=F32) * (2.0 / ROPE_DIMS))
    ang = jnp.arange(T, dtype=jnp.int32).astype(F32)[:, None] * inv_freq[None, :]
    cos, sin = jnp.cos(ang), jnp.sin(ang)
    zeros = jnp.zeros((T, HEAD_DIM - ROPE_DIMS), F32)
    z_half = jnp.zeros((T, half), F32)
    cos_t = jnp.concatenate([cos, cos, jnp.ones((T, HEAD_DIM - ROPE_DIMS), F32)], axis=-1)
    sa_t = jnp.concatenate([-sin, z_half, zeros], axis=-1)
    sb_t = jnp.concatenate([z_half, sin, zeros], axis=-1)
    return cos_t, sa_t, sb_t


def _layer(h, p_i, w_in, layer, conv_w, a_log, dt_bias, gdn_norm_w, w_out, attn_pre_norm,
           attn_post_norm, mlp_pre_norm, mlp_post_norm, w_up, w_down, w_ple, w_ple_gate, tables):
    B, T, D = h.shape
    tm = min(512, T)
    assert T % tm == 0 and T % MOBA_BLOCK == 0 and T % GDN_CHUNK == 0
    nh = 2 * GDN_HEADS
    gparams = jnp.zeros((2, LANES), F32)
    gparams = gparams.at[0, GDN_HEADS:nh].set(a_log.astype(F32))
    gparams = gparams.at[1, GDN_HEADS:nh].set(dt_bias.astype(F32))
    row = lambda v: v.reshape(1, -1).astype(F32)

    gq, gk, gv, gz, bg, mq, mk, mv = _in_proj(
        h, row(attn_pre_norm), w_in.astype(F32), layer, conv_w.astype(F32), gparams, *tables,
        tm=tm)
    o_gdn = _gdn(gq, gk, gv, gz, bg, row(gdn_norm_w))
    o_moba = _moba(mq, mk, mv)
    return _mlp(o_gdn, o_moba, h, p_i, w_out.astype(BF16), row(attn_post_norm),
                row(mlp_pre_norm), w_up.astype(BF16), w_down.astype(BF16),
                row(mlp_post_norm), w_ple_gate.astype(BF16), w_ple.astype(BF16), tm=tm)


def kernel(x, p, w_in, conv_w, a_log, dt_bias, gdn_norm_w, w_out, attn_pre_norm, attn_post_norm,
           mlp_pre_norm, mlp_post_norm, w_up, w_down, w_ple, w_ple_gate):
    tables = _rope_tables(x.shape[1])
    h = x
    for i in range(w_in.shape[0]):
        h = _layer(h, p[i], w_in, i, conv_w[i], a_log[i], dt_bias[i], gdn_norm_w[i], w_out[i],
                   attn_pre_norm[i], attn_post_norm[i], mlp_pre_norm[i], mlp_post_norm[i],
                   w_up[i], w_down[i], w_ple[i], w_ple_gate[i], tables)
    return h
```

```python
import functools
import math

import jax
import jax.numpy as jnp
import numpy as np
from jax import lax
from jax.experimental import pallas as pl
from jax.experimental.pallas import tpu as pltpu

F32 = jnp.float32
BF16 = jnp.bfloat16

D_MODEL = 1024
HEAD_DIM = 128
GDN_HEADS = 4
GDN_WIDTH = GDN_HEADS * HEAD_DIM
GDN_CONV = 4
GDN_CHUNK = 64
GDN_GROUP = 4
GDN_RING = 3
MOBA_HEADS = 4
MOBA_WIDTH = MOBA_HEADS * HEAD_DIM
MOBA_BLOCK = 256
MOBA_TOPK = 3
MOBA_HEADS_PER_STEP = 2
ROPE_DIMS = 32
ROPE_THETA = 500000.0
D_FF = 4 * D_MODEL
PLE_DIM = 256
RMS_EPS = 1e-6
LANES = 128
SUBLANES = 8
BF16_ROWS = 16
CONV_HALO = 8
REPACK_ROWS = 128
QK_SCALE = HEAD_DIM ** -0.5
MOBA_Q_SCALE = QK_SCALE * math.log2(math.e)
VMEM_LIMIT = 56 * 1024 * 1024

C_GQKV = 0
C_GZ = 3 * GDN_WIDTH
C_MQ = 4 * GDN_WIDTH
C_MK = C_MQ + MOBA_WIDTH
C_MV = C_MK + MOBA_WIDTH
C_BA = C_MV + MOBA_WIDTH
C_END = C_BA + LANES


def _rms(x, w):
    return x * lax.rsqrt(jnp.mean(x * x, axis=-1, keepdims=True) + RMS_EPS) * w


def _sigmoid(x):
    return 1.0 / (1.0 + jnp.exp(-x))


def _aligned(x, m):
    return x if isinstance(x, int) else pl.multiple_of(x, m)


def _dot(a, b):
    return jnp.dot(a.astype(BF16), b.astype(BF16), preferred_element_type=F32)


def _dot_nt(a, b):
    return lax.dot_general(a.astype(BF16), b.astype(BF16), (((1,), (1,)), ((), ())),
                           preferred_element_type=F32)


def _in_proj_body(x_ref, nw_ref, win_ref, cw_ref, gp_ref, cos_ref, sa_ref, sb_ref,
                  gq_ref, gk_ref, gv_ref, gz_ref, bg_ref, mq_ref, mk_ref, mv_ref,
                  halo, w_ref, *, tm):
    i = pl.program_id(1)

    @pl.when((pl.program_id(0) == 0) & (i == 0))
    def _():
        n_gate = 2 * GDN_HEADS
        for r0 in range(0, C_BA, REPACK_ROWS):
            src = r0 if r0 < C_MQ else r0 + n_gate
            w_ref[r0:r0 + REPACK_ROWS, :] = win_ref[src:src + REPACK_ROWS, :].astype(BF16)
        gate_rows = jnp.concatenate([win_ref[C_MQ:C_MQ + n_gate, :],
                                     jnp.zeros((LANES - n_gate, D_MODEL), F32)], axis=0)
        w_ref[C_BA:C_END, :] = gate_rows.astype(BF16)

    u = _rms(x_ref[...], nw_ref[...]).astype(BF16)

    def proj(c0, width):
        return _dot_nt(u, w_ref[c0:c0 + width, :])

    @pl.when(i == 0)
    def _():
        halo[...] = jnp.zeros_like(halo)

    pair_w = 2 * HEAD_DIM

    def conv_pair(p):
        pair = proj(C_GQKV + p * pair_w, pair_w)
        for s in (2 * p, 2 * p + 1):
            c0 = s * HEAD_DIM
            cur = pair[:, (s % 2) * HEAD_DIM:(s % 2 + 1) * HEAD_DIM]
            xs = jnp.concatenate([halo[:, c0:c0 + HEAD_DIM], cur], axis=0)
            halo[:, c0:c0 + HEAD_DIM] = cur[tm - CONV_HALO:, :]
            cw = [cw_ref[j:j + 1, c0:c0 + HEAD_DIM] for j in range(GDN_CONV)]
            x1 = pltpu.roll(xs, 1, 0)
            acc = (cw[3] * xs + cw[2] * x1) + pltpu.roll(cw[1] * xs + cw[0] * x1, 2, 0)
            acc = acc[CONV_HALO:, :]
            y = acc * _sigmoid(acc)
            which, h = divmod(s, GDN_HEADS)
            if which < 2:
                inv_norm = lax.rsqrt(jnp.sum(y * y, axis=-1, keepdims=True) + RMS_EPS)
                y = y * (inv_norm * QK_SCALE if which == 0 else inv_norm)
            (gq_ref, gk_ref, gv_ref)[which][:, h * HEAD_DIM:(h + 1) * HEAD_DIM] = y.astype(BF16)

    def plain_pair(ref, c_base, p):
        ref[:, p * pair_w:(p + 1) * pair_w] = proj(c_base + p * pair_w, pair_w).astype(BF16)

    def rotary_pair(ref, c_base, scale, p):
        cos, sa, sb = cos_ref[...], sa_ref[...], sb_ref[...]
        half = ROPE_DIMS // 2
        pair = proj(c_base + p * pair_w, pair_w)
        for h in (2 * p, 2 * p + 1):
            xr = pair[:, (h % 2) * HEAD_DIM:(h % 2 + 1) * HEAD_DIM]
            rot = (xr * cos + pltpu.roll(xr, LANES - half, 1) * sa
                   + pltpu.roll(xr, half, 1) * sb)
            if scale is not None:
                rot = rot * scale
            ref[:, h * HEAD_DIM:(h + 1) * HEAD_DIM] = rot.astype(BF16)

    def gates():
        ba = proj(C_BA, LANES)
        lane = lax.broadcasted_iota(jnp.int32, (tm, LANES), 1)
        xg = ba + gp_ref[1:2, :]
        e = jnp.exp(-jnp.abs(xg))
        r = 1.0 / (1.0 + e)
        sig = jnp.where(xg >= 0.0, r, e * r)
        g = -jnp.exp(gp_ref[0:1, :]) * (jnp.maximum(xg, 0.0) + jnp.log1p(e))
        bg_ref[...] = jnp.where(lane < GDN_HEADS, sig,
                                jnp.where(lane < 2 * GDN_HEADS, g, 0.0))

    convs = [functools.partial(conv_pair, p) for p in range(3 * GDN_WIDTH // pair_w)]
    light = ([functools.partial(plain_pair, gz_ref, C_GZ, p) for p in range(GDN_WIDTH // pair_w)]
             + [functools.partial(plain_pair, mv_ref, C_MV, p) for p in range(MOBA_WIDTH // pair_w)]
             + [functools.partial(rotary_pair, mq_ref, C_MQ, MOBA_Q_SCALE, p)
                for p in range(MOBA_WIDTH // pair_w)]
             + [functools.partial(rotary_pair, mk_ref, C_MK, None, p)
                for p in range(MOBA_WIDTH // pair_w)]
             + [gates])
    done = 0
    for n, conv in enumerate(convs, start=1):
        conv()
        upto = len(light) * n // len(convs)
        for stage in light[done:upto]:
            stage()
        done = upto


def _in_proj(x, nw, w_in, layer, conv_w, gparams, cos_t, sa_t, sb_t, *, tm):
    B, T, D = x.shape
    tok = lambda width: pl.BlockSpec((None, tm, width), lambda b, i: (b, i, 0))
    const = lambda shape: pl.BlockSpec(shape, lambda b, i: (0,) * len(shape))
    table = pl.BlockSpec((tm, LANES), lambda b, i: (i, 0))
    o512 = jax.ShapeDtypeStruct((B, T, GDN_WIDTH), BF16)
    return pl.pallas_call(
        functools.partial(_in_proj_body, tm=tm),
        grid=(B, T // tm),
        in_specs=[tok(D), const((1, D)),
                  pl.BlockSpec((None,) + w_in.shape[1:], lambda b, i: (layer, 0, 0),
                               pipeline_mode=pl.Buffered(1)),
                  const((GDN_CONV, 3 * GDN_WIDTH)),
                  const((2, LANES)), table, table, table],
        out_specs=[tok(GDN_WIDTH)] * 4 + [tok(LANES)] + [tok(MOBA_WIDTH)] * 3,
        out_shape=[o512] * 4 + [jax.ShapeDtypeStruct((B, T, LANES), F32)] + [o512] * 3,
        scratch_shapes=[pltpu.VMEM((CONV_HALO, 3 * GDN_WIDTH), F32),
                        pltpu.VMEM((C_END, D), BF16)],
        compiler_params=pltpu.CompilerParams(
            dimension_semantics=("arbitrary", "arbitrary"), vmem_limit_bytes=VMEM_LIMIT),
        name="in_proj",
    )(x, nw, w_in, conv_w, gparams, cos_t, sa_t, sb_t)


def _gdn_body(q_ref, k_ref, v_ref, z_ref, bg_ref, nw_ref, o_ref,
              s_ref, mneg_ref, n_ref, o1_ref, o2_ref, a_ref, snap_ref, *, T):
    C = GDN_CHUNK
    H = GDN_HEADS
    G = GDN_GROUP
    rows_per_trip = G * C
    ntrips = T // rows_per_trip
    ri = lax.broadcasted_iota(jnp.int32, (C, C), 0)
    ci = lax.broadcasted_iota(jnp.int32, (C, C), 1)
    causal = ci <= ri
    strict = ci < ri
    eye = (ri == ci).astype(F32)
    cols = [slice(h * HEAD_DIM, (h + 1) * HEAD_DIM) for h in range(H)]
    s_ref[...] = jnp.zeros_like(s_ref)

    def phase_a(row0, slot):
        inst = [(c, h) for c in range(G) for h in range(H)]
        rng = range(len(inst))
        r0s = [_aligned(row0 + c * C, C) for c in range(G)]
        bgs = [bg_ref[pl.ds(r0, C), :] for r0 in r0s]
        q = [q_ref[pl.ds(r0s[c], C), cols[h]].astype(F32) for c, h in inst]
        k = [k_ref[pl.ds(r0s[c], C), cols[h]].astype(F32) for c, h in inst]
        v = [v_ref[pl.ds(r0s[c], C), cols[h]].astype(F32) for c, h in inst]
        beta = [bgs[c][:, h:h + 1] for c, h in inst]
        gc_col, gc_last, decay = [], [], []
        for c, h in inst:
            gb = jnp.broadcast_to(bgs[c][:, H + h:H + h + 1], (C, C))
            row = jnp.sum(jnp.where(ri <= ci, gb, 0.0), axis=0, keepdims=True)
            g_row = jnp.sum(jnp.where(ri == ci, gb, 0.0), axis=0, keepdims=True)
            col = jnp.sum(jnp.where(causal, jnp.broadcast_to(g_row, (C, C)), 0.0),
                          axis=1, keepdims=True)
            gc_col.append(col)
            gc_last.append(row[:, C - 1:C])
            decay.append(jnp.exp(jnp.where(causal, col - row, -jnp.inf)))
        kb = [k[t] * beta[t] for t in rng]
        eg = [jnp.exp(gc_col[t]) for t in rng]
        kq = [_dot_nt(jnp.concatenate([kb[t], q[t]], axis=0), k[t]) for t in rng]
        yield
        low = [jnp.where(strict, kq[t][:C] * decay[t], 0.0) for t in rng]
        intra = [jnp.where(causal, kq[t][C:] * decay[t], 0.0) for t in rng]
        inv = [eye - low[t] for t in rng]
        pw = [_dot(low[t], low[t]) for t in rng]
        yield
        for _ in range(4):
            r = [_dot(jnp.concatenate([pw[t], inv[t]], axis=0), pw[t]) for t in rng]
            inv = [inv[t] + r[t][C:] for t in rng]
            pw = [r[t][:C] for t in rng]
            yield
        inv = [inv[t] + _dot(inv[t], pw[t]) for t in rng]
        yield
        wu = [_dot(inv[t], jnp.concatenate([kb[t] * eg[t], v[t] * beta[t]], axis=-1))
              for t in rng]
        yield
        kd = [k[t] * jnp.exp(gc_last[t] - gc_col[t]) for t in rng]
        r = [_dot(jnp.concatenate([kd[t].T, intra[t]], axis=0), wu[t]) for t in rng]
        mn = [r[t][:HEAD_DIM] for t in rng]
        io = [r[t][HEAD_DIM:] for t in rng]
        for t, (c, h) in enumerate(inst):
            rows = slice(c * C, (c + 1) * C)
            mneg_ref[slot, c, h] = (-mn[t][:, :HEAD_DIM]).astype(BF16)
            n_ref[slot, c, h] = mn[t][:, HEAD_DIM:]
            o1_ref[slot, rows, cols[h]] = (q[t] * eg[t] - io[t][:, :HEAD_DIM]).astype(BF16)
            o2_ref[slot, rows, cols[h]] = io[t][:, HEAD_DIM:]
            a_ref[slot, c, h] = jnp.broadcast_to(jnp.exp(gc_last[t]), (SUBLANES, HEAD_DIM))
        yield

    def phase_b_step(slot, c):
        hs = range(H)
        state = [s_ref[h] for h in hs]
        sb = [state[h].astype(BF16) for h in hs]
        ms = [jnp.dot(mneg_ref[slot, c, h], sb[h], preferred_element_type=F32) for h in hs]
        for h in hs:
            snap_ref[slot, c, h] = sb[h]
            s_ref[h] = state[h] * a_ref[slot, c, h, 0:1, :] + ms[h] + n_ref[slot, c, h]

    def phase_c(row0, slot):
        inst = [(c, h) for c in range(G) for h in range(H)]
        os_ = [jnp.dot(o1_ref[slot, c * C:(c + 1) * C, cols[h]], snap_ref[slot, c, h],
                       preferred_element_type=F32) for c, h in inst]
        for t, (c, h) in enumerate(inst):
            r0 = _aligned(row0 + c * C, C)
            o = os_[t] + o2_ref[slot, c * C:(c + 1) * C, cols[h]]
            z = z_ref[pl.ds(r0, C), cols[h]].astype(F32)
            o_ref[pl.ds(r0, C), cols[h]] = (_rms(o, nw_ref[...])
                                            * (z * _sigmoid(z))).astype(o_ref.dtype)

    def trip(g, do_a, do_b, do_c):
        row0 = lambda d: _aligned((g - d) * rows_per_trip, rows_per_trip)
        slot = lambda d: (g - d) % GDN_RING
        if do_c:
            phase_c(row0(2), slot(2))
        b_steps = [functools.partial(phase_b_step, slot(1), c) for c in range(G)] if do_b else []
        if do_a:
            for n, _ in enumerate(phase_a(row0(0), slot(0))):
                if n % 2 == 0 and b_steps:
                    b_steps.pop(0)()
        for step in b_steps:
            step()

    def steady(g, carry):
        trip(g, True, True, True)
        return carry

    all_on = lambda g: g < ntrips and 1 <= g <= ntrips and 2 <= g <= ntrips + 1
    g = 0
    while g < ntrips + 2:
        if all_on(g):
            last = g
            while all_on(last + 1):
                last += 1
            lax.fori_loop(g, last + 1, steady, 0)
            g = last + 1
        else:
            trip(g, g < ntrips, 1 <= g <= ntrips, 2 <= g <= ntrips + 1)
            g += 1


def _gdn(gq, gk, gv, gz, bg, nw):
    B, T, _ = gq.shape
    rows_per_trip = GDN_GROUP * GDN_CHUNK
    assert T % rows_per_trip == 0
    tok = lambda width: pl.BlockSpec((None, T, width), lambda b: (b, 0, 0))
    per_chunk = (GDN_RING, GDN_GROUP, GDN_HEADS)
    return pl.pallas_call(
        functools.partial(_gdn_body, T=T),
        grid=(B,),
        in_specs=[tok(GDN_WIDTH)] * 4 + [tok(LANES), pl.BlockSpec((1, HEAD_DIM), lambda b: (0, 0))],
        out_specs=tok(GDN_WIDTH),
        out_shape=jax.ShapeDtypeStruct((B, T, GDN_WIDTH), BF16),
        scratch_shapes=[
            pltpu.VMEM((GDN_HEADS, HEAD_DIM, HEAD_DIM), F32),
            pltpu.VMEM(per_chunk + (HEAD_DIM, HEAD_DIM), BF16),
            pltpu.VMEM(per_chunk + (HEAD_DIM, HEAD_DIM), F32),
            pltpu.VMEM((GDN_RING, rows_per_trip, GDN_WIDTH), BF16),
            pltpu.VMEM((GDN_RING, rows_per_trip, GDN_WIDTH), F32),
            pltpu.VMEM(per_chunk + (SUBLANES, HEAD_DIM), F32),
            pltpu.VMEM(per_chunk + (HEAD_DIM, HEAD_DIM), BF16),
        ],
        compiler_params=pltpu.CompilerParams(
            dimension_semantics=("arbitrary",), vmem_limit_bytes=VMEM_LIMIT),
        name="gdn",
    )(gq, gk, gv, gz, bg, nw)


def _moba_body(q_ref, k_ref, v_ref, o_ref, vt_ref, *, T):
    BS = MOBA_BLOCK
    nb = T // BS
    nrow = -(-nb // SUBLANES) * SUBLANES
    neg = -jnp.inf
    heads = [slice(h * HEAD_DIM, (h + 1) * HEAD_DIM) for h in range(MOBA_HEADS_PER_STEP)]
    blk = lax.broadcasted_iota(jnp.int32, (nrow, BS), 0)
    key = lax.broadcasted_iota(jnp.int32, (BS, BS), 0)
    qry = lax.broadcasted_iota(jnp.int32, (BS, BS), 1)

    kmean = []
    for n, hd in enumerate(heads):
        means = [jnp.mean(k_ref[j * BS:(j + 1) * BS, hd].astype(F32), axis=0, keepdims=True)
                 for j in range(nb)]
        if nrow > nb:
            means.append(jnp.zeros((nrow - nb, HEAD_DIM), F32))
        kmean.append(jnp.concatenate(means, axis=0))
        for j in range(nb):
            vt_ref[n, 0:HEAD_DIM, j * BS:(j + 1) * BS] = v_ref[j * BS:(j + 1) * BS, hd].T
        vt_ref[n, HEAD_DIM:, :] = jnp.ones((BF16_ROWS, T), BF16)

    def block_scores(n, i):
        q = q_ref[i * BS:(i + 1) * BS, heads[n]]
        sel = None
        if i > MOBA_TOPK:
            gate = lax.dot_general(kmean[n], q.astype(F32), (((1,), (1,)), ((), ())),
                                   precision=lax.Precision.HIGHEST, preferred_element_type=F32)
            gate = jnp.where(blk < i, gate, neg)
            rank = jnp.zeros((nrow, BS), F32)
            for jp in range(i):
                gj = gate[jp:jp + 1, :]
                beats = (gj > gate) | ((gj == gate) & (blk > jp))
                rank = rank + jnp.where(beats, 1.0, 0.0)
            sel = jnp.where((rank < MOBA_TOPK) & (blk < i), 1.0, 0.0)
        scores = []
        for j in range(i + 1):
            s = _dot_nt(k_ref[j * BS:(j + 1) * BS, heads[n]], q)
            if j == i:
                s = jnp.where(key <= qry, s, neg)
            elif sel is not None:
                s = jnp.where(sel[j:j + 1, :] > 0.5, s, neg)
            scores.append(s)
        return scores

    def block_output(n, i, scores):
        m = scores[0].max(axis=0, keepdims=True)
        for s in scores[1:]:
            m = jnp.maximum(m, s.max(axis=0, keepdims=True))
        acc = jnp.zeros((HEAD_DIM + BF16_ROWS, BS), F32)
        for j, s in enumerate(scores):
            acc = acc + _dot(vt_ref[n, :, j * BS:(j + 1) * BS], jnp.exp2(s - m))
        out = acc[:HEAD_DIM, :] / acc[HEAD_DIM:HEAD_DIM + 1, :]
        o_ref[i * BS:(i + 1) * BS, heads[n]] = out.T.astype(o_ref.dtype)

    hs = range(MOBA_HEADS_PER_STEP)
    scores = [block_scores(n, 0) for n in hs]
    for i in range(nb):
        following = [block_scores(n, i + 1) for n in hs] if i + 1 < nb else None
        for n in hs:
            block_output(n, i, scores[n])
        scores = following


def _moba(mq, mk, mv):
    B, T, _ = mq.shape
    assert T % MOBA_BLOCK == 0 and MOBA_HEADS % MOBA_HEADS_PER_STEP == 0
    width = MOBA_HEADS_PER_STEP * HEAD_DIM
    head = pl.BlockSpec((None, T, width), lambda b, h: (b, 0, h))
    return pl.pallas_call(
        functools.partial(_moba_body, T=T),
        grid=(B, MOBA_HEADS // MOBA_HEADS_PER_STEP),
        in_specs=[head] * 3,
        out_specs=head,
        out_shape=jax.ShapeDtypeStruct((B, T, MOBA_WIDTH), BF16),
        scratch_shapes=[pltpu.VMEM((MOBA_HEADS_PER_STEP, HEAD_DIM + BF16_ROWS, T), BF16)],
        compiler_params=pltpu.CompilerParams(
            dimension_semantics=("arbitrary", "arbitrary"), vmem_limit_bytes=VMEM_LIMIT),
        name="moba",
    )(mq, mk, mv)


def _mlp_body(og_ref, om_ref, x_ref, p_ref, wo_ref, n1_ref, n2_ref, wu_ref, wd_ref, n3_ref,
              wg_ref, wp_ref, o_ref, *, ff_chunk):
    tm = x_ref.shape[0]
    halves = [slice(r0, r0 + tm // 2) for r0 in (0, tm // 2)]
    mix = [_dot(og_ref[r, :], wo_ref[0:GDN_WIDTH, :])
           + _dot(om_ref[r, :], wo_ref[GDN_WIDTH:GDN_WIDTH + MOBA_WIDTH, :]) for r in halves]
    h = [x_ref[r, :] + _rms(mix[n], n1_ref[...]) for n, r in enumerate(halves)]
    a = [_rms(h[n], n2_ref[...]).astype(BF16) for n in range(2)]
    f = [None, None]
    for c0 in range(0, D_FF, ff_chunk):
        up = [jnp.dot(a[n], wu_ref[:, c0:c0 + ff_chunk], preferred_element_type=F32)
              for n in range(2)]
        for n in range(2):
            part = _dot(jnp.square(jnp.maximum(up[n], 0.0)), wd_ref[c0:c0 + ff_chunk, :])
            f[n] = part if f[n] is None else f[n] + part
    h = [h[n] + _rms(f[n], n3_ref[...]) for n in range(2)]
    gate = [_dot(h[n], wg_ref[...]) for n in range(2)]
    ple = [_dot(p_ref[r, :], wp_ref[...]) for r in halves]
    for n, r in enumerate(halves):
        o_ref[r, :] = h[n] + _sigmoid(gate[n]) * ple[n]


def _mlp(og, om, x, p, wo, n1, n2, wu, wd, n3, wg, wp, *, tm):
    B, T, D = x.shape
    tok = lambda width: pl.BlockSpec((None, tm, width), lambda b, i: (b, i, 0))
    const = lambda shape: pl.BlockSpec(shape, lambda b, i: (0,) * len(shape),
                                       pipeline_mode=pl.Buffered(1))
    return pl.pallas_call(
        functools.partial(_mlp_body, ff_chunk=1024),
        grid=(B, T // tm),
        in_specs=[tok(GDN_WIDTH), tok(MOBA_WIDTH), tok(D), tok(PLE_DIM),
                  const((D, D)), const((1, D)), const((1, D)), const((D, D_FF)),
                  const((D_FF, D)), const((1, D)), const((D, D)), const((PLE_DIM, D))],
        out_specs=tok(D),
        out_shape=jax.ShapeDtypeStruct((B, T, D), F32),
        compiler_params=pltpu.CompilerParams(
            dimension_semantics=("arbitrary", "arbitrary"), vmem_limit_bytes=VMEM_LIMIT),
        name="mlp",
    )(og, om, x, p, wo, n1, n2, wu, wd, n3, wg, wp)


def _rope_tables(T):
    half = ROPE_DIMS // 2
    inv_freq = ROPE_THETA ** (-np.arange(half, dtype=np.float64) * (2.0 / ROPE_DIMS))
    ang = np.arange(T, dtype=np.float64)[:, None] * inv_freq[None, :]
    cos, sin = np.cos(ang), np.sin(ang)
    zeros = np.zeros((T, HEAD_DIM - ROPE_DIMS))
    z_half = np.zeros((T, half))
    cos_t = np.concatenate([cos, cos, np.ones((T, HEAD_DIM - ROPE_DIMS))], axis=-1)
    sa_t = np.concatenate([-sin, z_half, zeros], axis=-1)
    sb_t = np.concatenate([z_half, sin, zeros], axis=-1)
    cos_t, sa_t, sb_t = (jnp.asarray(t, dtype=F32) for t in (cos_t, sa_t, sb_t))
    return cos_t, sa_t, sb_t


def _layer(h, p_i, w_in, layer, conv_w, a_log, dt_bias, gdn_norm_w, w_out, attn_pre_norm,
           attn_post_norm, mlp_pre_norm, mlp_post_norm, w_up, w_down, w_ple, w_ple_gate, tables):
    B, T, D = h.shape
    tm = min(512, T)
    assert T % tm == 0 and T % MOBA_BLOCK == 0 and T % GDN_CHUNK == 0
    nh = 2 * GDN_HEADS
    gparams = jnp.zeros((2, LANES), F32)
    gparams = gparams.at[0, GDN_HEADS:nh].set(a_log.astype(F32))
    gparams = gparams.at[1, GDN_HEADS:nh].set(dt_bias.astype(F32))
    row = lambda v: v.reshape(1, -1).astype(F32)

    gq, gk, gv, gz, bg, mq, mk, mv = _in_proj(
        h, row(attn_pre_norm), jnp.swapaxes(w_in.astype(F32), 1, 2), layer, conv_w.astype(F32),
        gparams, *tables,
        tm=tm)
    o_gdn = _gdn(gq, gk, gv, gz, bg, row(gdn_norm_w))
    o_moba = _moba(mq, mk, mv)
    return _mlp(o_gdn, o_moba, h, p_i, w_out.astype(BF16), row(attn_post_norm),
                row(mlp_pre_norm), w_up.astype(BF16), w_down.astype(BF16),
                row(mlp_post_norm), w_ple_gate.astype(BF16), w_ple.astype(BF16), tm=tm)


def kernel(x, p, w_in, conv_w, a_log, dt_bias, gdn_norm_w, w_out, attn_pre_norm, attn_post_norm,
           mlp_pre_norm, mlp_post_norm, w_up, w_down, w_ple, w_ple_gate):
    tables = _rope_tables(x.shape[1])
    h = x
    for i in range(w_in.shape[0]):
        h = _layer(h, p[i], w_in, i, conv_w[i], a_log[i], dt_bias[i], gdn_norm_w[i], w_out[i],
                   attn_pre_norm[i], attn_post_norm[i], mlp_pre_norm[i], mlp_post_norm[i],
                   w_up[i], w_down[i], w_ple[i], w_ple_gate[i], tables)
    return h
```

```python
import functools
import math

import jax
import jax.numpy as jnp
import numpy as np
from jax import lax
from jax.experimental import pallas as pl
from jax.experimental.pallas import tpu as pltpu

F32 = jnp.float32
BF16 = jnp.bfloat16

D_MODEL = 1024
HEAD_DIM = 128
GDN_HEADS = 4
GDN_WIDTH = GDN_HEADS * HEAD_DIM
GDN_CONV = 4
GDN_CHUNK = 64
GDN_GROUP = 4
GDN_RING = 3
MOBA_HEADS = 4
MOBA_WIDTH = MOBA_HEADS * HEAD_DIM
MOBA_BLOCK = 256
MOBA_TOPK = 3
MOBA_HEADS_PER_STEP = 2
ROPE_DIMS = 32
ROPE_THETA = 500000.0
D_FF = 4 * D_MODEL
PLE_DIM = 256
RMS_EPS = 1e-6
LANES = 128
SUBLANES = 8
BF16_ROWS = 16
CONV_HALO = 8
REPACK_ROWS = 128
QK_SCALE = HEAD_DIM ** -0.5
MOBA_Q_SCALE = QK_SCALE * math.log2(math.e)
VMEM_LIMIT = 56 * 1024 * 1024

C_GQKV = 0
C_GZ = 3 * GDN_WIDTH
C_MQ = 4 * GDN_WIDTH
C_MK = C_MQ + MOBA_WIDTH
C_MV = C_MK + MOBA_WIDTH
C_BA = C_MV + MOBA_WIDTH
C_END = C_BA + LANES


def _rms(x, w):
    return x * lax.rsqrt(jnp.mean(x * x, axis=-1, keepdims=True) + RMS_EPS) * w


def _sigmoid(x):
    return 1.0 / (1.0 + jnp.exp(-x))


def _aligned(x, m):
    return x if isinstance(x, int) else pl.multiple_of(x, m)


def _dot(a, b):
    return jnp.dot(a.astype(BF16), b.astype(BF16), preferred_element_type=F32)


def _dot_nt(a, b):
    return lax.dot_general(a.astype(BF16), b.astype(BF16), (((1,), (1,)), ((), ())),
                           preferred_element_type=F32)


def _in_proj_body(x_ref, nw_ref, win_ref, cw_ref, gp_ref, cos_ref, sa_ref, sb_ref,
                  gq_ref, gk_ref, gv_ref, gz_ref, bg_ref, mq_ref, mk_ref, mv_ref,
                  halo, w_ref, *, tm):
    i = pl.program_id(1)

    @pl.when((pl.program_id(0) == 0) & (i == 0))
    def _():
        n_gate = 2 * GDN_HEADS
        for c0 in range(0, C_BA, REPACK_ROWS):
            src = c0 if c0 < C_MQ else c0 + n_gate
            w_ref[:, c0:c0 + REPACK_ROWS] = win_ref[src:src + REPACK_ROWS, :].T.astype(BF16)
        gate_rows = jnp.concatenate([win_ref[C_MQ:C_MQ + n_gate, :],
                                     jnp.zeros((LANES - n_gate, D_MODEL), F32)], axis=0)
        w_ref[:, C_BA:C_END] = gate_rows.T.astype(BF16)

    u = _rms(x_ref[...], nw_ref[...]).astype(BF16)

    def proj(c0, width):
        return jnp.dot(u, w_ref[:, c0:c0 + width], preferred_element_type=F32)

    @pl.when(i == 0)
    def _():
        halo[...] = jnp.zeros_like(halo)

    pair_w = 2 * HEAD_DIM

    def conv_pair(p):
        pair = proj(C_GQKV + p * pair_w, pair_w)
        for s in (2 * p, 2 * p + 1):
            c0 = s * HEAD_DIM
            cur = pair[:, (s % 2) * HEAD_DIM:(s % 2 + 1) * HEAD_DIM]
            xs = jnp.concatenate([halo[:, c0:c0 + HEAD_DIM], cur], axis=0)
            halo[:, c0:c0 + HEAD_DIM] = cur[tm - CONV_HALO:, :]
            cw = [cw_ref[j:j + 1, c0:c0 + HEAD_DIM] for j in range(GDN_CONV)]
            x1 = pltpu.roll(xs, 1, 0)
            acc = (cw[3] * xs + cw[2] * x1) + pltpu.roll(cw[1] * xs + cw[0] * x1, 2, 0)
            acc = acc[CONV_HALO:, :]
            y = acc * _sigmoid(acc)
            which, h = divmod(s, GDN_HEADS)
            if which < 2:
                inv_norm = lax.rsqrt(jnp.sum(y * y, axis=-1, keepdims=True) + RMS_EPS)
                y = y * (inv_norm * QK_SCALE if which == 0 else inv_norm)
            (gq_ref, gk_ref, gv_ref)[which][:, h * HEAD_DIM:(h + 1) * HEAD_DIM] = y.astype(BF16)

    def plain_pair(ref, c_base, p):
        ref[:, p * pair_w:(p + 1) * pair_w] = proj(c_base + p * pair_w, pair_w).astype(BF16)

    def rotary_pair(ref, c_base, scale, p):
        cos, sa, sb = cos_ref[...], sa_ref[...], sb_ref[...]
        half = ROPE_DIMS // 2
        pair = proj(c_base + p * pair_w, pair_w)
        for h in (2 * p, 2 * p + 1):
            xr = pair[:, (h % 2) * HEAD_DIM:(h % 2 + 1) * HEAD_DIM]
            rot = (xr * cos + pltpu.roll(xr, LANES - half, 1) * sa
                   + pltpu.roll(xr, half, 1) * sb)
            if scale is not None:
                rot = rot * scale
            ref[:, h * HEAD_DIM:(h + 1) * HEAD_DIM] = rot.astype(BF16)

    def gates():
        ba = proj(C_BA, LANES)
        lane = lax.broadcasted_iota(jnp.int32, (tm, LANES), 1)
        xg = ba + gp_ref[1:2, :]
        e = jnp.exp(-jnp.abs(xg))
        r = 1.0 / (1.0 + e)
        sig = jnp.where(xg >= 0.0, r, e * r)
        g = -jnp.exp(gp_ref[0:1, :]) * (jnp.maximum(xg, 0.0) + jnp.log1p(e))
        bg_ref[...] = jnp.where(lane < GDN_HEADS, sig,
                                jnp.where(lane < 2 * GDN_HEADS, g, 0.0))

    convs = [functools.partial(conv_pair, p) for p in range(3 * GDN_WIDTH // pair_w)]
    light = ([functools.partial(plain_pair, gz_ref, C_GZ, p) for p in range(GDN_WIDTH // pair_w)]
             + [functools.partial(plain_pair, mv_ref, C_MV, p) for p in range(MOBA_WIDTH // pair_w)]
             + [functools.partial(rotary_pair, mq_ref, C_MQ, MOBA_Q_SCALE, p)
                for p in range(MOBA_WIDTH // pair_w)]
             + [functools.partial(rotary_pair, mk_ref, C_MK, None, p)
                for p in range(MOBA_WIDTH // pair_w)]
             + [gates])
    done = 0
    for n, conv in enumerate(convs, start=1):
        conv()
        upto = len(light) * n // len(convs)
        for stage in light[done:upto]:
            stage()
        done = upto


def _in_proj(x, nw, w_in, layer, conv_w, gparams, cos_t, sa_t, sb_t, *, tm):
    B, T, D = x.shape
    tok = lambda width: pl.BlockSpec((None, tm, width), lambda b, i: (b, i, 0))
    const = lambda shape: pl.BlockSpec(shape, lambda b, i: (0,) * len(shape))
    table = pl.BlockSpec((tm, LANES), lambda b, i: (i, 0))
    o512 = jax.ShapeDtypeStruct((B, T, GDN_WIDTH), BF16)
    return pl.pallas_call(
        functools.partial(_in_proj_body, tm=tm),
        grid=(B, T // tm),
        in_specs=[tok(D), const((1, D)),
                  pl.BlockSpec((None,) + w_in.shape[1:], lambda b, i: (layer, 0, 0),
                               pipeline_mode=pl.Buffered(1)),
                  const((GDN_CONV, 3 * GDN_WIDTH)),
                  const((2, LANES)), table, table, table],
        out_specs=[tok(GDN_WIDTH)] * 4 + [tok(LANES)] + [tok(MOBA_WIDTH)] * 3,
        out_shape=[o512] * 4 + [jax.ShapeDtypeStruct((B, T, LANES), F32)] + [o512] * 3,
        scratch_shapes=[pltpu.VMEM((CONV_HALO, 3 * GDN_WIDTH), F32),
                        pltpu.VMEM((D, C_END), BF16)],
        compiler_params=pltpu.CompilerParams(
            dimension_semantics=("arbitrary", "arbitrary"), vmem_limit_bytes=VMEM_LIMIT),
        name="in_proj",
    )(x, nw, w_in, conv_w, gparams, cos_t, sa_t, sb_t)


def _gdn_body(q_ref, k_ref, v_ref, z_ref, bg_ref, nw_ref, o_ref,
              s_ref, mneg_ref, n_ref, o1_ref, o2_ref, a_ref, snap_ref, *, T):
    C = GDN_CHUNK
    H = GDN_HEADS
    G = GDN_GROUP
    rows_per_trip = G * C
    ntrips = T // rows_per_trip
    ri = lax.broadcasted_iota(jnp.int32, (C, C), 0)
    ci = lax.broadcasted_iota(jnp.int32, (C, C), 1)
    causal = ci <= ri
    strict = ci < ri
    eye = (ri == ci).astype(F32)
    cols = [slice(h * HEAD_DIM, (h + 1) * HEAD_DIM) for h in range(H)]
    s_ref[...] = jnp.zeros_like(s_ref)

    def phase_a(row0, slot):
        inst = [(c, h) for c in range(G) for h in range(H)]
        rng = range(len(inst))
        r0s = [_aligned(row0 + c * C, C) for c in range(G)]
        bgs = [bg_ref[pl.ds(r0, C), :] for r0 in r0s]
        q = [q_ref[pl.ds(r0s[c], C), cols[h]].astype(F32) for c, h in inst]
        k = [k_ref[pl.ds(r0s[c], C), cols[h]].astype(F32) for c, h in inst]
        v = [v_ref[pl.ds(r0s[c], C), cols[h]].astype(F32) for c, h in inst]
        beta = [bgs[c][:, h:h + 1] for c, h in inst]
        gc_col, gc_last, decay = [], [], []
        for c, h in inst:
            gb = jnp.broadcast_to(bgs[c][:, H + h:H + h + 1], (C, C))
            row = jnp.sum(jnp.where(ri <= ci, gb, 0.0), axis=0, keepdims=True)
            g_row = jnp.sum(jnp.where(ri == ci, gb, 0.0), axis=0, keepdims=True)
            col = jnp.sum(jnp.where(causal, jnp.broadcast_to(g_row, (C, C)), 0.0),
                          axis=1, keepdims=True)
            gc_col.append(col)
            gc_last.append(row[:, C - 1:C])
            decay.append(jnp.exp(jnp.where(causal, col - row, -jnp.inf)))
        kb = [k[t] * beta[t] for t in rng]
        eg = [jnp.exp(gc_col[t]) for t in rng]
        kq = [_dot_nt(jnp.concatenate([kb[t], q[t]], axis=0), k[t]) for t in rng]
        yield
        low = [jnp.where(strict, kq[t][:C] * decay[t], 0.0) for t in rng]
        intra = [jnp.where(causal, kq[t][C:] * decay[t], 0.0) for t in rng]
        inv = [eye - low[t] for t in rng]
        pw = [_dot(low[t], low[t]) for t in rng]
        yield
        for _ in range(4):
            r = [_dot(jnp.concatenate([pw[t], inv[t]], axis=0), pw[t]) for t in rng]
            inv = [inv[t] + r[t][C:] for t in rng]
            pw = [r[t][:C] for t in rng]
            yield
        inv = [inv[t] + _dot(inv[t], pw[t]) for t in rng]
        yield
        wu = [_dot(inv[t], jnp.concatenate([kb[t] * eg[t], v[t] * beta[t]], axis=-1))
              for t in rng]
        yield
        kd = [k[t] * jnp.exp(gc_last[t] - gc_col[t]) for t in rng]
        r = [_dot(jnp.concatenate([kd[t].T, intra[t]], axis=0), wu[t]) for t in rng]
        mn = [r[t][:HEAD_DIM] for t in rng]
        io = [r[t][HEAD_DIM:] for t in rng]
        for t, (c, h) in enumerate(inst):
            rows = slice(c * C, (c + 1) * C)
            mneg_ref[slot, c, h] = (-mn[t][:, :HEAD_DIM]).astype(BF16)
            n_ref[slot, c, h] = mn[t][:, HEAD_DIM:]
            o1_ref[slot, rows, cols[h]] = (q[t] * eg[t] - io[t][:, :HEAD_DIM]).astype(BF16)
            o2_ref[slot, rows, cols[h]] = io[t][:, HEAD_DIM:]
            a_ref[slot, c, h] = jnp.broadcast_to(jnp.exp(gc_last[t]), (SUBLANES, HEAD_DIM))
        yield

    def phase_b_step(slot, c):
        hs = range(H)
        state = [s_ref[h] for h in hs]
        sb = [state[h].astype(BF16) for h in hs]
        ms = [jnp.dot(mneg_ref[slot, c, h], sb[h], preferred_element_type=F32) for h in hs]
        for h in hs:
            snap_ref[slot, c, h] = sb[h]
            s_ref[h] = state[h] * a_ref[slot, c, h, 0:1, :] + ms[h] + n_ref[slot, c, h]

    def phase_c(row0, slot):
        inst = [(c, h) for c in range(G) for h in range(H)]
        os_ = [jnp.dot(o1_ref[slot, c * C:(c + 1) * C, cols[h]], snap_ref[slot, c, h],
                       preferred_element_type=F32) for c, h in inst]
        for t, (c, h) in enumerate(inst):
            r0 = _aligned(row0 + c * C, C)
            o = os_[t] + o2_ref[slot, c * C:(c + 1) * C, cols[h]]
            z = z_ref[pl.ds(r0, C), cols[h]].astype(F32)
            o_ref[pl.ds(r0, C), cols[h]] = (_rms(o, nw_ref[...])
                                            * (z * _sigmoid(z))).astype(o_ref.dtype)

    def trip(g, do_a, do_b, do_c):
        row0 = lambda d: _aligned((g - d) * rows_per_trip, rows_per_trip)
        slot = lambda d: (g - d) % GDN_RING
        if do_c:
            phase_c(row0(2), slot(2))
        b_steps = [functools.partial(phase_b_step, slot(1), c) for c in range(G)] if do_b else []
        if do_a:
            for n, _ in enumerate(phase_a(row0(0), slot(0))):
                if n % 2 == 0 and b_steps:
                    b_steps.pop(0)()
        for step in b_steps:
            step()

    def steady(g, carry):
        trip(g, True, True, True)
        return carry

    all_on = lambda g: g < ntrips and 1 <= g <= ntrips and 2 <= g <= ntrips + 1
    g = 0
    while g < ntrips + 2:
        if all_on(g):
            last = g
            while all_on(last + 1):
                last += 1
            lax.fori_loop(g, last + 1, steady, 0)
            g = last + 1
        else:
            trip(g, g < ntrips, 1 <= g <= ntrips, 2 <= g <= ntrips + 1)
            g += 1


def _gdn(gq, gk, gv, gz, bg, nw):
    B, T, _ = gq.shape
    rows_per_trip = GDN_GROUP * GDN_CHUNK
    assert T % rows_per_trip == 0
    tok = lambda width: pl.BlockSpec((None, T, width), lambda b: (b, 0, 0))
    per_chunk = (GDN_RING, GDN_GROUP, GDN_HEADS)
    return pl.pallas_call(
        functools.partial(_gdn_body, T=T),
        grid=(B,),
        in_specs=[tok(GDN_WIDTH)] * 4 + [tok(LANES), pl.BlockSpec((1, HEAD_DIM), lambda b: (0, 0))],
        out_specs=tok(GDN_WIDTH),
        out_shape=jax.ShapeDtypeStruct((B, T, GDN_WIDTH), BF16),
        scratch_shapes=[
            pltpu.VMEM((GDN_HEADS, HEAD_DIM, HEAD_DIM), F32),
            pltpu.VMEM(per_chunk + (HEAD_DIM, HEAD_DIM), BF16),
            pltpu.VMEM(per_chunk + (HEAD_DIM, HEAD_DIM), F32),
            pltpu.VMEM((GDN_RING, rows_per_trip, GDN_WIDTH), BF16),
            pltpu.VMEM((GDN_RING, rows_per_trip, GDN_WIDTH), F32),
            pltpu.VMEM(per_chunk + (SUBLANES, HEAD_DIM), F32),
            pltpu.VMEM(per_chunk + (HEAD_DIM, HEAD_DIM), BF16),
        ],
        compiler_params=pltpu.CompilerParams(
            dimension_semantics=("arbitrary",), vmem_limit_bytes=VMEM_LIMIT),
        name="gdn",
    )(gq, gk, gv, gz, bg, nw)


def _moba_body(q_ref, k_ref, v_ref, o_ref, vt_ref, *, T):
    BS = MOBA_BLOCK
    nb = T // BS
    nrow = -(-nb // SUBLANES) * SUBLANES
    neg = -jnp.inf
    heads = [slice(h * HEAD_DIM, (h + 1) * HEAD_DIM) for h in range(MOBA_HEADS_PER_STEP)]
    blk = lax.broadcasted_iota(jnp.int32, (nrow, BS), 0)
    key = lax.broadcasted_iota(jnp.int32, (BS, BS), 0)
    qry = lax.broadcasted_iota(jnp.int32, (BS, BS), 1)

    kmean = []
    for n, hd in enumerate(heads):
        means = [jnp.mean(k_ref[j * BS:(j + 1) * BS, hd].astype(F32), axis=0, keepdims=True)
                 for j in range(nb)]
        if nrow > nb:
            means.append(jnp.zeros((nrow - nb, HEAD_DIM), F32))
        kmean.append(jnp.concatenate(means, axis=0))
        for j in range(nb):
            vt_ref[n, 0:HEAD_DIM, j * BS:(j + 1) * BS] = v_ref[j * BS:(j + 1) * BS, hd].T
        vt_ref[n, HEAD_DIM:, :] = jnp.ones((BF16_ROWS, T), BF16)

    def block_scores(n, i):
        q = q_ref[i * BS:(i + 1) * BS, heads[n]]
        sel = None
        if i > MOBA_TOPK:
            gate = lax.dot_general(kmean[n], q.astype(F32), (((1,), (1,)), ((), ())),
                                   precision=lax.Precision.HIGHEST, preferred_element_type=F32)
            gate = jnp.where(blk < i, gate, neg)
            rank = jnp.zeros((nrow, BS), F32)
            for jp in range(i):
                gj = gate[jp:jp + 1, :]
                beats = (gj > gate) | ((gj == gate) & (blk > jp))
                rank = rank + jnp.where(beats, 1.0, 0.0)
            sel = jnp.where((rank < MOBA_TOPK) & (blk < i), 1.0, 0.0)
        scores = []
        for j in range(i + 1):
            s = _dot_nt(k_ref[j * BS:(j + 1) * BS, heads[n]], q)
            if j == i:
                s = jnp.where(key <= qry, s, neg)
            elif sel is not None:
                s = jnp.where(sel[j:j + 1, :] > 0.5, s, neg)
            scores.append(s)
        return scores

    def block_output(n, i, scores):
        m = scores[0].max(axis=0, keepdims=True)
        for s in scores[1:]:
            m = jnp.maximum(m, s.max(axis=0, keepdims=True))
        acc = jnp.zeros((HEAD_DIM + BF16_ROWS, BS), F32)
        for j, s in enumerate(scores):
            acc = acc + _dot(vt_ref[n, :, j * BS:(j + 1) * BS], jnp.exp2(s - m))
        out = acc[:HEAD_DIM, :] / acc[HEAD_DIM:HEAD_DIM + 1, :]
        o_ref[i * BS:(i + 1) * BS, heads[n]] = out.T.astype(o_ref.dtype)

    hs = range(MOBA_HEADS_PER_STEP)
    scores = [block_scores(n, 0) for n in hs]
    for i in range(nb):
        following = [block_scores(n, i + 1) for n in hs] if i + 1 < nb else None
        for n in hs:
            block_output(n, i, scores[n])
        scores = following


def _moba(mq, mk, mv):
    B, T, _ = mq.shape
    assert T % MOBA_BLOCK == 0 and MOBA_HEADS % MOBA_HEADS_PER_STEP == 0
    width = MOBA_HEADS_PER_STEP * HEAD_DIM
    head = pl.BlockSpec((None, T, width), lambda b, h: (b, 0, h))
    return pl.pallas_call(
        functools.partial(_moba_body, T=T),
        grid=(B, MOBA_HEADS // MOBA_HEADS_PER_STEP),
        in_specs=[head] * 3,
        out_specs=head,
        out_shape=jax.ShapeDtypeStruct((B, T, MOBA_WIDTH), BF16),
        scratch_shapes=[pltpu.VMEM((MOBA_HEADS_PER_STEP, HEAD_DIM + BF16_ROWS, T), BF16)],
        compiler_params=pltpu.CompilerParams(
            dimension_semantics=("arbitrary", "arbitrary"), vmem_limit_bytes=VMEM_LIMIT),
        name="moba",
    )(mq, mk, mv)


def _mlp_body(og_ref, om_ref, x_ref, p_ref, wo_ref, n1_ref, n2_ref, wu_ref, wd_ref, n3_ref,
              wg_ref, wp_ref, o_ref, *, ff_chunk):
    tm = x_ref.shape[0]
    halves = [slice(r0, r0 + tm // 2) for r0 in (0, tm // 2)]
    mix = [_dot(og_ref[r, :], wo_ref[0:GDN_WIDTH, :])
           + _dot(om_ref[r, :], wo_ref[GDN_WIDTH:GDN_WIDTH + MOBA_WIDTH, :]) for r in halves]
    h = [x_ref[r, :] + _rms(mix[n], n1_ref[...]) for n, r in enumerate(halves)]
    a = [_rms(h[n], n2_ref[...]).astype(BF16) for n in range(2)]
    f = [None, None]
    for c0 in range(0, D_FF, ff_chunk):
        up = [jnp.dot(a[n], wu_ref[:, c0:c0 + ff_chunk], preferred_element_type=F32)
              for n in range(2)]
        for n in range(2):
            part = _dot(jnp.square(jnp.maximum(up[n], 0.0)), wd_ref[c0:c0 + ff_chunk, :])
            f[n] = part if f[n] is None else f[n] + part
    h = [h[n] + _rms(f[n], n3_ref[...]) for n in range(2)]
    gate = [_dot(h[n], wg_ref[...]) for n in range(2)]
    ple = [_dot(p_ref[r, :], wp_ref[...]) for r in halves]
    for n, r in enumerate(halves):
        o_ref[r, :] = h[n] + _sigmoid(gate[n]) * ple[n]


def _mlp(og, om, x, p, wo, n1, n2, wu, wd, n3, wg, wp, *, tm):
    B, T, D = x.shape
    tok = lambda width: pl.BlockSpec((None, tm, width), lambda b, i: (b, i, 0))
    const = lambda shape: pl.BlockSpec(shape, lambda b, i: (0,) * len(shape),
                                       pipeline_mode=pl.Buffered(1))
    return pl.pallas_call(
        functools.partial(_mlp_body, ff_chunk=1024),
        grid=(B, T // tm),
        in_specs=[tok(GDN_WIDTH), tok(MOBA_WIDTH), tok(D), tok(PLE_DIM),
                  const((D, D)), const((1, D)), const((1, D)), const((D, D_FF)),
                  const((D_FF, D)), const((1, D)), const((D, D)), const((PLE_DIM, D))],
        out_specs=tok(D),
        out_shape=jax.ShapeDtypeStruct((B, T, D), F32),
        compiler_params=pltpu.CompilerParams(
            dimension_semantics=("arbitrary", "arbitrary"), vmem_limit_bytes=VMEM_LIMIT),
        name="mlp",
    )(og, om, x, p, wo, n1, n2, wu, wd, n3, wg, wp)


def _rope_tables(T):
    half = ROPE_DIMS // 2
    inv_freq = ROPE_THETA ** (-np.arange(half, dtype=np.float64) * (2.0 / ROPE_DIMS))
    ang = np.arange(T, dtype=np.float64)[:, None] * inv_freq[None, :]
    cos, sin = np.cos(ang), np.sin(ang)
    zeros = np.zeros((T, HEAD_DIM - ROPE_DIMS))
    z_half = np.zeros((T, half))
    cos_t = np.concatenate([cos, cos, np.ones((T, HEAD_DIM - ROPE_DIMS))], axis=-1)
    sa_t = np.concatenate([-sin, z_half, zeros], axis=-1)
    sb_t = np.concatenate([z_half, sin, zeros], axis=-1)
    cos_t, sa_t, sb_t = (jnp.asarray(t, dtype=F32) for t in (cos_t, sa_t, sb_t))
    return cos_t, sa_t, sb_t


def _layer(h, p_i, w_in, layer, conv_w, a_log, dt_bias, gdn_norm_w, w_out, attn_pre_norm,
           attn_post_norm, mlp_pre_norm, mlp_post_norm, w_up, w_down, w_ple, w_ple_gate, tables):
    B, T, D = h.shape
    tm = min(512, T)
    assert T % tm == 0 and T % MOBA_BLOCK == 0 and T % GDN_CHUNK == 0
    nh = 2 * GDN_HEADS
    gparams = jnp.zeros((2, LANES), F32)
    gparams = gparams.at[0, GDN_HEADS:nh].set(a_log.astype(F32))
    gparams = gparams.at[1, GDN_HEADS:nh].set(dt_bias.astype(F32))
    row = lambda v: v.reshape(1, -1).astype(F32)

    gq, gk, gv, gz, bg, mq, mk, mv = _in_proj(
        h, row(attn_pre_norm), jnp.swapaxes(w_in.astype(F32), 1, 2), layer, conv_w.astype(F32),
        gparams, *tables,
        tm=tm)
    o_gdn = _gdn(gq, gk, gv, gz, bg, row(gdn_norm_w))
    o_moba = _moba(mq, mk, mv)
    return _mlp(o_gdn, o_moba, h, p_i, w_out.astype(BF16), row(attn_post_norm),
                row(mlp_pre_norm), w_up.astype(BF16), w_down.astype(BF16),
                row(mlp_post_norm), w_ple_gate.astype(BF16), w_ple.astype(BF16), tm=tm)


def kernel(x, p, w_in, conv_w, a_log, dt_bias, gdn_norm_w, w_out, attn_pre_norm, attn_post_norm,
           mlp_pre_norm, mlp_post_norm, w_up, w_down, w_ple, w_ple_gate):
    tables = _rope_tables(x.shape[1])
    h = x
    for i in range(w_in.shape[0]):
        h = _layer(h, p[i], w_in, i, conv_w[i], a_log[i], dt_bias[i], gdn_norm_w[i], w_out[i],
                   attn_pre_norm[i], attn_post_norm[i], mlp_pre_norm[i], mlp_post_norm[i],
                   w_up[i], w_down[i], w_ple[i], w_ple_gate[i], tables)
    return h
```

```python
import functools
import math

import jax
import jax.numpy as jnp
import numpy as np
from jax import lax
from jax.experimental import pallas as pl
from jax.experimental.pallas import tpu as pltpu

F32 = jnp.float32
BF16 = jnp.bfloat16

D_MODEL = 1024
HEAD_DIM = 128
GDN_HEADS = 4
GDN_WIDTH = GDN_HEADS * HEAD_DIM
GDN_CONV = 4
GDN_CHUNK = 64
GDN_GROUP = 4
GDN_RING = 3
MOBA_HEADS = 4
MOBA_WIDTH = MOBA_HEADS * HEAD_DIM
MOBA_BLOCK = 256
MOBA_TOPK = 3
MOBA_HEADS_PER_STEP = 2
ROPE_DIMS = 32
ROPE_THETA = 500000.0
D_FF = 4 * D_MODEL
PLE_DIM = 256
RMS_EPS = 1e-6
LANES = 128
SUBLANES = 8
BF16_ROWS = 16
CONV_HALO = 8
REPACK_ROWS = 128
MLP_PART_ROWS = 256
QK_SCALE = HEAD_DIM ** -0.5
MOBA_Q_SCALE = QK_SCALE * math.log2(math.e)
VMEM_LIMIT = 56 * 1024 * 1024

C_GQKV = 0
C_GZ = 3 * GDN_WIDTH
C_MQ = 4 * GDN_WIDTH
C_MK = C_MQ + MOBA_WIDTH
C_MV = C_MK + MOBA_WIDTH
C_BA = C_MV + MOBA_WIDTH
C_END = C_BA + LANES


def _rms(x, w):
    return x * lax.rsqrt(jnp.mean(x * x, axis=-1, keepdims=True) + RMS_EPS) * w


def _sigmoid(x):
    return 1.0 / (1.0 + jnp.exp(-x))


def _aligned(x, m):
    return x if isinstance(x, int) else pl.multiple_of(x, m)


def _dot(a, b):
    return jnp.dot(a.astype(BF16), b.astype(BF16), preferred_element_type=F32)


def _dot_nt(a, b):
    return lax.dot_general(a.astype(BF16), b.astype(BF16), (((1,), (1,)), ((), ())),
                           preferred_element_type=F32)


def _in_proj_body(x_ref, nw_ref, win_ref, cw_ref, gp_ref, cos_ref, sa_ref, sb_ref,
                  gq_ref, gk_ref, gv_ref, gz_ref, bg_ref, mq_ref, mk_ref, mv_ref,
                  halo, w_ref, *, tm):
    i = pl.program_id(1)

    @pl.when((pl.program_id(0) == 0) & (i == 0))
    def _():
        n_gate = 2 * GDN_HEADS
        for c0 in range(0, C_BA, REPACK_ROWS):
            src = c0 if c0 < C_MQ else c0 + n_gate
            w_ref[:, c0:c0 + REPACK_ROWS] = win_ref[src:src + REPACK_ROWS, :].T.astype(BF16)
        gate_rows = jnp.concatenate([win_ref[C_MQ:C_MQ + n_gate, :],
                                     jnp.zeros((LANES - n_gate, D_MODEL), F32)], axis=0)
        w_ref[:, C_BA:C_END] = gate_rows.T.astype(BF16)

    u = _rms(x_ref[...], nw_ref[...]).astype(BF16)

    def proj(c0, width):
        return jnp.dot(u, w_ref[:, c0:c0 + width], preferred_element_type=F32)

    @pl.when(i == 0)
    def _():
        halo[...] = jnp.zeros_like(halo)

    pair_w = 2 * HEAD_DIM

    def conv_pair(p):
        pair = proj(C_GQKV + p * pair_w, pair_w)
        for s in (2 * p, 2 * p + 1):
            c0 = s * HEAD_DIM
            cur = pair[:, (s % 2) * HEAD_DIM:(s % 2 + 1) * HEAD_DIM]
            xs = jnp.concatenate([halo[:, c0:c0 + HEAD_DIM], cur], axis=0)
            halo[:, c0:c0 + HEAD_DIM] = cur[tm - CONV_HALO:, :]
            cw = [cw_ref[j:j + 1, c0:c0 + HEAD_DIM] for j in range(GDN_CONV)]
            x1 = pltpu.roll(xs, 1, 0)
            acc = (cw[3] * xs + cw[2] * x1) + pltpu.roll(cw[1] * xs + cw[0] * x1, 2, 0)
            acc = acc[CONV_HALO:, :]
            y = acc * _sigmoid(acc)
            which, h = divmod(s, GDN_HEADS)
            if which < 2:
                inv_norm = lax.rsqrt(jnp.sum(y * y, axis=-1, keepdims=True) + RMS_EPS)
                y = y * (inv_norm * QK_SCALE if which == 0 else inv_norm)
            (gq_ref, gk_ref, gv_ref)[which][:, h * HEAD_DIM:(h + 1) * HEAD_DIM] = y.astype(BF16)

    def plain_pair(ref, c_base, p):
        ref[:, p * pair_w:(p + 1) * pair_w] = proj(c_base + p * pair_w, pair_w).astype(BF16)

    def rotary_pair(ref, c_base, scale, p):
        cos, sa, sb = cos_ref[...], sa_ref[...], sb_ref[...]
        half = ROPE_DIMS // 2
        pair = proj(c_base + p * pair_w, pair_w)
        for h in (2 * p, 2 * p + 1):
            xr = pair[:, (h % 2) * HEAD_DIM:(h % 2 + 1) * HEAD_DIM]
            rot = (xr * cos + pltpu.roll(xr, LANES - half, 1) * sa
                   + pltpu.roll(xr, half, 1) * sb)
            if scale is not None:
                rot = rot * scale
            ref[:, h * HEAD_DIM:(h + 1) * HEAD_DIM] = rot.astype(BF16)

    def gates():
        ba = proj(C_BA, LANES)
        lane = lax.broadcasted_iota(jnp.int32, (tm, LANES), 1)
        xg = ba + gp_ref[1:2, :]
        e = jnp.exp(-jnp.abs(xg))
        r = 1.0 / (1.0 + e)
        sig = jnp.where(xg >= 0.0, r, e * r)
        g = -jnp.exp(gp_ref[0:1, :]) * (jnp.maximum(xg, 0.0) + jnp.log1p(e))
        bg_ref[...] = jnp.where(lane < GDN_HEADS, sig,
                                jnp.where(lane < 2 * GDN_HEADS, g, 0.0))

    convs = [functools.partial(conv_pair, p) for p in range(3 * GDN_WIDTH // pair_w)]
    light = ([functools.partial(plain_pair, gz_ref, C_GZ, p) for p in range(GDN_WIDTH // pair_w)]
             + [functools.partial(plain_pair, mv_ref, C_MV, p) for p in range(MOBA_WIDTH // pair_w)]
             + [functools.partial(rotary_pair, mq_ref, C_MQ, MOBA_Q_SCALE, p)
                for p in range(MOBA_WIDTH // pair_w)]
             + [functools.partial(rotary_pair, mk_ref, C_MK, None, p)
                for p in range(MOBA_WIDTH // pair_w)]
             + [gates])
    done = 0
    for n, conv in enumerate(convs, start=1):
        conv()
        upto = len(light) * n // len(convs)
        for stage in light[done:upto]:
            stage()
        done = upto


def _in_proj(x, nw, w_in, layer, conv_w, gparams, cos_t, sa_t, sb_t, *, tm):
    B, T, D = x.shape
    tok = lambda width: pl.BlockSpec((None, tm, width), lambda b, i: (b, i, 0))
    const = lambda shape: pl.BlockSpec(shape, lambda b, i: (0,) * len(shape))
    table = pl.BlockSpec((tm, LANES), lambda b, i: (i, 0))
    o512 = jax.ShapeDtypeStruct((B, T, GDN_WIDTH), BF16)
    return pl.pallas_call(
        functools.partial(_in_proj_body, tm=tm),
        grid=(B, T // tm),
        in_specs=[tok(D), const((1, D)),
                  pl.BlockSpec((None,) + w_in.shape[1:], lambda b, i: (layer, 0, 0),
                               pipeline_mode=pl.Buffered(1)),
                  const((GDN_CONV, 3 * GDN_WIDTH)),
                  const((2, LANES)), table, table, table],
        out_specs=[tok(GDN_WIDTH)] * 4 + [tok(LANES)] + [tok(MOBA_WIDTH)] * 3,
        out_shape=[o512] * 4 + [jax.ShapeDtypeStruct((B, T, LANES), F32)] + [o512] * 3,
        scratch_shapes=[pltpu.VMEM((CONV_HALO, 3 * GDN_WIDTH), F32),
                        pltpu.VMEM((D, C_END), BF16)],
        compiler_params=pltpu.CompilerParams(
            dimension_semantics=("arbitrary", "arbitrary"), vmem_limit_bytes=VMEM_LIMIT),
        name="in_proj",
    )(x, nw, w_in, conv_w, gparams, cos_t, sa_t, sb_t)


def _gdn_body(q_ref, k_ref, v_ref, z_ref, bg_ref, nw_ref, o_ref,
              s_ref, mneg_ref, n_ref, o1_ref, o2_ref, a_ref, snap_ref, *, T):
    C = GDN_CHUNK
    H = GDN_HEADS
    G = GDN_GROUP
    rows_per_trip = G * C
    ntrips = T // rows_per_trip
    ri = lax.broadcasted_iota(jnp.int32, (C, C), 0)
    ci = lax.broadcasted_iota(jnp.int32, (C, C), 1)
    causal = ci <= ri
    strict = ci < ri
    eye = (ri == ci).astype(F32)
    cols = [slice(h * HEAD_DIM, (h + 1) * HEAD_DIM) for h in range(H)]
    s_ref[...] = jnp.zeros_like(s_ref)

    def phase_a(row0, slot):
        inst = [(c, h) for c in range(G) for h in range(H)]
        rng = range(len(inst))
        r0s = [_aligned(row0 + c * C, C) for c in range(G)]
        bgs = [bg_ref[pl.ds(r0, C), :] for r0 in r0s]
        q = [q_ref[pl.ds(r0s[c], C), cols[h]].astype(F32) for c, h in inst]
        k = [k_ref[pl.ds(r0s[c], C), cols[h]].astype(F32) for c, h in inst]
        v = [v_ref[pl.ds(r0s[c], C), cols[h]].astype(F32) for c, h in inst]
        beta = [bgs[c][:, h:h + 1] for c, h in inst]
        gc_col, gc_last, decay = [], [], []
        for c, h in inst:
            gb = jnp.broadcast_to(bgs[c][:, H + h:H + h + 1], (C, C))
            row = jnp.sum(jnp.where(ri <= ci, gb, 0.0), axis=0, keepdims=True)
            g_row = jnp.sum(jnp.where(ri == ci, gb, 0.0), axis=0, keepdims=True)
            col = jnp.sum(jnp.where(causal, jnp.broadcast_to(g_row, (C, C)), 0.0),
                          axis=1, keepdims=True)
            gc_col.append(col)
            gc_last.append(row[:, C - 1:C])
            decay.append(jnp.exp(jnp.where(causal, col - row, -jnp.inf)))
        kb = [k[t] * beta[t] for t in rng]
        eg = [jnp.exp(gc_col[t]) for t in rng]
        kq = [_dot_nt(jnp.concatenate([kb[t], q[t]], axis=0), k[t]) for t in rng]
        yield
        low = [jnp.where(strict, kq[t][:C] * decay[t], 0.0) for t in rng]
        intra = [jnp.where(causal, kq[t][C:] * decay[t], 0.0) for t in rng]
        inv = [eye - low[t] for t in rng]
        pw = [_dot(low[t], low[t]) for t in rng]
        yield
        for _ in range(4):
            r = [_dot(jnp.concatenate([pw[t], inv[t]], axis=0), pw[t]) for t in rng]
            inv = [inv[t] + r[t][C:] for t in rng]
            pw = [r[t][:C] for t in rng]
            yield
        inv = [inv[t] + _dot(inv[t], pw[t]) for t in rng]
        yield
        wu = [_dot(inv[t], jnp.concatenate([kb[t] * eg[t], v[t] * beta[t]], axis=-1))
              for t in rng]
        yield
        kd = [k[t] * jnp.exp(gc_last[t] - gc_col[t]) for t in rng]
        r = [_dot(jnp.concatenate([kd[t].T, intra[t]], axis=0), wu[t]) for t in rng]
        mn = [r[t][:HEAD_DIM] for t in rng]
        io = [r[t][HEAD_DIM:] for t in rng]
        for t, (c, h) in enumerate(inst):
            rows = slice(c * C, (c + 1) * C)
            mneg_ref[slot, c, h] = (-mn[t][:, :HEAD_DIM]).astype(BF16)
            n_ref[slot, c, h] = mn[t][:, HEAD_DIM:]
            o1_ref[slot, rows, cols[h]] = (q[t] * eg[t] - io[t][:, :HEAD_DIM]).astype(BF16)
            o2_ref[slot, rows, cols[h]] = io[t][:, HEAD_DIM:]
            a_ref[slot, c, h] = jnp.broadcast_to(jnp.exp(gc_last[t]), (SUBLANES, HEAD_DIM))
        yield

    def phase_b_step(slot, c):
        hs = range(H)
        state = [s_ref[h] for h in hs]
        sb = [state[h].astype(BF16) for h in hs]
        ms = [jnp.dot(mneg_ref[slot, c, h], sb[h], preferred_element_type=F32) for h in hs]
        for h in hs:
            snap_ref[slot, c, h] = sb[h]
            s_ref[h] = state[h] * a_ref[slot, c, h, 0:1, :] + ms[h] + n_ref[slot, c, h]

    def phase_c(row0, slot):
        inst = [(c, h) for c in range(G) for h in range(H)]
        os_ = [jnp.dot(o1_ref[slot, c * C:(c + 1) * C, cols[h]], snap_ref[slot, c, h],
                       preferred_element_type=F32) for c, h in inst]
        for t, (c, h) in enumerate(inst):
            r0 = _aligned(row0 + c * C, C)
            o = os_[t] + o2_ref[slot, c * C:(c + 1) * C, cols[h]]
            z = z_ref[pl.ds(r0, C), cols[h]].astype(F32)
            o_ref[pl.ds(r0, C), cols[h]] = (_rms(o, nw_ref[...])
                                            * (z * _sigmoid(z))).astype(o_ref.dtype)

    def trip(g, do_a, do_b, do_c):
        row0 = lambda d: _aligned((g - d) * rows_per_trip, rows_per_trip)
        slot = lambda d: (g - d) % GDN_RING
        if do_c:
            phase_c(row0(2), slot(2))
        b_steps = [functools.partial(phase_b_step, slot(1), c) for c in range(G)] if do_b else []
        if do_a:
            for n, _ in enumerate(phase_a(row0(0), slot(0))):
                if n % 2 == 0 and b_steps:
                    b_steps.pop(0)()
        for step in b_steps:
            step()

    def steady(g, carry):
        trip(g, True, True, True)
        return carry

    all_on = lambda g: g < ntrips and 1 <= g <= ntrips and 2 <= g <= ntrips + 1
    g = 0
    while g < ntrips + 2:
        if all_on(g):
            last = g
            while all_on(last + 1):
                last += 1
            lax.fori_loop(g, last + 1, steady, 0)
            g = last + 1
        else:
            trip(g, g < ntrips, 1 <= g <= ntrips, 2 <= g <= ntrips + 1)
            g += 1


def _gdn(gq, gk, gv, gz, bg, nw):
    B, T, _ = gq.shape
    rows_per_trip = GDN_GROUP * GDN_CHUNK
    assert T % rows_per_trip == 0
    tok = lambda width: pl.BlockSpec((None, T, width), lambda b: (b, 0, 0))
    per_chunk = (GDN_RING, GDN_GROUP, GDN_HEADS)
    return pl.pallas_call(
        functools.partial(_gdn_body, T=T),
        grid=(B,),
        in_specs=[tok(GDN_WIDTH)] * 4 + [tok(LANES), pl.BlockSpec((1, HEAD_DIM), lambda b: (0, 0))],
        out_specs=tok(GDN_WIDTH),
        out_shape=jax.ShapeDtypeStruct((B, T, GDN_WIDTH), BF16),
        scratch_shapes=[
            pltpu.VMEM((GDN_HEADS, HEAD_DIM, HEAD_DIM), F32),
            pltpu.VMEM(per_chunk + (HEAD_DIM, HEAD_DIM), BF16),
            pltpu.VMEM(per_chunk + (HEAD_DIM, HEAD_DIM), F32),
            pltpu.VMEM((GDN_RING, rows_per_trip, GDN_WIDTH), BF16),
            pltpu.VMEM((GDN_RING, rows_per_trip, GDN_WIDTH), F32),
            pltpu.VMEM(per_chunk + (SUBLANES, HEAD_DIM), F32),
            pltpu.VMEM(per_chunk + (HEAD_DIM, HEAD_DIM), BF16),
        ],
        compiler_params=pltpu.CompilerParams(
            dimension_semantics=("arbitrary",), vmem_limit_bytes=VMEM_LIMIT),
        name="gdn",
    )(gq, gk, gv, gz, bg, nw)


def _moba_body(q_ref, k_ref, v_ref, o_ref, vt_ref, *, T):
    BS = MOBA_BLOCK
    nb = T // BS
    nrow = -(-nb // SUBLANES) * SUBLANES
    neg = -jnp.inf
    heads = [slice(h * HEAD_DIM, (h + 1) * HEAD_DIM) for h in range(MOBA_HEADS_PER_STEP)]
    blk = lax.broadcasted_iota(jnp.int32, (nrow, BS), 0)
    key = lax.broadcasted_iota(jnp.int32, (BS, BS), 0)
    qry = lax.broadcasted_iota(jnp.int32, (BS, BS), 1)

    kmean = []
    for n, hd in enumerate(heads):
        means = [jnp.mean(k_ref[j * BS:(j + 1) * BS, hd].astype(F32), axis=0, keepdims=True)
                 for j in range(nb)]
        if nrow > nb:
            means.append(jnp.zeros((nrow - nb, HEAD_DIM), F32))
        kmean.append(jnp.concatenate(means, axis=0))
        for j in range(nb):
            vt_ref[n, 0:HEAD_DIM, j * BS:(j + 1) * BS] = v_ref[j * BS:(j + 1) * BS, hd].T
        vt_ref[n, HEAD_DIM:, :] = jnp.ones((BF16_ROWS, T), BF16)

    def block_scores(n, i):
        q = q_ref[i * BS:(i + 1) * BS, heads[n]]
        sel = None
        if i > MOBA_TOPK:
            gate = lax.dot_general(kmean[n], q.astype(F32), (((1,), (1,)), ((), ())),
                                   precision=lax.Precision.HIGHEST, preferred_element_type=F32)
            gate = jnp.where(blk < i, gate, neg)
            rank = jnp.zeros((nrow, BS), F32)
            for jp in range(i):
                gj = gate[jp:jp + 1, :]
                beats = (gj > gate) | ((gj == gate) & (blk > jp))
                rank = rank + jnp.where(beats, 1.0, 0.0)
            sel = jnp.where((rank < MOBA_TOPK) & (blk < i), 1.0, 0.0)
        scores = []
        for j in range(i + 1):
            s = _dot_nt(k_ref[j * BS:(j + 1) * BS, heads[n]], q)
            if j == i:
                s = jnp.where(key <= qry, s, neg)
            elif sel is not None:
                s = jnp.where(sel[j:j + 1, :] > 0.5, s, neg)
            scores.append(s)
        return scores

    def block_output(n, i, scores):
        m = scores[0].max(axis=0, keepdims=True)
        for s in scores[1:]:
            m = jnp.maximum(m, s.max(axis=0, keepdims=True))
        acc = jnp.zeros((HEAD_DIM + BF16_ROWS, BS), F32)
        for j, s in enumerate(scores):
            acc = acc + _dot(vt_ref[n, :, j * BS:(j + 1) * BS], jnp.exp2(s - m))
        out = acc[:HEAD_DIM, :] / acc[HEAD_DIM:HEAD_DIM + 1, :]
        o_ref[i * BS:(i + 1) * BS, heads[n]] = out.T.astype(o_ref.dtype)

    hs = range(MOBA_HEADS_PER_STEP)
    scores = [block_scores(n, 0) for n in hs]
    for i in range(nb):
        following = [block_scores(n, i + 1) for n in hs] if i + 1 < nb else None
        for n in hs:
            block_output(n, i, scores[n])
        scores = following


def _moba(mq, mk, mv):
    B, T, _ = mq.shape
    assert T % MOBA_BLOCK == 0 and MOBA_HEADS % MOBA_HEADS_PER_STEP == 0
    width = MOBA_HEADS_PER_STEP * HEAD_DIM
    head = pl.BlockSpec((None, T, width), lambda b, h: (b, 0, h))
    return pl.pallas_call(
        functools.partial(_moba_body, T=T),
        grid=(B, MOBA_HEADS // MOBA_HEADS_PER_STEP),
        in_specs=[head] * 3,
        out_specs=head,
        out_shape=jax.ShapeDtypeStruct((B, T, MOBA_WIDTH), BF16),
        scratch_shapes=[pltpu.VMEM((MOBA_HEADS_PER_STEP, HEAD_DIM + BF16_ROWS, T), BF16)],
        compiler_params=pltpu.CompilerParams(
            dimension_semantics=("arbitrary", "arbitrary"), vmem_limit_bytes=VMEM_LIMIT),
        name="moba",
    )(mq, mk, mv)


def _mlp_body(og_ref, om_ref, x_ref, p_ref, wo_ref, n1_ref, n2_ref, wu_ref, wd_ref, n3_ref,
              wg_ref, wp_ref, o_ref, *, ff_chunk):
    parts = [slice(r0, r0 + MLP_PART_ROWS) for r0 in range(0, x_ref.shape[0], MLP_PART_ROWS)]
    ns = range(len(parts))
    mix = [_dot(og_ref[r, :], wo_ref[0:GDN_WIDTH, :])
           + _dot(om_ref[r, :], wo_ref[GDN_WIDTH:GDN_WIDTH + MOBA_WIDTH, :]) for r in parts]
    h = [x_ref[r, :] + _rms(mix[n], n1_ref[...]) for n, r in enumerate(parts)]
    a = [_rms(h[n], n2_ref[...]).astype(BF16) for n in ns]
    f = [None for _ in ns]
    for c0 in range(0, D_FF, ff_chunk):
        up = [jnp.dot(a[n], wu_ref[:, c0:c0 + ff_chunk], preferred_element_type=F32)
              for n in ns]
        for n in ns:
            part = _dot(jnp.square(jnp.maximum(up[n], 0.0)), wd_ref[c0:c0 + ff_chunk, :])
            f[n] = part if f[n] is None else f[n] + part
    h = [h[n] + _rms(f[n], n3_ref[...]) for n in ns]
    gate = [_dot(h[n], wg_ref[...]) for n in ns]
    ple = [_dot(p_ref[r, :], wp_ref[...]) for r in parts]
    for n, r in enumerate(parts):
        o_ref[r, :] = h[n] + _sigmoid(gate[n]) * ple[n]


def _mlp(og, om, x, p, wo, n1, n2, wu, wd, n3, wg, wp, *, tm):
    B, T, D = x.shape
    tok = lambda width: pl.BlockSpec((None, tm, width), lambda b, i: (b, i, 0))
    const = lambda shape: pl.BlockSpec(shape, lambda b, i: (0,) * len(shape),
                                       pipeline_mode=pl.Buffered(1))
    return pl.pallas_call(
        functools.partial(_mlp_body, ff_chunk=1024),
        grid=(B, T // tm),
        in_specs=[tok(GDN_WIDTH), tok(MOBA_WIDTH), tok(D), tok(PLE_DIM),
                  const((D, D)), const((1, D)), const((1, D)), const((D, D_FF)),
                  const((D_FF, D)), const((1, D)), const((D, D)), const((PLE_DIM, D))],
        out_specs=tok(D),
        out_shape=jax.ShapeDtypeStruct((B, T, D), F32),
        compiler_params=pltpu.CompilerParams(
            dimension_semantics=("arbitrary", "arbitrary"), vmem_limit_bytes=VMEM_LIMIT),
        name="mlp",
    )(og, om, x, p, wo, n1, n2, wu, wd, n3, wg, wp)


def _rope_tables(T):
    half = ROPE_DIMS // 2
    inv_freq = ROPE_THETA ** (-np.arange(half, dtype=np.float64) * (2.0 / ROPE_DIMS))
    ang = np.arange(T, dtype=np.float64)[:, None] * inv_freq[None, :]
    cos, sin = np.cos(ang), np.sin(ang)
    zeros = np.zeros((T, HEAD_DIM - ROPE_DIMS))
    z_half = np.zeros((T, half))
    cos_t = np.concatenate([cos, cos, np.ones((T, HEAD_DIM - ROPE_DIMS))], axis=-1)
    sa_t = np.concatenate([-sin, z_half, zeros], axis=-1)
    sb_t = np.concatenate([z_half, sin, zeros], axis=-1)
    cos_t, sa_t, sb_t = (jnp.asarray(t, dtype=F32) for t in (cos_t, sa_t, sb_t))
    return cos_t, sa_t, sb_t


def _layer(h, p_i, w_in, layer, conv_w, a_log, dt_bias, gdn_norm_w, w_out, attn_pre_norm,
           attn_post_norm, mlp_pre_norm, mlp_post_norm, w_up, w_down, w_ple, w_ple_gate, tables):
    B, T, D = h.shape
    tm = next(t for t in (1024, 512, MOBA_BLOCK) if T % t == 0)
    assert T % tm == 0 and T % MOBA_BLOCK == 0 and T % GDN_CHUNK == 0
    nh = 2 * GDN_HEADS
    gparams = jnp.zeros((2, LANES), F32)
    gparams = gparams.at[0, GDN_HEADS:nh].set(a_log.astype(F32))
    gparams = gparams.at[1, GDN_HEADS:nh].set(dt_bias.astype(F32))
    row = lambda v: v.reshape(1, -1).astype(F32)

    gq, gk, gv, gz, bg, mq, mk, mv = _in_proj(
        h, row(attn_pre_norm), jnp.swapaxes(w_in.astype(F32), 1, 2), layer, conv_w.astype(F32),
        gparams, *tables,
        tm=tm)
    o_gdn = _gdn(gq, gk, gv, gz, bg, row(gdn_norm_w))
    o_moba = _moba(mq, mk, mv)
    return _mlp(o_gdn, o_moba, h, p_i, w_out.astype(BF16), row(attn_post_norm),
                row(mlp_pre_norm), w_up.astype(BF16), w_down.astype(BF16),
                row(mlp_post_norm), w_ple_gate.astype(BF16), w_ple.astype(BF16), tm=tm)


def kernel(x, p, w_in, conv_w, a_log, dt_bias, gdn_norm_w, w_out, attn_pre_norm, attn_post_norm,
           mlp_pre_norm, mlp_post_norm, w_up, w_down, w_ple, w_ple_gate):
    tables = _rope_tables(x.shape[1])
    h = x
    for i in range(w_in.shape[0]):
        h = _layer(h, p[i], w_in, i, conv_w[i], a_log[i], dt_bias[i], gdn_norm_w[i], w_out[i],
                   attn_pre_norm[i], attn_post_norm[i], mlp_pre_norm[i], mlp_post_norm[i],
                   w_up[i], w_down[i], w_ple[i], w_ple_gate[i], tables)
    return h
```

```python
import functools
import math

import jax
import jax.numpy as jnp
import numpy as np
from jax import lax
from jax.experimental import pallas as pl
from jax.experimental.pallas import tpu as pltpu

F32 = jnp.float32
BF16 = jnp.bfloat16

D_MODEL = 1024
HEAD_DIM = 128
GDN_HEADS = 4
GDN_WIDTH = GDN_HEADS * HEAD_DIM
GDN_CONV = 4
GDN_CHUNK = 64
GDN_GROUP = 4
GDN_RING = 3
MOBA_HEADS = 4
MOBA_WIDTH = MOBA_HEADS * HEAD_DIM
MOBA_BLOCK = 256
MOBA_TOPK = 3
MOBA_HEADS_PER_STEP = 2
ROPE_DIMS = 32
ROPE_THETA = 500000.0
D_FF = 4 * D_MODEL
PLE_DIM = 256
RMS_EPS = 1e-6
LANES = 128
SUBLANES = 8
BF16_ROWS = 16
CONV_HALO = 8
REPACK_ROWS = 128
MLP_PART_ROWS = 256
IN_PART_ROWS = 256
QK_SCALE = HEAD_DIM ** -0.5
MOBA_Q_SCALE = QK_SCALE * math.log2(math.e)
VMEM_LIMIT = 56 * 1024 * 1024

C_GQKV = 0
C_GZ = 3 * GDN_WIDTH
C_MQ = 4 * GDN_WIDTH
C_MK = C_MQ + MOBA_WIDTH
C_MV = C_MK + MOBA_WIDTH
C_BA = C_MV + MOBA_WIDTH
C_END = C_BA + LANES


def _rms(x, w):
    return x * lax.rsqrt(jnp.mean(x * x, axis=-1, keepdims=True) + RMS_EPS) * w


def _sigmoid(x):
    return 1.0 / (1.0 + jnp.exp(-x))


def _aligned(x, m):
    return x if isinstance(x, int) else pl.multiple_of(x, m)


def _dot(a, b):
    return jnp.dot(a.astype(BF16), b.astype(BF16), preferred_element_type=F32)


def _dot_nt(a, b):
    return lax.dot_general(a.astype(BF16), b.astype(BF16), (((1,), (1,)), ((), ())),
                           preferred_element_type=F32)


def _in_proj_body(x_ref, nw_ref, win_ref, cw_ref, gp_ref, cos_ref, sa_ref, sb_ref,
                  gq_ref, gk_ref, gv_ref, gz_ref, bg_ref, mq_ref, mk_ref, mv_ref,
                  halo, w_ref, *, tm):
    i = pl.program_id(1)

    @pl.when((pl.program_id(0) == 0) & (i == 0))
    def _():
        n_gate = 2 * GDN_HEADS
        for c0 in range(0, C_BA, REPACK_ROWS):
            src = c0 if c0 < C_MQ else c0 + n_gate
            w_ref[:, c0:c0 + REPACK_ROWS] = win_ref[src:src + REPACK_ROWS, :].T.astype(BF16)
        gate_rows = jnp.concatenate([win_ref[C_MQ:C_MQ + n_gate, :],
                                     jnp.zeros((LANES - n_gate, D_MODEL), F32)], axis=0)
        w_ref[:, C_BA:C_END] = gate_rows.T.astype(BF16)

    parts = [slice(r0, r0 + IN_PART_ROWS) for r0 in range(0, tm, IN_PART_ROWS)]
    ns = range(len(parts))
    u = [None for _ in ns]

    def proj(n, c0, width):
        if u[n] is None:
            u[n] = _rms(x_ref[parts[n], :], nw_ref[...]).astype(BF16)
        return jnp.dot(u[n], w_ref[:, c0:c0 + width], preferred_element_type=F32)

    @pl.when(i == 0)
    def _():
        halo[...] = jnp.zeros_like(halo)

    pair_w = 2 * HEAD_DIM

    def conv_pair(p):
        pairs = [proj(n, C_GQKV + p * pair_w, pair_w) for n in ns]
        for s in (2 * p, 2 * p + 1):
            c0 = s * HEAD_DIM
            cw = [cw_ref[j:j + 1, c0:c0 + HEAD_DIM] for j in range(GDN_CONV)]
            which, h = divmod(s, GDN_HEADS)
            cur = [pairs[n][:, (s % 2) * HEAD_DIM:(s % 2 + 1) * HEAD_DIM] for n in ns]
            prev = halo[:, c0:c0 + HEAD_DIM]
            halo[:, c0:c0 + HEAD_DIM] = cur[-1][IN_PART_ROWS - CONV_HALO:, :]
            for n in ns:
                xs = jnp.concatenate([prev, cur[n]], axis=0)
                prev = cur[n][IN_PART_ROWS - CONV_HALO:, :]
                x1 = pltpu.roll(xs, 1, 0)
                acc = (cw[3] * xs + cw[2] * x1) + pltpu.roll(cw[1] * xs + cw[0] * x1, 2, 0)
                acc = acc[CONV_HALO:, :]
                y = acc * _sigmoid(acc)
                if which < 2:
                    inv_norm = lax.rsqrt(jnp.sum(y * y, axis=-1, keepdims=True) + RMS_EPS)
                    y = y * (inv_norm * QK_SCALE if which == 0 else inv_norm)
                (gq_ref, gk_ref, gv_ref)[which][parts[n], h * HEAD_DIM:(h + 1) * HEAD_DIM] = (
                    y.astype(BF16))

    def plain_pair(ref, c_base, p):
        vals = [proj(n, c_base + p * pair_w, pair_w) for n in ns]
        for n in ns:
            ref[parts[n], p * pair_w:(p + 1) * pair_w] = vals[n].astype(BF16)

    def rotary_pair(ref, c_base, scale, p):
        half = ROPE_DIMS // 2
        pairs = [proj(n, c_base + p * pair_w, pair_w) for n in ns]
        for n in ns:
            cos, sa, sb = cos_ref[parts[n], :], sa_ref[parts[n], :], sb_ref[parts[n], :]
            for h in (2 * p, 2 * p + 1):
                xr = pairs[n][:, (h % 2) * HEAD_DIM:(h % 2 + 1) * HEAD_DIM]
                rot = (xr * cos + pltpu.roll(xr, LANES - half, 1) * sa
                       + pltpu.roll(xr, half, 1) * sb)
                if scale is not None:
                    rot = rot * scale
                ref[parts[n], h * HEAD_DIM:(h + 1) * HEAD_DIM] = rot.astype(BF16)

    def gates():
        lane = lax.broadcasted_iota(jnp.int32, (IN_PART_ROWS, LANES), 1)
        bas = [proj(n, C_BA, LANES) for n in ns]
        for n in ns:
            xg = bas[n] + gp_ref[1:2, :]
            e = jnp.exp(-jnp.abs(xg))
            r = 1.0 / (1.0 + e)
            sig = jnp.where(xg >= 0.0, r, e * r)
            g = -jnp.exp(gp_ref[0:1, :]) * (jnp.maximum(xg, 0.0) + jnp.log1p(e))
            bg_ref[parts[n], :] = jnp.where(lane < GDN_HEADS, sig,
                                            jnp.where(lane < 2 * GDN_HEADS, g, 0.0))

    convs = [functools.partial(conv_pair, p) for p in range(3 * GDN_WIDTH // pair_w)]
    light = ([functools.partial(plain_pair, gz_ref, C_GZ, p) for p in range(GDN_WIDTH // pair_w)]
             + [functools.partial(plain_pair, mv_ref, C_MV, p) for p in range(MOBA_WIDTH // pair_w)]
             + [functools.partial(rotary_pair, mq_ref, C_MQ, MOBA_Q_SCALE, p)
                for p in range(MOBA_WIDTH // pair_w)]
             + [functools.partial(rotary_pair, mk_ref, C_MK, None, p)
                for p in range(MOBA_WIDTH // pair_w)]
             + [gates])
    done = 0
    for n, conv in enumerate(convs, start=1):
        conv()
        upto = len(light) * n // len(convs)
        for stage in light[done:upto]:
            stage()
        done = upto


def _in_proj(x, nw, w_in, layer, conv_w, gparams, cos_t, sa_t, sb_t, *, tm):
    B, T, D = x.shape
    tok = lambda width: pl.BlockSpec((None, tm, width), lambda b, i: (b, i, 0))
    const = lambda shape: pl.BlockSpec(shape, lambda b, i: (0,) * len(shape))
    table = pl.BlockSpec((tm, LANES), lambda b, i: (i, 0))
    o512 = jax.ShapeDtypeStruct((B, T, GDN_WIDTH), BF16)
    return pl.pallas_call(
        functools.partial(_in_proj_body, tm=tm),
        grid=(B, T // tm),
        in_specs=[tok(D), const((1, D)),
                  pl.BlockSpec((None,) + w_in.shape[1:], lambda b, i: (layer, 0, 0),
                               pipeline_mode=pl.Buffered(1)),
                  const((GDN_CONV, 3 * GDN_WIDTH)),
                  const((2, LANES)), table, table, table],
        out_specs=[tok(GDN_WIDTH)] * 4 + [tok(LANES)] + [tok(MOBA_WIDTH)] * 3,
        out_shape=[o512] * 4 + [jax.ShapeDtypeStruct((B, T, LANES), F32)] + [o512] * 3,
        scratch_shapes=[pltpu.VMEM((CONV_HALO, 3 * GDN_WIDTH), F32),
                        pltpu.VMEM((D, C_END), BF16)],
        compiler_params=pltpu.CompilerParams(
            dimension_semantics=("arbitrary", "arbitrary"), vmem_limit_bytes=VMEM_LIMIT),
        name="in_proj",
    )(x, nw, w_in, conv_w, gparams, cos_t, sa_t, sb_t)


def _gdn_body(q_ref, k_ref, v_ref, z_ref, bg_ref, nw_ref, o_ref,
              s_ref, mneg_ref, n_ref, o1_ref, o2_ref, a_ref, snap_ref, *, T):
    C = GDN_CHUNK
    H = GDN_HEADS
    G = GDN_GROUP
    rows_per_trip = G * C
    ntrips = T // rows_per_trip
    ri = lax.broadcasted_iota(jnp.int32, (C, C), 0)
    ci = lax.broadcasted_iota(jnp.int32, (C, C), 1)
    causal = ci <= ri
    strict = ci < ri
    eye = (ri == ci).astype(F32)
    cols = [slice(h * HEAD_DIM, (h + 1) * HEAD_DIM) for h in range(H)]
    s_ref[...] = jnp.zeros_like(s_ref)

    def phase_a(row0, slot):
        inst = [(c, h) for c in range(G) for h in range(H)]
        rng = range(len(inst))
        r0s = [_aligned(row0 + c * C, C) for c in range(G)]
        bgs = [bg_ref[pl.ds(r0, C), :] for r0 in r0s]
        q = [q_ref[pl.ds(r0s[c], C), cols[h]].astype(F32) for c, h in inst]
        k = [k_ref[pl.ds(r0s[c], C), cols[h]].astype(F32) for c, h in inst]
        v = [v_ref[pl.ds(r0s[c], C), cols[h]].astype(F32) for c, h in inst]
        beta = [bgs[c][:, h:h + 1] for c, h in inst]
        gc_col, gc_last, decay = [], [], []
        for c, h in inst:
            gb = jnp.broadcast_to(bgs[c][:, H + h:H + h + 1], (C, C))
            row = jnp.sum(jnp.where(ri <= ci, gb, 0.0), axis=0, keepdims=True)
            g_row = jnp.sum(jnp.where(ri == ci, gb, 0.0), axis=0, keepdims=True)
            col = jnp.sum(jnp.where(causal, jnp.broadcast_to(g_row, (C, C)), 0.0),
                          axis=1, keepdims=True)
            gc_col.append(col)
            gc_last.append(row[:, C - 1:C])
            decay.append(jnp.exp(jnp.where(causal, col - row, -jnp.inf)))
        kb = [k[t] * beta[t] for t in rng]
        eg = [jnp.exp(gc_col[t]) for t in rng]
        kq = [_dot_nt(jnp.concatenate([kb[t], q[t]], axis=0), k[t]) for t in rng]
        yield
        low = [jnp.where(strict, kq[t][:C] * decay[t], 0.0) for t in rng]
        intra = [jnp.where(causal, kq[t][C:] * decay[t], 0.0) for t in rng]
        inv = [eye - low[t] for t in rng]
        pw = [_dot(low[t], low[t]) for t in rng]
        yield
        for _ in range(4):
            r = [_dot(jnp.concatenate([pw[t], inv[t]], axis=0), pw[t]) for t in rng]
            inv = [inv[t] + r[t][C:] for t in rng]
            pw = [r[t][:C] for t in rng]
            yield
        inv = [inv[t] + _dot(inv[t], pw[t]) for t in rng]
        yield
        wu = [_dot(inv[t], jnp.concatenate([kb[t] * eg[t], v[t] * beta[t]], axis=-1))
              for t in rng]
        yield
        kd = [k[t] * jnp.exp(gc_last[t] - gc_col[t]) for t in rng]
        r = [_dot(jnp.concatenate([kd[t].T, intra[t]], axis=0), wu[t]) for t in rng]
        mn = [r[t][:HEAD_DIM] for t in rng]
        io = [r[t][HEAD_DIM:] for t in rng]
        for t, (c, h) in enumerate(inst):
            rows = slice(c * C, (c + 1) * C)
            mneg_ref[slot, c, h] = (-mn[t][:, :HEAD_DIM]).astype(BF16)
            n_ref[slot, c, h] = mn[t][:, HEAD_DIM:]
            o1_ref[slot, rows, cols[h]] = (q[t] * eg[t] - io[t][:, :HEAD_DIM]).astype(BF16)
            o2_ref[slot, rows, cols[h]] = io[t][:, HEAD_DIM:]
            a_ref[slot, c, h] = jnp.broadcast_to(jnp.exp(gc_last[t]), (SUBLANES, HEAD_DIM))
        yield

    def phase_b_step(slot, c):
        hs = range(H)
        state = [s_ref[h] for h in hs]
        sb = [state[h].astype(BF16) for h in hs]
        ms = [jnp.dot(mneg_ref[slot, c, h], sb[h], preferred_element_type=F32) for h in hs]
        for h in hs:
            snap_ref[slot, c, h] = sb[h]
            s_ref[h] = state[h] * a_ref[slot, c, h, 0:1, :] + ms[h] + n_ref[slot, c, h]

    def phase_c(row0, slot):
        inst = [(c, h) for c in range(G) for h in range(H)]
        os_ = [jnp.dot(o1_ref[slot, c * C:(c + 1) * C, cols[h]], snap_ref[slot, c, h],
                       preferred_element_type=F32) for c, h in inst]
        for t, (c, h) in enumerate(inst):
            r0 = _aligned(row0 + c * C, C)
            o = os_[t] + o2_ref[slot, c * C:(c + 1) * C, cols[h]]
            z = z_ref[pl.ds(r0, C), cols[h]].astype(F32)
            o_ref[pl.ds(r0, C), cols[h]] = (_rms(o, nw_ref[...])
                                            * (z * _sigmoid(z))).astype(o_ref.dtype)

    def trip(g, do_a, do_b, do_c):
        row0 = lambda d: _aligned((g - d) * rows_per_trip, rows_per_trip)
        slot = lambda d: (g - d) % GDN_RING
        if do_c:
            phase_c(row0(2), slot(2))
        b_steps = [functools.partial(phase_b_step, slot(1), c) for c in range(G)] if do_b else []
        if do_a:
            for n, _ in enumerate(phase_a(row0(0), slot(0))):
                if n % 2 == 0 and b_steps:
                    b_steps.pop(0)()
        for step in b_steps:
            step()

    def steady(g, carry):
        trip(g, True, True, True)
        return carry

    all_on = lambda g: g < ntrips and 1 <= g <= ntrips and 2 <= g <= ntrips + 1
    g = 0
    while g < ntrips + 2:
        if all_on(g):
            last = g
            while all_on(last + 1):
                last += 1
            lax.fori_loop(g, last + 1, steady, 0)
            g = last + 1
        else:
            trip(g, g < ntrips, 1 <= g <= ntrips, 2 <= g <= ntrips + 1)
            g += 1


def _gdn(gq, gk, gv, gz, bg, nw):
    B, T, _ = gq.shape
    rows_per_trip = GDN_GROUP * GDN_CHUNK
    assert T % rows_per_trip == 0
    tok = lambda width: pl.BlockSpec((None, T, width), lambda b: (b, 0, 0))
    per_chunk = (GDN_RING, GDN_GROUP, GDN_HEADS)
    return pl.pallas_call(
        functools.partial(_gdn_body, T=T),
        grid=(B,),
        in_specs=[tok(GDN_WIDTH)] * 4 + [tok(LANES), pl.BlockSpec((1, HEAD_DIM), lambda b: (0, 0))],
        out_specs=tok(GDN_WIDTH),
        out_shape=jax.ShapeDtypeStruct((B, T, GDN_WIDTH), BF16),
        scratch_shapes=[
            pltpu.VMEM((GDN_HEADS, HEAD_DIM, HEAD_DIM), F32),
            pltpu.VMEM(per_chunk + (HEAD_DIM, HEAD_DIM), BF16),
            pltpu.VMEM(per_chunk + (HEAD_DIM, HEAD_DIM), F32),
            pltpu.VMEM((GDN_RING, rows_per_trip, GDN_WIDTH), BF16),
            pltpu.VMEM((GDN_RING, rows_per_trip, GDN_WIDTH), F32),
            pltpu.VMEM(per_chunk + (SUBLANES, HEAD_DIM), F32),
            pltpu.VMEM(per_chunk + (HEAD_DIM, HEAD_DIM), BF16),
        ],
        compiler_params=pltpu.CompilerParams(
            dimension_semantics=("arbitrary",), vmem_limit_bytes=VMEM_LIMIT),
        name="gdn",
    )(gq, gk, gv, gz, bg, nw)


def _moba_body(q_ref, k_ref, v_ref, o_ref, vt_ref, *, T):
    BS = MOBA_BLOCK
    nb = T // BS
    nrow = -(-nb // SUBLANES) * SUBLANES
    neg = -jnp.inf
    heads = [slice(h * HEAD_DIM, (h + 1) * HEAD_DIM) for h in range(MOBA_HEADS_PER_STEP)]
    blk = lax.broadcasted_iota(jnp.int32, (nrow, BS), 0)
    key = lax.broadcasted_iota(jnp.int32, (BS, BS), 0)
    qry = lax.broadcasted_iota(jnp.int32, (BS, BS), 1)

    kmean = []
    for n, hd in enumerate(heads):
        means = [jnp.mean(k_ref[j * BS:(j + 1) * BS, hd].astype(F32), axis=0, keepdims=True)
                 for j in range(nb)]
        if nrow > nb:
            means.append(jnp.zeros((nrow - nb, HEAD_DIM), F32))
        kmean.append(jnp.concatenate(means, axis=0))
        for j in range(nb):
            vt_ref[n, 0:HEAD_DIM, j * BS:(j + 1) * BS] = v_ref[j * BS:(j + 1) * BS, hd].T
        vt_ref[n, HEAD_DIM:, :] = jnp.ones((BF16_ROWS, T), BF16)

    def block_scores(n, i):
        q = q_ref[i * BS:(i + 1) * BS, heads[n]]
        sel = None
        if i > MOBA_TOPK:
            gate = lax.dot_general(kmean[n], q.astype(F32), (((1,), (1,)), ((), ())),
                                   precision=lax.Precision.HIGHEST, preferred_element_type=F32)
            gate = jnp.where(blk < i, gate, neg)
            rank = jnp.zeros((nrow, BS), F32)
            for jp in range(i):
                gj = gate[jp:jp + 1, :]
                beats = (gj > gate) | ((gj == gate) & (blk > jp))
                rank = rank + jnp.where(beats, 1.0, 0.0)
            sel = jnp.where((rank < MOBA_TOPK) & (blk < i), 1.0, 0.0)
        scores = []
        for j in range(i + 1):
            s = _dot_nt(k_ref[j * BS:(j + 1) * BS, heads[n]], q)
            if j == i:
                s = jnp.where(key <= qry, s, neg)
            elif sel is not None:
                s = jnp.where(sel[j:j + 1, :] > 0.5, s, neg)
            scores.append(s)
        return scores

    def block_output(n, i, scores):
        m = scores[0].max(axis=0, keepdims=True)
        for s in scores[1:]:
            m = jnp.maximum(m, s.max(axis=0, keepdims=True))
        acc = jnp.zeros((HEAD_DIM + BF16_ROWS, BS), F32)
        for j, s in enumerate(scores):
            acc = acc + _dot(vt_ref[n, :, j * BS:(j + 1) * BS], jnp.exp2(s - m))
        out = acc[:HEAD_DIM, :] / acc[HEAD_DIM:HEAD_DIM + 1, :]
        o_ref[i * BS:(i + 1) * BS, heads[n]] = out.T.astype(o_ref.dtype)

    hs = range(MOBA_HEADS_PER_STEP)
    scores = [block_scores(n, 0) for n in hs]
    for i in range(nb):
        following = [block_scores(n, i + 1) for n in hs] if i + 1 < nb else None
        for n in hs:
            block_output(n, i, scores[n])
        scores = following


def _moba(mq, mk, mv):
    B, T, _ = mq.shape
    assert T % MOBA_BLOCK == 0 and MOBA_HEADS % MOBA_HEADS_PER_STEP == 0
    width = MOBA_HEADS_PER_STEP * HEAD_DIM
    head = pl.BlockSpec((None, T, width), lambda b, h: (b, 0, h))
    return pl.pallas_call(
        functools.partial(_moba_body, T=T),
        grid=(B, MOBA_HEADS // MOBA_HEADS_PER_STEP),
        in_specs=[head] * 3,
        out_specs=head,
        out_shape=jax.ShapeDtypeStruct((B, T, MOBA_WIDTH), BF16),
        scratch_shapes=[pltpu.VMEM((MOBA_HEADS_PER_STEP, HEAD_DIM + BF16_ROWS, T), BF16)],
        compiler_params=pltpu.CompilerParams(
            dimension_semantics=("arbitrary", "arbitrary"), vmem_limit_bytes=VMEM_LIMIT),
        name="moba",
    )(mq, mk, mv)


def _mlp_body(og_ref, om_ref, x_ref, p_ref, wo_ref, n1_ref, n2_ref, wu_ref, wd_ref, n3_ref,
              wg_ref, wp_ref, o_ref, *, ff_chunk):
    parts = [slice(r0, r0 + MLP_PART_ROWS) for r0 in range(0, x_ref.shape[0], MLP_PART_ROWS)]
    ns = range(len(parts))
    mix = [_dot(og_ref[r, :], wo_ref[0:GDN_WIDTH, :])
           + _dot(om_ref[r, :], wo_ref[GDN_WIDTH:GDN_WIDTH + MOBA_WIDTH, :]) for r in parts]
    h = [x_ref[r, :] + _rms(mix[n], n1_ref[...]) for n, r in enumerate(parts)]
    a = [_rms(h[n], n2_ref[...]).astype(BF16) for n in ns]
    f = [None for _ in ns]
    for c0 in range(0, D_FF, ff_chunk):
        up = [jnp.dot(a[n], wu_ref[:, c0:c0 + ff_chunk], preferred_element_type=F32)
              for n in ns]
        for n in ns:
            part = _dot(jnp.square(jnp.maximum(up[n], 0.0)), wd_ref[c0:c0 + ff_chunk, :])
            f[n] = part if f[n] is None else f[n] + part
    h = [h[n] + _rms(f[n], n3_ref[...]) for n in ns]
    gate = [_dot(h[n], wg_ref[...]) for n in ns]
    ple = [_dot(p_ref[r, :], wp_ref[...]) for r in parts]
    for n, r in enumerate(parts):
        o_ref[r, :] = h[n] + _sigmoid(gate[n]) * ple[n]


def _mlp(og, om, x, p, wo, n1, n2, wu, wd, n3, wg, wp, *, tm):
    B, T, D = x.shape
    tok = lambda width: pl.BlockSpec((None, tm, width), lambda b, i: (b, i, 0))
    const = lambda shape: pl.BlockSpec(shape, lambda b, i: (0,) * len(shape),
                                       pipeline_mode=pl.Buffered(1))
    return pl.pallas_call(
        functools.partial(_mlp_body, ff_chunk=1024),
        grid=(B, T // tm),
        in_specs=[tok(GDN_WIDTH), tok(MOBA_WIDTH), tok(D), tok(PLE_DIM),
                  const((D, D)), const((1, D)), const((1, D)), const((D, D_FF)),
                  const((D_FF, D)), const((1, D)), const((D, D)), const((PLE_DIM, D))],
        out_specs=tok(D),
        out_shape=jax.ShapeDtypeStruct((B, T, D), F32),
        compiler_params=pltpu.CompilerParams(
            dimension_semantics=("arbitrary", "arbitrary"), vmem_limit_bytes=VMEM_LIMIT),
        name="mlp",
    )(og, om, x, p, wo, n1, n2, wu, wd, n3, wg, wp)


def _rope_tables(T):
    half = ROPE_DIMS // 2
    inv_freq = ROPE_THETA ** (-np.arange(half, dtype=np.float64) * (2.0 / ROPE_DIMS))
    ang = np.arange(T, dtype=np.float64)[:, None] * inv_freq[None, :]
    cos, sin = np.cos(ang), np.sin(ang)
    zeros = np.zeros((T, HEAD_DIM - ROPE_DIMS))
    z_half = np.zeros((T, half))
    cos_t = np.concatenate([cos, cos, np.ones((T, HEAD_DIM - ROPE_DIMS))], axis=-1)
    sa_t = np.concatenate([-sin, z_half, zeros], axis=-1)
    sb_t = np.concatenate([z_half, sin, zeros], axis=-1)
    cos_t, sa_t, sb_t = (jnp.asarray(t, dtype=F32) for t in (cos_t, sa_t, sb_t))
    return cos_t, sa_t, sb_t


def _layer(h, p_i, w_in, layer, conv_w, a_log, dt_bias, gdn_norm_w, w_out, attn_pre_norm,
           attn_post_norm, mlp_pre_norm, mlp_post_norm, w_up, w_down, w_ple, w_ple_gate, tables):
    B, T, D = h.shape
    tm = next(t for t in (512, MOBA_BLOCK) if T % t == 0)
    assert T % tm == 0 and T % MOBA_BLOCK == 0 and T % GDN_CHUNK == 0
    nh = 2 * GDN_HEADS
    gparams = jnp.zeros((2, LANES), F32)
    gparams = gparams.at[0, GDN_HEADS:nh].set(a_log.astype(F32))
    gparams = gparams.at[1, GDN_HEADS:nh].set(dt_bias.astype(F32))
    row = lambda v: v.reshape(1, -1).astype(F32)

    gq, gk, gv, gz, bg, mq, mk, mv = _in_proj(
        h, row(attn_pre_norm), jnp.swapaxes(w_in.astype(F32), 1, 2), layer, conv_w.astype(F32),
        gparams, *tables,
        tm=tm)
    o_gdn = _gdn(gq, gk, gv, gz, bg, row(gdn_norm_w))
    o_moba = _moba(mq, mk, mv)
    return _mlp(o_gdn, o_moba, h, p_i, w_out.astype(BF16), row(attn_post_norm),
                row(mlp_pre_norm), w_up.astype(BF16), w_down.astype(BF16),
                row(mlp_post_norm), w_ple_gate.astype(BF16), w_ple.astype(BF16), tm=tm)


def kernel(x, p, w_in, conv_w, a_log, dt_bias, gdn_norm_w, w_out, attn_pre_norm, attn_post_norm,
           mlp_pre_norm, mlp_post_norm, w_up, w_down, w_ple, w_ple_gate):
    tables = _rope_tables(x.shape[1])
    h = x
    for i in range(w_in.shape[0]):
        h = _layer(h, p[i], w_in, i, conv_w[i], a_log[i], dt_bias[i], gdn_norm_w[i], w_out[i],
                   attn_pre_norm[i], attn_post_norm[i], mlp_pre_norm[i], mlp_post_norm[i],
                   w_up[i], w_down[i], w_ple[i], w_ple_gate[i], tables)
    return h
```

```python
import functools
import math

import jax
import jax.numpy as jnp
import numpy as np
from jax import lax
from jax.experimental import pallas as pl
from jax.experimental.pallas import tpu as pltpu

F32 = jnp.float32
BF16 = jnp.bfloat16

D_MODEL = 1024
HEAD_DIM = 128
GDN_HEADS = 4
GDN_WIDTH = GDN_HEADS * HEAD_DIM
GDN_CONV = 4
GDN_CHUNK = 64
GDN_GROUP = 4
GDN_RING = 3
MOBA_HEADS = 4
MOBA_WIDTH = MOBA_HEADS * HEAD_DIM
MOBA_BLOCK = 256
MOBA_TOPK = 3
MOBA_HEADS_PER_STEP = 2
ROPE_DIMS = 32
ROPE_THETA = 500000.0
D_FF = 4 * D_MODEL
PLE_DIM = 256
RMS_EPS = 1e-6
LANES = 128
SUBLANES = 8
BF16_ROWS = 16
CONV_HALO = 8
REPACK_ROWS = 128
MLP_PART_ROWS = 256
IN_PART_ROWS = 256
QK_SCALE = HEAD_DIM ** -0.5
MOBA_Q_SCALE = QK_SCALE * math.log2(math.e)
VMEM_LIMIT = 56 * 1024 * 1024

C_GQKV = 0
C_GZ = 3 * GDN_WIDTH
C_MQ = 4 * GDN_WIDTH
C_MK = C_MQ + MOBA_WIDTH
C_MV = C_MK + MOBA_WIDTH
C_BA = C_MV + MOBA_WIDTH
C_END = C_BA + LANES


def _rms(x, w):
    return x * lax.rsqrt(jnp.mean(x * x, axis=-1, keepdims=True) + RMS_EPS) * w


def _sigmoid(x):
    return 1.0 / (1.0 + jnp.exp(-x))


def _aligned(x, m):
    return x if isinstance(x, int) else pl.multiple_of(x, m)


def _dot(a, b):
    return jnp.dot(a.astype(BF16), b.astype(BF16), preferred_element_type=F32)


def _dot_nt(a, b):
    return lax.dot_general(a.astype(BF16), b.astype(BF16), (((1,), (1,)), ((), ())),
                           preferred_element_type=F32)


def _in_proj_body(x_ref, nw_ref, win_ref, cw_ref, gp_ref, cos_ref, sa_ref, sb_ref,
                  gq_ref, gk_ref, gv_ref, gz_ref, bg_ref, mq_ref, mk_ref, mv_ref,
                  halo, w_ref, *, tm):
    i = pl.program_id(1)

    @pl.when((pl.program_id(0) == 0) & (i == 0))
    def _():
        n_gate = 2 * GDN_HEADS
        for c0 in range(0, C_BA, REPACK_ROWS):
            src = c0 if c0 < C_MQ else c0 + n_gate
            w_ref[:, c0:c0 + REPACK_ROWS] = win_ref[src:src + REPACK_ROWS, :].T.astype(BF16)
        gate_rows = jnp.concatenate([win_ref[C_MQ:C_MQ + n_gate, :],
                                     jnp.zeros((LANES - n_gate, D_MODEL), F32)], axis=0)
        w_ref[:, C_BA:C_END] = gate_rows.T.astype(BF16)

    parts = [slice(r0, r0 + IN_PART_ROWS) for r0 in range(0, tm, IN_PART_ROWS)]
    ns = range(len(parts))
    u = [None for _ in ns]

    def proj(n, c0, width):
        if u[n] is None:
            u[n] = _rms(x_ref[parts[n], :], nw_ref[...]).astype(BF16)
        return jnp.dot(u[n], w_ref[:, c0:c0 + width], preferred_element_type=F32)

    @pl.when(i == 0)
    def _():
        halo[...] = jnp.zeros_like(halo)

    pair_w = 2 * HEAD_DIM

    def conv_pair(p):
        pairs = [proj(n, C_GQKV + p * pair_w, pair_w) for n in ns]
        for s in (2 * p, 2 * p + 1):
            c0 = s * HEAD_DIM
            cw = [cw_ref[j:j + 1, c0:c0 + HEAD_DIM] for j in range(GDN_CONV)]
            which, h = divmod(s, GDN_HEADS)
            cur = [pairs[n][:, (s % 2) * HEAD_DIM:(s % 2 + 1) * HEAD_DIM] for n in ns]
            prev = halo[:, c0:c0 + HEAD_DIM]
            halo[:, c0:c0 + HEAD_DIM] = cur[-1][IN_PART_ROWS - CONV_HALO:, :]
            for n in ns:
                xs = jnp.concatenate([prev, cur[n]], axis=0)
                prev = cur[n][IN_PART_ROWS - CONV_HALO:, :]
                x1 = pltpu.roll(xs, 1, 0)
                acc = (cw[3] * xs + cw[2] * x1) + pltpu.roll(cw[1] * xs + cw[0] * x1, 2, 0)
                acc = acc[CONV_HALO:, :]
                y = acc * _sigmoid(acc)
                if which < 2:
                    inv_norm = lax.rsqrt(jnp.sum(y * y, axis=-1, keepdims=True) + RMS_EPS)
                    y = y * (inv_norm * QK_SCALE if which == 0 else inv_norm)
                (gq_ref, gk_ref, gv_ref)[which][parts[n], h * HEAD_DIM:(h + 1) * HEAD_DIM] = (
                    y.astype(BF16))

    def plain_pair(ref, c_base, p):
        vals = [proj(n, c_base + p * pair_w, pair_w) for n in ns]
        for n in ns:
            ref[parts[n], p * pair_w:(p + 1) * pair_w] = vals[n].astype(BF16)

    def rotary_pair(ref, c_base, scale, p):
        half = ROPE_DIMS // 2
        pairs = [proj(n, c_base + p * pair_w, pair_w) for n in ns]
        for n in ns:
            cos, sa, sb = cos_ref[parts[n], :], sa_ref[parts[n], :], sb_ref[parts[n], :]
            for h in (2 * p, 2 * p + 1):
                xr = pairs[n][:, (h % 2) * HEAD_DIM:(h % 2 + 1) * HEAD_DIM]
                rot = (xr * cos + pltpu.roll(xr, LANES - half, 1) * sa
                       + pltpu.roll(xr, half, 1) * sb)
                if scale is not None:
                    rot = rot * scale
                ref[parts[n], h * HEAD_DIM:(h + 1) * HEAD_DIM] = rot.astype(BF16)

    def gates():
        lane = lax.broadcasted_iota(jnp.int32, (IN_PART_ROWS, LANES), 1)
        bas = [proj(n, C_BA, LANES) for n in ns]
        for n in ns:
            xg = bas[n] + gp_ref[1:2, :]
            e = jnp.exp(-jnp.abs(xg))
            r = 1.0 / (1.0 + e)
            sig = jnp.where(xg >= 0.0, r, e * r)
            g = -jnp.exp(gp_ref[0:1, :]) * (jnp.maximum(xg, 0.0) + jnp.log1p(e))
            bg_ref[parts[n], :] = jnp.where(lane < GDN_HEADS, sig,
                                            jnp.where(lane < 2 * GDN_HEADS, g, 0.0))

    convs = [functools.partial(conv_pair, p) for p in range(3 * GDN_WIDTH // pair_w)]
    light = ([functools.partial(plain_pair, gz_ref, C_GZ, p) for p in range(GDN_WIDTH // pair_w)]
             + [functools.partial(plain_pair, mv_ref, C_MV, p) for p in range(MOBA_WIDTH // pair_w)]
             + [functools.partial(rotary_pair, mq_ref, C_MQ, MOBA_Q_SCALE, p)
                for p in range(MOBA_WIDTH // pair_w)]
             + [functools.partial(rotary_pair, mk_ref, C_MK, None, p)
                for p in range(MOBA_WIDTH // pair_w)]
             + [gates])
    done = 0
    for n, conv in enumerate(convs, start=1):
        conv()
        upto = len(light) * n // len(convs)
        for stage in light[done:upto]:
            stage()
        done = upto


def _in_proj(x, nw, w_in, layer, conv_w, gparams, cos_t, sa_t, sb_t, *, tm):
    B, T, D = x.shape
    tok = lambda width: pl.BlockSpec((None, tm, width), lambda b, i: (b, i, 0))
    const = lambda shape: pl.BlockSpec(shape, lambda b, i: (0,) * len(shape))
    table = pl.BlockSpec((tm, LANES), lambda b, i: (i, 0))
    o512 = jax.ShapeDtypeStruct((B, T, GDN_WIDTH), BF16)
    return pl.pallas_call(
        functools.partial(_in_proj_body, tm=tm),
        grid=(B, T // tm),
        in_specs=[tok(D), const((1, D)),
                  pl.BlockSpec((None,) + w_in.shape[1:], lambda b, i: (layer, 0, 0),
                               pipeline_mode=pl.Buffered(1)),
                  const((GDN_CONV, 3 * GDN_WIDTH)),
                  const((2, LANES)), table, table, table],
        out_specs=[tok(GDN_WIDTH)] * 4 + [tok(LANES)] + [tok(MOBA_WIDTH)] * 3,
        out_shape=[o512] * 4 + [jax.ShapeDtypeStruct((B, T, LANES), F32)] + [o512] * 3,
        scratch_shapes=[pltpu.VMEM((CONV_HALO, 3 * GDN_WIDTH), F32),
                        pltpu.VMEM((D, C_END), BF16)],
        compiler_params=pltpu.CompilerParams(
            dimension_semantics=("arbitrary", "arbitrary"), vmem_limit_bytes=VMEM_LIMIT),
        name="in_proj",
    )(x, nw, w_in, conv_w, gparams, cos_t, sa_t, sb_t)


def _gdn_body(q_ref, k_ref, v_ref, z_ref, bg_ref, nw_ref, o_ref,
              s_ref, mneg_ref, n_ref, o1_ref, o2_ref, a_ref, snap_ref, *, T):
    C = GDN_CHUNK
    H = GDN_HEADS
    G = GDN_GROUP
    rows_per_trip = G * C
    ntrips = T // rows_per_trip
    ri = lax.broadcasted_iota(jnp.int32, (C, C), 0)
    ci = lax.broadcasted_iota(jnp.int32, (C, C), 1)
    causal = ci <= ri
    strict = ci < ri
    eye = (ri == ci).astype(F32)
    cols = [slice(h * HEAD_DIM, (h + 1) * HEAD_DIM) for h in range(H)]
    s_ref[...] = jnp.zeros_like(s_ref)

    def phase_a(row0, slot):
        inst = [(c, h) for c in range(G) for h in range(H)]
        rng = range(len(inst))
        r0s = [_aligned(row0 + c * C, C) for c in range(G)]
        bgs = [bg_ref[pl.ds(r0, C), :] for r0 in r0s]
        q = [q_ref[pl.ds(r0s[c], C), cols[h]].astype(F32) for c, h in inst]
        k = [k_ref[pl.ds(r0s[c], C), cols[h]].astype(F32) for c, h in inst]
        v = [v_ref[pl.ds(r0s[c], C), cols[h]].astype(F32) for c, h in inst]
        beta = [bgs[c][:, h:h + 1] for c, h in inst]
        gc_col, gc_last, decay = [], [], []
        for c, h in inst:
            gb = jnp.broadcast_to(bgs[c][:, H + h:H + h + 1], (C, C))
            row = jnp.sum(jnp.where(ri <= ci, gb, 0.0), axis=0, keepdims=True)
            g_row = jnp.sum(jnp.where(ri == ci, gb, 0.0), axis=0, keepdims=True)
            col = jnp.sum(jnp.where(causal, jnp.broadcast_to(g_row, (C, C)), 0.0),
                          axis=1, keepdims=True)
            gc_col.append(col)
            gc_last.append(row[:, C - 1:C])
            decay.append(jnp.exp(jnp.where(causal, col - row, -jnp.inf)))
        kb = [k[t] * beta[t] for t in rng]
        eg = [jnp.exp(gc_col[t]) for t in rng]
        kq = [_dot_nt(jnp.concatenate([kb[t], q[t]], axis=0), k[t]) for t in rng]
        yield
        low = [jnp.where(strict, kq[t][:C] * decay[t], 0.0) for t in rng]
        intra = [jnp.where(causal, kq[t][C:] * decay[t], 0.0) for t in rng]
        inv = [eye - low[t] for t in rng]
        pw = [_dot(low[t], low[t]) for t in rng]
        yield
        for _ in range(4):
            r = [_dot(jnp.concatenate([pw[t], inv[t]], axis=0), pw[t]) for t in rng]
            inv = [inv[t] + r[t][C:] for t in rng]
            pw = [r[t][:C] for t in rng]
            yield
        inv = [inv[t] + _dot(inv[t], pw[t]) for t in rng]
        yield
        wu = [_dot(inv[t], jnp.concatenate([kb[t] * eg[t], v[t] * beta[t]], axis=-1))
              for t in rng]
        yield
        kd = [k[t] * jnp.exp(gc_last[t] - gc_col[t]) for t in rng]
        r = [_dot(jnp.concatenate([kd[t].T, intra[t]], axis=0), wu[t]) for t in rng]
        mn = [r[t][:HEAD_DIM] for t in rng]
        io = [r[t][HEAD_DIM:] for t in rng]
        for t, (c, h) in enumerate(inst):
            rows = slice(c * C, (c + 1) * C)
            mneg_ref[slot, c, h] = (-mn[t][:, :HEAD_DIM]).astype(BF16)
            n_ref[slot, c, h] = mn[t][:, HEAD_DIM:]
            o1_ref[slot, rows, cols[h]] = (q[t] * eg[t] - io[t][:, :HEAD_DIM]).astype(BF16)
            o2_ref[slot, rows, cols[h]] = io[t][:, HEAD_DIM:]
            a_ref[slot, c, h] = jnp.broadcast_to(jnp.exp(gc_last[t]), (SUBLANES, HEAD_DIM))
        yield

    def phase_b_step(slot, c):
        hs = range(H)
        state = [s_ref[h] for h in hs]
        sb = [state[h].astype(BF16) for h in hs]
        ms = [jnp.dot(mneg_ref[slot, c, h], sb[h], preferred_element_type=F32) for h in hs]
        for h in hs:
            snap_ref[slot, c, h] = sb[h]
            s_ref[h] = state[h] * a_ref[slot, c, h, 0:1, :] + ms[h] + n_ref[slot, c, h]

    def phase_c(row0, slot, chunks):
        inst = [(c, h) for c in chunks for h in range(H)]
        os_ = [jnp.dot(o1_ref[slot, c * C:(c + 1) * C, cols[h]], snap_ref[slot, c, h],
                       preferred_element_type=F32) for c, h in inst]
        for t, (c, h) in enumerate(inst):
            r0 = _aligned(row0 + c * C, C)
            o = os_[t] + o2_ref[slot, c * C:(c + 1) * C, cols[h]]
            z = z_ref[pl.ds(r0, C), cols[h]].astype(F32)
            o_ref[pl.ds(r0, C), cols[h]] = (_rms(o, nw_ref[...])
                                            * (z * _sigmoid(z))).astype(o_ref.dtype)

    def trip(g, do_a, do_b, do_c):
        row0 = lambda d: _aligned((g - d) * rows_per_trip, rows_per_trip)
        slot = lambda d: (g - d) % GDN_RING
        c_steps = ([functools.partial(phase_c, row0(2), slot(2), (c,)) for c in range(G)]
                   if do_c else [])
        b_steps = [functools.partial(phase_b_step, slot(1), c) for c in range(G)] if do_b else []
        if do_a:
            for n, _ in enumerate(phase_a(row0(0), slot(0))):
                if n % 2 == 0 and c_steps:
                    c_steps.pop(0)()
                if n % 2 == 1 and b_steps:
                    b_steps.pop(0)()
        while b_steps or c_steps:
            for steps in (b_steps, c_steps):
                if steps:
                    steps.pop(0)()

    def steady(g, carry):
        trip(g, True, True, True)
        return carry

    all_on = lambda g: g < ntrips and 1 <= g <= ntrips and 2 <= g <= ntrips + 1
    g = 0
    while g < ntrips + 2:
        if all_on(g):
            last = g
            while all_on(last + 1):
                last += 1
            lax.fori_loop(g, last + 1, steady, 0)
            g = last + 1
        else:
            trip(g, g < ntrips, 1 <= g <= ntrips, 2 <= g <= ntrips + 1)
            g += 1


def _gdn(gq, gk, gv, gz, bg, nw):
    B, T, _ = gq.shape
    rows_per_trip = GDN_GROUP * GDN_CHUNK
    assert T % rows_per_trip == 0
    tok = lambda width: pl.BlockSpec((None, T, width), lambda b: (b, 0, 0))
    per_chunk = (GDN_RING, GDN_GROUP, GDN_HEADS)
    return pl.pallas_call(
        functools.partial(_gdn_body, T=T),
        grid=(B,),
        in_specs=[tok(GDN_WIDTH)] * 4 + [tok(LANES), pl.BlockSpec((1, HEAD_DIM), lambda b: (0, 0))],
        out_specs=tok(GDN_WIDTH),
        out_shape=jax.ShapeDtypeStruct((B, T, GDN_WIDTH), BF16),
        scratch_shapes=[
            pltpu.VMEM((GDN_HEADS, HEAD_DIM, HEAD_DIM), F32),
            pltpu.VMEM(per_chunk + (HEAD_DIM, HEAD_DIM), BF16),
            pltpu.VMEM(per_chunk + (HEAD_DIM, HEAD_DIM), F32),
            pltpu.VMEM((GDN_RING, rows_per_trip, GDN_WIDTH), BF16),
            pltpu.VMEM((GDN_RING, rows_per_trip, GDN_WIDTH), F32),
            pltpu.VMEM(per_chunk + (SUBLANES, HEAD_DIM), F32),
            pltpu.VMEM(per_chunk + (HEAD_DIM, HEAD_DIM), BF16),
        ],
        compiler_params=pltpu.CompilerParams(
            dimension_semantics=("arbitrary",), vmem_limit_bytes=VMEM_LIMIT),
        name="gdn",
    )(gq, gk, gv, gz, bg, nw)


def _moba_body(q_ref, k_ref, v_ref, o_ref, vt_ref, *, T):
    BS = MOBA_BLOCK
    nb = T // BS
    nrow = -(-nb // SUBLANES) * SUBLANES
    neg = -jnp.inf
    heads = [slice(h * HEAD_DIM, (h + 1) * HEAD_DIM) for h in range(MOBA_HEADS_PER_STEP)]
    blk = lax.broadcasted_iota(jnp.int32, (nrow, BS), 0)
    key = lax.broadcasted_iota(jnp.int32, (BS, BS), 0)
    qry = lax.broadcasted_iota(jnp.int32, (BS, BS), 1)

    kmean = []
    for n, hd in enumerate(heads):
        means = [jnp.mean(k_ref[j * BS:(j + 1) * BS, hd].astype(F32), axis=0, keepdims=True)
                 for j in range(nb)]
        if nrow > nb:
            means.append(jnp.zeros((nrow - nb, HEAD_DIM), F32))
        kmean.append(jnp.concatenate(means, axis=0))
        for j in range(nb):
            vt_ref[n, 0:HEAD_DIM, j * BS:(j + 1) * BS] = v_ref[j * BS:(j + 1) * BS, hd].T
        vt_ref[n, HEAD_DIM:, :] = jnp.ones((BF16_ROWS, T), BF16)

    def block_scores(n, i):
        q = q_ref[i * BS:(i + 1) * BS, heads[n]]
        sel = None
        if i > MOBA_TOPK:
            gate = lax.dot_general(kmean[n], q.astype(F32), (((1,), (1,)), ((), ())),
                                   precision=lax.Precision.HIGHEST, preferred_element_type=F32)
            gate = jnp.where(blk < i, gate, neg)
            rank = jnp.zeros((nrow, BS), F32)
            for jp in range(i):
                gj = gate[jp:jp + 1, :]
                beats = (gj > gate) | ((gj == gate) & (blk > jp))
                rank = rank + jnp.where(beats, 1.0, 0.0)
            sel = jnp.where((rank < MOBA_TOPK) & (blk < i), 1.0, 0.0)
        scores = []
        for j in range(i + 1):
            s = _dot_nt(k_ref[j * BS:(j + 1) * BS, heads[n]], q)
            if j == i:
                s = jnp.where(key <= qry, s, neg)
            elif sel is not None:
                s = jnp.where(sel[j:j + 1, :] > 0.5, s, neg)
            scores.append(s)
        return scores

    def block_output(n, i, scores):
        m = scores[0].max(axis=0, keepdims=True)
        for s in scores[1:]:
            m = jnp.maximum(m, s.max(axis=0, keepdims=True))
        acc = jnp.zeros((HEAD_DIM + BF16_ROWS, BS), F32)
        for j, s in enumerate(scores):
            acc = acc + _dot(vt_ref[n, :, j * BS:(j + 1) * BS], jnp.exp2(s - m))
        out = acc[:HEAD_DIM, :] / acc[HEAD_DIM:HEAD_DIM + 1, :]
        o_ref[i * BS:(i + 1) * BS, heads[n]] = out.T.astype(o_ref.dtype)

    hs = range(MOBA_HEADS_PER_STEP)
    scores = [block_scores(n, 0) for n in hs]
    for i in range(nb):
        following = [block_scores(n, i + 1) for n in hs] if i + 1 < nb else None
        for n in hs:
            block_output(n, i, scores[n])
        scores = following


def _moba(mq, mk, mv):
    B, T, _ = mq.shape
    assert T % MOBA_BLOCK == 0 and MOBA_HEADS % MOBA_HEADS_PER_STEP == 0
    width = MOBA_HEADS_PER_STEP * HEAD_DIM
    head = pl.BlockSpec((None, T, width), lambda b, h: (b, 0, h))
    return pl.pallas_call(
        functools.partial(_moba_body, T=T),
        grid=(B, MOBA_HEADS // MOBA_HEADS_PER_STEP),
        in_specs=[head] * 3,
        out_specs=head,
        out_shape=jax.ShapeDtypeStruct((B, T, MOBA_WIDTH), BF16),
        scratch_shapes=[pltpu.VMEM((MOBA_HEADS_PER_STEP, HEAD_DIM + BF16_ROWS, T), BF16)],
        compiler_params=pltpu.CompilerParams(
            dimension_semantics=("arbitrary", "arbitrary"), vmem_limit_bytes=VMEM_LIMIT),
        name="moba",
    )(mq, mk, mv)


def _mlp_body(og_ref, om_ref, x_ref, p_ref, wo_ref, n1_ref, n2_ref, wu_ref, wd_ref, n3_ref,
              wg_ref, wp_ref, o_ref, *, ff_chunk):
    parts = [slice(r0, r0 + MLP_PART_ROWS) for r0 in range(0, x_ref.shape[0], MLP_PART_ROWS)]
    ns = range(len(parts))
    mix = [_dot(og_ref[r, :], wo_ref[0:GDN_WIDTH, :])
           + _dot(om_ref[r, :], wo_ref[GDN_WIDTH:GDN_WIDTH + MOBA_WIDTH, :]) for r in parts]
    h = [x_ref[r, :] + _rms(mix[n], n1_ref[...]) for n, r in enumerate(parts)]
    a = [_rms(h[n], n2_ref[...]).astype(BF16) for n in ns]
    f = [None for _ in ns]
    for c0 in range(0, D_FF, ff_chunk):
        up = [jnp.dot(a[n], wu_ref[:, c0:c0 + ff_chunk], preferred_element_type=F32)
              for n in ns]
        for n in ns:
            part = _dot(jnp.square(jnp.maximum(up[n], 0.0)), wd_ref[c0:c0 + ff_chunk, :])
            f[n] = part if f[n] is None else f[n] + part
    h = [h[n] + _rms(f[n], n3_ref[...]) for n in ns]
    gate = [_dot(h[n], wg_ref[...]) for n in ns]
    ple = [_dot(p_ref[r, :], wp_ref[...]) for r in parts]
    for n, r in enumerate(parts):
        o_ref[r, :] = h[n] + _sigmoid(gate[n]) * ple[n]


def _mlp(og, om, x, p, wo, n1, n2, wu, wd, n3, wg, wp, *, tm):
    B, T, D = x.shape
    tok = lambda width: pl.BlockSpec((None, tm, width), lambda b, i: (b, i, 0))
    const = lambda shape: pl.BlockSpec(shape, lambda b, i: (0,) * len(shape),
                                       pipeline_mode=pl.Buffered(1))
    return pl.pallas_call(
        functools.partial(_mlp_body, ff_chunk=1024),
        grid=(B, T // tm),
        in_specs=[tok(GDN_WIDTH), tok(MOBA_WIDTH), tok(D), tok(PLE_DIM),
                  const((D, D)), const((1, D)), const((1, D)), const((D, D_FF)),
                  const((D_FF, D)), const((1, D)), const((D, D)), const((PLE_DIM, D))],
        out_specs=tok(D),
        out_shape=jax.ShapeDtypeStruct((B, T, D), F32),
        compiler_params=pltpu.CompilerParams(
            dimension_semantics=("arbitrary", "arbitrary"), vmem_limit_bytes=VMEM_LIMIT),
        name="mlp",
    )(og, om, x, p, wo, n1, n2, wu, wd, n3, wg, wp)


def _rope_tables(T):
    half = ROPE_DIMS // 2
    inv_freq = ROPE_THETA ** (-np.arange(half, dtype=np.float64) * (2.0 / ROPE_DIMS))
    ang = np.arange(T, dtype=np.float64)[:, None] * inv_freq[None, :]
    cos, sin = np.cos(ang), np.sin(ang)
    zeros = np.zeros((T, HEAD_DIM - ROPE_DIMS))
    z_half = np.zeros((T, half))
    cos_t = np.concatenate([cos, cos, np.ones((T, HEAD_DIM - ROPE_DIMS))], axis=-1)
    sa_t = np.concatenate([-sin, z_half, zeros], axis=-1)
    sb_t = np.concatenate([z_half, sin, zeros], axis=-1)
    cos_t, sa_t, sb_t = (jnp.asarray(t, dtype=F32) for t in (cos_t, sa_t, sb_t))
    return cos_t, sa_t, sb_t


def _layer(h, p_i, w_in, layer, conv_w, a_log, dt_bias, gdn_norm_w, w_out, attn_pre_norm,
           attn_post_norm, mlp_pre_norm, mlp_post_norm, w_up, w_down, w_ple, w_ple_gate, tables):
    B, T, D = h.shape
    tm = next(t for t in (512, MOBA_BLOCK) if T % t == 0)
    assert T % tm == 0 and T % MOBA_BLOCK == 0 and T % GDN_CHUNK == 0
    nh = 2 * GDN_HEADS
    gparams = jnp.zeros((2, LANES), F32)
    gparams = gparams.at[0, GDN_HEADS:nh].set(a_log.astype(F32))
    gparams = gparams.at[1, GDN_HEADS:nh].set(dt_bias.astype(F32))
    row = lambda v: v.reshape(1, -1).astype(F32)

    gq, gk, gv, gz, bg, mq, mk, mv = _in_proj(
        h, row(attn_pre_norm), jnp.swapaxes(w_in.astype(F32), 1, 2), layer, conv_w.astype(F32),
        gparams, *tables,
        tm=tm)
    o_gdn = _gdn(gq, gk, gv, gz, bg, row(gdn_norm_w))
    o_moba = _moba(mq, mk, mv)
    return _mlp(o_gdn, o_moba, h, p_i, w_out.astype(BF16), row(attn_post_norm),
                row(mlp_pre_norm), w_up.astype(BF16), w_down.astype(BF16),
                row(mlp_post_norm), w_ple_gate.astype(BF16), w_ple.astype(BF16), tm=tm)


def kernel(x, p, w_in, conv_w, a_log, dt_bias, gdn_norm_w, w_out, attn_pre_norm, attn_post_norm,
           mlp_pre_norm, mlp_post_norm, w_up, w_down, w_ple, w_ple_gate):
    tables = _rope_tables(x.shape[1])
    h = x
    for i in range(w_in.shape[0]):
        h = _layer(h, p[i], w_in, i, conv_w[i], a_log[i], dt_bias[i], gdn_norm_w[i], w_out[i],
                   attn_pre_norm[i], attn_post_norm[i], mlp_pre_norm[i], mlp_post_norm[i],
                   w_up[i], w_down[i], w_ple[i], w_ple_gate[i], tables)
    return h
```

```python
import functools
import math

import jax
import jax.numpy as jnp
import numpy as np
from jax import lax
from jax.experimental import pallas as pl
from jax.experimental.pallas import tpu as pltpu

F32 = jnp.float32
BF16 = jnp.bfloat16

D_MODEL = 1024
HEAD_DIM = 128
GDN_HEADS = 4
GDN_WIDTH = GDN_HEADS * HEAD_DIM
GDN_CONV = 4
GDN_CHUNK = 64
GDN_GROUP = 4
GDN_RING = 3
MOBA_HEADS = 4
MOBA_WIDTH = MOBA_HEADS * HEAD_DIM
MOBA_BLOCK = 256
MOBA_TOPK = 3
MOBA_HEADS_PER_STEP = 2
MOBA_LOOKAHEAD = 2
ROPE_DIMS = 32
ROPE_THETA = 500000.0
D_FF = 4 * D_MODEL
PLE_DIM = 256
RMS_EPS = 1e-6
LANES = 128
SUBLANES = 8
BF16_ROWS = 16
CONV_HALO = 8
REPACK_ROWS = 128
MLP_PART_ROWS = 256
IN_PART_ROWS = 256
QK_SCALE = HEAD_DIM ** -0.5
MOBA_Q_SCALE = QK_SCALE * math.log2(math.e)
VMEM_LIMIT = 56 * 1024 * 1024

C_GQKV = 0
C_GZ = 3 * GDN_WIDTH
C_MQ = 4 * GDN_WIDTH
C_MK = C_MQ + MOBA_WIDTH
C_MV = C_MK + MOBA_WIDTH
C_BA = C_MV + MOBA_WIDTH
C_END = C_BA + LANES


def _rms(x, w):
    return x * lax.rsqrt(jnp.mean(x * x, axis=-1, keepdims=True) + RMS_EPS) * w


def _sigmoid(x):
    return 1.0 / (1.0 + jnp.exp(-x))


def _aligned(x, m):
    return x if isinstance(x, int) else pl.multiple_of(x, m)


def _dot(a, b):
    return jnp.dot(a.astype(BF16), b.astype(BF16), preferred_element_type=F32)


def _dot_nt(a, b):
    return lax.dot_general(a.astype(BF16), b.astype(BF16), (((1,), (1,)), ((), ())),
                           preferred_element_type=F32)


def _in_proj_body(x_ref, nw_ref, win_ref, cw_ref, gp_ref, cos_ref, sa_ref, sb_ref,
                  gq_ref, gk_ref, gv_ref, gz_ref, bg_ref, mq_ref, mk_ref, mv_ref,
                  halo, w_ref, *, tm):
    i = pl.program_id(1)

    @pl.when((pl.program_id(0) == 0) & (i == 0))
    def _():
        n_gate = 2 * GDN_HEADS
        for c0 in range(0, C_BA, REPACK_ROWS):
            src = c0 if c0 < C_MQ else c0 + n_gate
            w_ref[:, c0:c0 + REPACK_ROWS] = win_ref[src:src + REPACK_ROWS, :].T.astype(BF16)
        gate_rows = jnp.concatenate([win_ref[C_MQ:C_MQ + n_gate, :],
                                     jnp.zeros((LANES - n_gate, D_MODEL), F32)], axis=0)
        w_ref[:, C_BA:C_END] = gate_rows.T.astype(BF16)

    parts = [slice(r0, r0 + IN_PART_ROWS) for r0 in range(0, tm, IN_PART_ROWS)]
    ns = range(len(parts))
    u = [None for _ in ns]

    def proj(n, c0, width):
        if u[n] is None:
            u[n] = _rms(x_ref[parts[n], :], nw_ref[...]).astype(BF16)
        return jnp.dot(u[n], w_ref[:, c0:c0 + width], preferred_element_type=F32)

    @pl.when(i == 0)
    def _():
        halo[...] = jnp.zeros_like(halo)

    pair_w = 2 * HEAD_DIM

    def conv_pair(p):
        pairs = [proj(n, C_GQKV + p * pair_w, pair_w) for n in ns]
        for s in (2 * p, 2 * p + 1):
            c0 = s * HEAD_DIM
            cw = [cw_ref[j:j + 1, c0:c0 + HEAD_DIM] for j in range(GDN_CONV)]
            which, h = divmod(s, GDN_HEADS)
            cur = [pairs[n][:, (s % 2) * HEAD_DIM:(s % 2 + 1) * HEAD_DIM] for n in ns]
            prev = halo[:, c0:c0 + HEAD_DIM]
            halo[:, c0:c0 + HEAD_DIM] = cur[-1][IN_PART_ROWS - CONV_HALO:, :]
            for n in ns:
                xs = jnp.concatenate([prev, cur[n]], axis=0)
                prev = cur[n][IN_PART_ROWS - CONV_HALO:, :]
                x1 = pltpu.roll(xs, 1, 0)
                acc = (cw[3] * xs + cw[2] * x1) + pltpu.roll(cw[1] * xs + cw[0] * x1, 2, 0)
                acc = acc[CONV_HALO:, :]
                y = acc * _sigmoid(acc)
                if which < 2:
                    inv_norm = lax.rsqrt(jnp.sum(y * y, axis=-1, keepdims=True) + RMS_EPS)
                    y = y * (inv_norm * QK_SCALE if which == 0 else inv_norm)
                (gq_ref, gk_ref, gv_ref)[which][parts[n], h * HEAD_DIM:(h + 1) * HEAD_DIM] = (
                    y.astype(BF16))

    def plain_pair(ref, c_base, p):
        vals = [proj(n, c_base + p * pair_w, pair_w) for n in ns]
        for n in ns:
            ref[parts[n], p * pair_w:(p + 1) * pair_w] = vals[n].astype(BF16)

    def rotary_pair(ref, c_base, scale, p):
        half = ROPE_DIMS // 2
        pairs = [proj(n, c_base + p * pair_w, pair_w) for n in ns]
        for n in ns:
            cos, sa, sb = cos_ref[parts[n], :], sa_ref[parts[n], :], sb_ref[parts[n], :]
            for h in (2 * p, 2 * p + 1):
                xr = pairs[n][:, (h % 2) * HEAD_DIM:(h % 2 + 1) * HEAD_DIM]
                rot = (xr * cos + pltpu.roll(xr, LANES - half, 1) * sa
                       + pltpu.roll(xr, half, 1) * sb)
                if scale is not None:
                    rot = rot * scale
                ref[parts[n], h * HEAD_DIM:(h + 1) * HEAD_DIM] = rot.astype(BF16)

    def gates():
        lane = lax.broadcasted_iota(jnp.int32, (IN_PART_ROWS, LANES), 1)
        bas = [proj(n, C_BA, LANES) for n in ns]
        for n in ns:
            xg = bas[n] + gp_ref[1:2, :]
            e = jnp.exp(-jnp.abs(xg))
            r = 1.0 / (1.0 + e)
            sig = jnp.where(xg >= 0.0, r, e * r)
            g = -jnp.exp(gp_ref[0:1, :]) * (jnp.maximum(xg, 0.0) + jnp.log1p(e))
            bg_ref[parts[n], :] = jnp.where(lane < GDN_HEADS, sig,
                                            jnp.where(lane < 2 * GDN_HEADS, g, 0.0))

    convs = [functools.partial(conv_pair, p) for p in range(3 * GDN_WIDTH // pair_w)]
    light = ([functools.partial(plain_pair, gz_ref, C_GZ, p) for p in range(GDN_WIDTH // pair_w)]
             + [functools.partial(plain_pair, mv_ref, C_MV, p) for p in range(MOBA_WIDTH // pair_w)]
             + [functools.partial(rotary_pair, mq_ref, C_MQ, MOBA_Q_SCALE, p)
                for p in range(MOBA_WIDTH // pair_w)]
             + [functools.partial(rotary_pair, mk_ref, C_MK, None, p)
                for p in range(MOBA_WIDTH // pair_w)]
             + [gates])
    done = 0
    for n, conv in enumerate(convs, start=1):
        conv()
        upto = len(light) * n // len(convs)
        for stage in light[done:upto]:
            stage()
        done = upto


def _in_proj(x, nw, w_in, layer, conv_w, gparams, cos_t, sa_t, sb_t, *, tm):
    B, T, D = x.shape
    tok = lambda width: pl.BlockSpec((None, tm, width), lambda b, i: (b, i, 0))
    const = lambda shape: pl.BlockSpec(shape, lambda b, i: (0,) * len(shape))
    table = pl.BlockSpec((tm, LANES), lambda b, i: (i, 0))
    o512 = jax.ShapeDtypeStruct((B, T, GDN_WIDTH), BF16)
    return pl.pallas_call(
        functools.partial(_in_proj_body, tm=tm),
        grid=(B, T // tm),
        in_specs=[tok(D), const((1, D)),
                  pl.BlockSpec((None,) + w_in.shape[1:], lambda b, i: (layer, 0, 0),
                               pipeline_mode=pl.Buffered(1)),
                  const((GDN_CONV, 3 * GDN_WIDTH)),
                  const((2, LANES)), table, table, table],
        out_specs=[tok(GDN_WIDTH)] * 4 + [tok(LANES)] + [tok(MOBA_WIDTH)] * 3,
        out_shape=[o512] * 4 + [jax.ShapeDtypeStruct((B, T, LANES), F32)] + [o512] * 3,
        scratch_shapes=[pltpu.VMEM((CONV_HALO, 3 * GDN_WIDTH), F32),
                        pltpu.VMEM((D, C_END), BF16)],
        compiler_params=pltpu.CompilerParams(
            dimension_semantics=("arbitrary", "arbitrary"), vmem_limit_bytes=VMEM_LIMIT),
        name="in_proj",
    )(x, nw, w_in, conv_w, gparams, cos_t, sa_t, sb_t)


def _gdn_body(q_ref, k_ref, v_ref, z_ref, bg_ref, nw_ref, o_ref,
              s_ref, mneg_ref, n_ref, o1_ref, o2_ref, a_ref, snap_ref, *, T):
    C = GDN_CHUNK
    H = GDN_HEADS
    G = GDN_GROUP
    rows_per_trip = G * C
    ntrips = T // rows_per_trip
    ri = lax.broadcasted_iota(jnp.int32, (C, C), 0)
    ci = lax.broadcasted_iota(jnp.int32, (C, C), 1)
    causal = ci <= ri
    strict = ci < ri
    eye = (ri == ci).astype(F32)
    cols = [slice(h * HEAD_DIM, (h + 1) * HEAD_DIM) for h in range(H)]
    s_ref[...] = jnp.zeros_like(s_ref)

    def phase_a(row0, slot):
        inst = [(c, h) for c in range(G) for h in range(H)]
        rng = range(len(inst))
        r0s = [_aligned(row0 + c * C, C) for c in range(G)]
        bgs = [bg_ref[pl.ds(r0, C), :] for r0 in r0s]
        q = [q_ref[pl.ds(r0s[c], C), cols[h]].astype(F32) for c, h in inst]
        k = [k_ref[pl.ds(r0s[c], C), cols[h]].astype(F32) for c, h in inst]
        v = [v_ref[pl.ds(r0s[c], C), cols[h]].astype(F32) for c, h in inst]
        beta = [bgs[c][:, h:h + 1] for c, h in inst]
        gc_col, gc_last, decay = [], [], []
        for c, h in inst:
            gb = jnp.broadcast_to(bgs[c][:, H + h:H + h + 1], (C, C))
            row = jnp.sum(jnp.where(ri <= ci, gb, 0.0), axis=0, keepdims=True)
            g_row = jnp.sum(jnp.where(ri == ci, gb, 0.0), axis=0, keepdims=True)
            col = jnp.sum(jnp.where(causal, jnp.broadcast_to(g_row, (C, C)), 0.0),
                          axis=1, keepdims=True)
            gc_col.append(col)
            gc_last.append(row[:, C - 1:C])
            decay.append(jnp.exp(jnp.where(causal, col - row, -jnp.inf)))
        kb = [k[t] * beta[t] for t in rng]
        eg = [jnp.exp(gc_col[t]) for t in rng]
        kq = [_dot_nt(jnp.concatenate([kb[t], q[t]], axis=0), k[t]) for t in rng]
        yield
        low = [jnp.where(strict, kq[t][:C] * decay[t], 0.0) for t in rng]
        intra = [jnp.where(causal, kq[t][C:] * decay[t], 0.0) for t in rng]
        inv = [eye - low[t] for t in rng]
        pw = [_dot(low[t], low[t]) for t in rng]
        yield
        for _ in range(4):
            r = [_dot(jnp.concatenate([pw[t], inv[t]], axis=0), pw[t]) for t in rng]
            inv = [inv[t] + r[t][C:] for t in rng]
            pw = [r[t][:C] for t in rng]
            yield
        inv = [inv[t] + _dot(inv[t], pw[t]) for t in rng]
        yield
        wu = [_dot(inv[t], jnp.concatenate([kb[t] * eg[t], v[t] * beta[t]], axis=-1))
              for t in rng]
        yield
        kd = [k[t] * jnp.exp(gc_last[t] - gc_col[t]) for t in rng]
        r = [_dot(jnp.concatenate([kd[t].T, intra[t]], axis=0), wu[t]) for t in rng]
        mn = [r[t][:HEAD_DIM] for t in rng]
        io = [r[t][HEAD_DIM:] for t in rng]
        for t, (c, h) in enumerate(inst):
            rows = slice(c * C, (c + 1) * C)
            mneg_ref[slot, c, h] = (-mn[t][:, :HEAD_DIM]).astype(BF16)
            n_ref[slot, c, h] = mn[t][:, HEAD_DIM:]
            o1_ref[slot, rows, cols[h]] = (q[t] * eg[t] - io[t][:, :HEAD_DIM]).astype(BF16)
            o2_ref[slot, rows, cols[h]] = io[t][:, HEAD_DIM:]
            a_ref[slot, c, h] = jnp.broadcast_to(jnp.exp(gc_last[t]), (SUBLANES, HEAD_DIM))
        yield

    def phase_b_step(slot, c):
        hs = range(H)
        state = [s_ref[h] for h in hs]
        sb = [state[h].astype(BF16) for h in hs]
        ms = [jnp.dot(mneg_ref[slot, c, h], sb[h], preferred_element_type=F32) for h in hs]
        for h in hs:
            snap_ref[slot, c, h] = sb[h]
            s_ref[h] = state[h] * a_ref[slot, c, h, 0:1, :] + ms[h] + n_ref[slot, c, h]

    def phase_c(row0, slot, chunks):
        inst = [(c, h) for c in chunks for h in range(H)]
        os_ = [jnp.dot(o1_ref[slot, c * C:(c + 1) * C, cols[h]], snap_ref[slot, c, h],
                       preferred_element_type=F32) for c, h in inst]
        for t, (c, h) in enumerate(inst):
            r0 = _aligned(row0 + c * C, C)
            o = os_[t] + o2_ref[slot, c * C:(c + 1) * C, cols[h]]
            z = z_ref[pl.ds(r0, C), cols[h]].astype(F32)
            o_ref[pl.ds(r0, C), cols[h]] = (_rms(o, nw_ref[...])
                                            * (z * _sigmoid(z))).astype(o_ref.dtype)

    def trip(g, do_a, do_b, do_c):
        row0 = lambda d: _aligned((g - d) * rows_per_trip, rows_per_trip)
        slot = lambda d: (g - d) % GDN_RING
        c_steps = ([functools.partial(phase_c, row0(2), slot(2), (c,)) for c in range(G)]
                   if do_c else [])
        b_steps = [functools.partial(phase_b_step, slot(1), c) for c in range(G)] if do_b else []
        if do_a:
            for n, _ in enumerate(phase_a(row0(0), slot(0))):
                if n % 2 == 0 and c_steps:
                    c_steps.pop(0)()
                if n % 2 == 1 and b_steps:
                    b_steps.pop(0)()
        while b_steps or c_steps:
            for steps in (b_steps, c_steps):
                if steps:
                    steps.pop(0)()

    def steady(g, carry):
        trip(g, True, True, True)
        return carry

    all_on = lambda g: g < ntrips and 1 <= g <= ntrips and 2 <= g <= ntrips + 1
    g = 0
    while g < ntrips + 2:
        if all_on(g):
            last = g
            while all_on(last + 1):
                last += 1
            lax.fori_loop(g, last + 1, steady, 0)
            g = last + 1
        else:
            trip(g, g < ntrips, 1 <= g <= ntrips, 2 <= g <= ntrips + 1)
            g += 1


def _gdn(gq, gk, gv, gz, bg, nw):
    B, T, _ = gq.shape
    rows_per_trip = GDN_GROUP * GDN_CHUNK
    assert T % rows_per_trip == 0
    tok = lambda width: pl.BlockSpec((None, T, width), lambda b: (b, 0, 0))
    per_chunk = (GDN_RING, GDN_GROUP, GDN_HEADS)
    return pl.pallas_call(
        functools.partial(_gdn_body, T=T),
        grid=(B,),
        in_specs=[tok(GDN_WIDTH)] * 4 + [tok(LANES), pl.BlockSpec((1, HEAD_DIM), lambda b: (0, 0))],
        out_specs=tok(GDN_WIDTH),
        out_shape=jax.ShapeDtypeStruct((B, T, GDN_WIDTH), BF16),
        scratch_shapes=[
            pltpu.VMEM((GDN_HEADS, HEAD_DIM, HEAD_DIM), F32),
            pltpu.VMEM(per_chunk + (HEAD_DIM, HEAD_DIM), BF16),
            pltpu.VMEM(per_chunk + (HEAD_DIM, HEAD_DIM), F32),
            pltpu.VMEM((GDN_RING, rows_per_trip, GDN_WIDTH), BF16),
            pltpu.VMEM((GDN_RING, rows_per_trip, GDN_WIDTH), F32),
            pltpu.VMEM(per_chunk + (SUBLANES, HEAD_DIM), F32),
            pltpu.VMEM(per_chunk + (HEAD_DIM, HEAD_DIM), BF16),
        ],
        compiler_params=pltpu.CompilerParams(
            dimension_semantics=("arbitrary",), vmem_limit_bytes=VMEM_LIMIT),
        name="gdn",
    )(gq, gk, gv, gz, bg, nw)


def _moba_body(q_ref, k_ref, v_ref, o_ref, vt_ref, *, T):
    BS = MOBA_BLOCK
    nb = T // BS
    nrow = -(-nb // SUBLANES) * SUBLANES
    neg = -jnp.inf
    heads = [slice(h * HEAD_DIM, (h + 1) * HEAD_DIM) for h in range(MOBA_HEADS_PER_STEP)]
    blk = lax.broadcasted_iota(jnp.int32, (nrow, BS), 0)
    key = lax.broadcasted_iota(jnp.int32, (BS, BS), 0)
    qry = lax.broadcasted_iota(jnp.int32, (BS, BS), 1)

    kmean = []
    for n, hd in enumerate(heads):
        means = [jnp.mean(k_ref[j * BS:(j + 1) * BS, hd].astype(F32), axis=0, keepdims=True)
                 for j in range(nb)]
        if nrow > nb:
            means.append(jnp.zeros((nrow - nb, HEAD_DIM), F32))
        kmean.append(jnp.concatenate(means, axis=0))
        for j in range(nb):
            vt_ref[n, 0:HEAD_DIM, j * BS:(j + 1) * BS] = v_ref[j * BS:(j + 1) * BS, hd].T
        vt_ref[n, HEAD_DIM:, :] = jnp.ones((BF16_ROWS, T), BF16)

    def block_scores(n, i):
        q = q_ref[i * BS:(i + 1) * BS, heads[n]]
        sel = None
        if i > MOBA_TOPK:
            gate = lax.dot_general(kmean[n], q.astype(F32), (((1,), (1,)), ((), ())),
                                   precision=lax.Precision.HIGHEST, preferred_element_type=F32)
            gate = jnp.where(blk < i, gate, neg)
            rank = jnp.zeros((nrow, BS), F32)
            for jp in range(i):
                gj = gate[jp:jp + 1, :]
                beats = (gj > gate) | ((gj == gate) & (blk > jp))
                rank = rank + jnp.where(beats, 1.0, 0.0)
            sel = jnp.where((rank < MOBA_TOPK) & (blk < i), 1.0, 0.0)
        scores = []
        for j in range(i + 1):
            s = _dot_nt(k_ref[j * BS:(j + 1) * BS, heads[n]], q)
            if j == i:
                s = jnp.where(key <= qry, s, neg)
            elif sel is not None:
                s = jnp.where(sel[j:j + 1, :] > 0.5, s, neg)
            scores.append(s)
        return scores

    def block_output(n, i, scores):
        m = scores[0].max(axis=0, keepdims=True)
        for s in scores[1:]:
            m = jnp.maximum(m, s.max(axis=0, keepdims=True))
        acc = jnp.zeros((HEAD_DIM + BF16_ROWS, BS), F32)
        for j, s in enumerate(scores):
            acc = acc + _dot(vt_ref[n, :, j * BS:(j + 1) * BS], jnp.exp2(s - m))
        out = acc[:HEAD_DIM, :] / acc[HEAD_DIM:HEAD_DIM + 1, :]
        o_ref[i * BS:(i + 1) * BS, heads[n]] = out.T.astype(o_ref.dtype)

    tasks = [(n, i) for i in range(nb) for n in range(MOBA_HEADS_PER_STEP)]
    pending = {t: block_scores(*tasks[t]) for t in range(min(MOBA_LOOKAHEAD, len(tasks)))}
    for t, (n, i) in enumerate(tasks):
        if t + MOBA_LOOKAHEAD < len(tasks):
            pending[t + MOBA_LOOKAHEAD] = block_scores(*tasks[t + MOBA_LOOKAHEAD])
        block_output(n, i, pending.pop(t))


def _moba(mq, mk, mv):
    B, T, _ = mq.shape
    assert T % MOBA_BLOCK == 0 and MOBA_HEADS % MOBA_HEADS_PER_STEP == 0
    width = MOBA_HEADS_PER_STEP * HEAD_DIM
    head = pl.BlockSpec((None, T, width), lambda b, h: (b, 0, h))
    return pl.pallas_call(
        functools.partial(_moba_body, T=T),
        grid=(B, MOBA_HEADS // MOBA_HEADS_PER_STEP),
        in_specs=[head] * 3,
        out_specs=head,
        out_shape=jax.ShapeDtypeStruct((B, T, MOBA_WIDTH), BF16),
        scratch_shapes=[pltpu.VMEM((MOBA_HEADS_PER_STEP, HEAD_DIM + BF16_ROWS, T), BF16)],
        compiler_params=pltpu.CompilerParams(
            dimension_semantics=("arbitrary", "arbitrary"), vmem_limit_bytes=VMEM_LIMIT),
        name="moba",
    )(mq, mk, mv)


def _mlp_body(og_ref, om_ref, x_ref, p_ref, wo_ref, n1_ref, n2_ref, wu_ref, wd_ref, n3_ref,
              wg_ref, wp_ref, o_ref, *, ff_chunk):
    parts = [slice(r0, r0 + MLP_PART_ROWS) for r0 in range(0, x_ref.shape[0], MLP_PART_ROWS)]
    ns = range(len(parts))
    mix = [_dot(og_ref[r, :], wo_ref[0:GDN_WIDTH, :])
           + _dot(om_ref[r, :], wo_ref[GDN_WIDTH:GDN_WIDTH + MOBA_WIDTH, :]) for r in parts]
    h = [x_ref[r, :] + _rms(mix[n], n1_ref[...]) for n, r in enumerate(parts)]
    a = [_rms(h[n], n2_ref[...]).astype(BF16) for n in ns]
    f = [None for _ in ns]
    for c0 in range(0, D_FF, ff_chunk):
        up = [jnp.dot(a[n], wu_ref[:, c0:c0 + ff_chunk], preferred_element_type=F32)
              for n in ns]
        for n in ns:
            part = _dot(jnp.square(jnp.maximum(up[n], 0.0)), wd_ref[c0:c0 + ff_chunk, :])
            f[n] = part if f[n] is None else f[n] + part
    h = [h[n] + _rms(f[n], n3_ref[...]) for n in ns]
    gate = [_dot(h[n], wg_ref[...]) for n in ns]
    ple = [_dot(p_ref[r, :], wp_ref[...]) for r in parts]
    for n, r in enumerate(parts):
        o_ref[r, :] = h[n] + _sigmoid(gate[n]) * ple[n]


def _mlp(og, om, x, p, wo, n1, n2, wu, wd, n3, wg, wp, *, tm):
    B, T, D = x.shape
    tok = lambda width: pl.BlockSpec((None, tm, width), lambda b, i: (b, i, 0))
    const = lambda shape: pl.BlockSpec(shape, lambda b, i: (0,) * len(shape),
                                       pipeline_mode=pl.Buffered(1))
    return pl.pallas_call(
        functools.partial(_mlp_body, ff_chunk=1024),
        grid=(B, T // tm),
        in_specs=[tok(GDN_WIDTH), tok(MOBA_WIDTH), tok(D), tok(PLE_DIM),
                  const((D, D)), const((1, D)), const((1, D)), const((D, D_FF)),
                  const((D_FF, D)), const((1, D)), const((D, D)), const((PLE_DIM, D))],
        out_specs=tok(D),
        out_shape=jax.ShapeDtypeStruct((B, T, D), F32),
        compiler_params=pltpu.CompilerParams(
            dimension_semantics=("arbitrary", "arbitrary"), vmem_limit_bytes=VMEM_LIMIT),
        name="mlp",
    )(og, om, x, p, wo, n1, n2, wu, wd, n3, wg, wp)


def _rope_tables(T):
    half = ROPE_DIMS // 2
    inv_freq = ROPE_THETA ** (-np.arange(half, dtype=np.float64) * (2.0 / ROPE_DIMS))
    ang = np.arange(T, dtype=np.float64)[:, None] * inv_freq[None, :]
    cos, sin = np.cos(ang), np.sin(ang)
    zeros = np.zeros((T, HEAD_DIM - ROPE_DIMS))
    z_half = np.zeros((T, half))
    cos_t = np.concatenate([cos, cos, np.ones((T, HEAD_DIM - ROPE_DIMS))], axis=-1)
    sa_t = np.concatenate([-sin, z_half, zeros], axis=-1)
    sb_t = np.concatenate([z_half, sin, zeros], axis=-1)
    cos_t, sa_t, sb_t = (jnp.asarray(t, dtype=F32) for t in (cos_t, sa_t, sb_t))
    return cos_t, sa_t, sb_t


def _layer(h, p_i, w_in, layer, conv_w, a_log, dt_bias, gdn_norm_w, w_out, attn_pre_norm,
           attn_post_norm, mlp_pre_norm, mlp_post_norm, w_up, w_down, w_ple, w_ple_gate, tables):
    B, T, D = h.shape
    tm = next(t for t in (512, MOBA_BLOCK) if T % t == 0)
    assert T % tm == 0 and T % MOBA_BLOCK == 0 and T % GDN_CHUNK == 0
    nh = 2 * GDN_HEADS
    gparams = jnp.zeros((2, LANES), F32)
    gparams = gparams.at[0, GDN_HEADS:nh].set(a_log.astype(F32))
    gparams = gparams.at[1, GDN_HEADS:nh].set(dt_bias.astype(F32))
    row = lambda v: v.reshape(1, -1).astype(F32)

    gq, gk, gv, gz, bg, mq, mk, mv = _in_proj(
        h, row(attn_pre_norm), jnp.swapaxes(w_in.astype(F32), 1, 2), layer, conv_w.astype(F32),
        gparams, *tables,
        tm=tm)
    o_gdn = _gdn(gq, gk, gv, gz, bg, row(gdn_norm_w))
    o_moba = _moba(mq, mk, mv)
    return _mlp(o_gdn, o_moba, h, p_i, w_out.astype(BF16), row(attn_post_norm),
                row(mlp_pre_norm), w_up.astype(BF16), w_down.astype(BF16),
                row(mlp_post_norm), w_ple_gate.astype(BF16), w_ple.astype(BF16), tm=tm)


def kernel(x, p, w_in, conv_w, a_log, dt_bias, gdn_norm_w, w_out, attn_pre_norm, attn_post_norm,
           mlp_pre_norm, mlp_post_norm, w_up, w_down, w_ple, w_ple_gate):
    tables = _rope_tables(x.shape[1])
    h = x
    for i in range(w_in.shape[0]):
        h = _layer(h, p[i], w_in, i, conv_w[i], a_log[i], dt_bias[i], gdn_norm_w[i], w_out[i],
                   attn_pre_norm[i], attn_post_norm[i], mlp_pre_norm[i], mlp_post_norm[i],
                   w_up[i], w_down[i], w_ple[i], w_ple_gate[i], tables)
    return h
```

```python
import functools
import math

import jax
import jax.numpy as jnp
import numpy as np
from jax import lax
from jax.experimental import pallas as pl
from jax.experimental.pallas import tpu as pltpu

F32 = jnp.float32
BF16 = jnp.bfloat16

D_MODEL = 1024
HEAD_DIM = 128
GDN_HEADS = 4
GDN_WIDTH = GDN_HEADS * HEAD_DIM
GDN_CONV = 4
GDN_CHUNK = 64
GDN_GROUP = 4
GDN_WAVES = 2
GDN_WAVE_LAG = 1
GDN_RING = 3
MOBA_HEADS = 4
MOBA_WIDTH = MOBA_HEADS * HEAD_DIM
MOBA_BLOCK = 256
MOBA_TOPK = 3
MOBA_HEADS_PER_STEP = 2
MOBA_LOOKAHEAD = 2
ROPE_DIMS = 32
ROPE_THETA = 500000.0
D_FF = 4 * D_MODEL
PLE_DIM = 256
RMS_EPS = 1e-6
LANES = 128
SUBLANES = 8
BF16_ROWS = 16
CONV_HALO = 8
REPACK_ROWS = 128
MLP_PART_ROWS = 256
IN_PART_ROWS = 256
QK_SCALE = HEAD_DIM ** -0.5
MOBA_Q_SCALE = QK_SCALE * math.log2(math.e)
VMEM_LIMIT = 56 * 1024 * 1024

C_GQKV = 0
C_GZ = 3 * GDN_WIDTH
C_MQ = 4 * GDN_WIDTH
C_MK = C_MQ + MOBA_WIDTH
C_MV = C_MK + MOBA_WIDTH
C_BA = C_MV + MOBA_WIDTH
C_END = C_BA + LANES


def _rms(x, w):
    return x * lax.rsqrt(jnp.mean(x * x, axis=-1, keepdims=True) + RMS_EPS) * w


def _sigmoid(x):
    return 1.0 / (1.0 + jnp.exp(-x))


def _aligned(x, m):
    return x if isinstance(x, int) else pl.multiple_of(x, m)


def _dot(a, b):
    return jnp.dot(a.astype(BF16), b.astype(BF16), preferred_element_type=F32)


def _dot_nt(a, b):
    return lax.dot_general(a.astype(BF16), b.astype(BF16), (((1,), (1,)), ((), ())),
                           preferred_element_type=F32)


def _in_proj_body(x_ref, nw_ref, win_ref, cw_ref, gp_ref, cos_ref, sa_ref, sb_ref,
                  gq_ref, gk_ref, gv_ref, gz_ref, bg_ref, mq_ref, mk_ref, mv_ref,
                  halo, w_ref, *, tm):
    i = pl.program_id(1)

    @pl.when((pl.program_id(0) == 0) & (i == 0))
    def _():
        n_gate = 2 * GDN_HEADS
        for c0 in range(0, C_BA, REPACK_ROWS):
            src = c0 if c0 < C_MQ else c0 + n_gate
            w_ref[:, c0:c0 + REPACK_ROWS] = win_ref[src:src + REPACK_ROWS, :].T.astype(BF16)
        gate_rows = jnp.concatenate([win_ref[C_MQ:C_MQ + n_gate, :],
                                     jnp.zeros((LANES - n_gate, D_MODEL), F32)], axis=0)
        w_ref[:, C_BA:C_END] = gate_rows.T.astype(BF16)

    parts = [slice(r0, r0 + IN_PART_ROWS) for r0 in range(0, tm, IN_PART_ROWS)]
    ns = range(len(parts))
    u = [None for _ in ns]

    def proj(n, c0, width):
        if u[n] is None:
            u[n] = _rms(x_ref[parts[n], :], nw_ref[...]).astype(BF16)
        return jnp.dot(u[n], w_ref[:, c0:c0 + width], preferred_element_type=F32)

    @pl.when(i == 0)
    def _():
        halo[...] = jnp.zeros_like(halo)

    pair_w = 2 * HEAD_DIM

    def conv_pair(p):
        pairs = [proj(n, C_GQKV + p * pair_w, pair_w) for n in ns]
        for s in (2 * p, 2 * p + 1):
            c0 = s * HEAD_DIM
            cw = [cw_ref[j:j + 1, c0:c0 + HEAD_DIM] for j in range(GDN_CONV)]
            which, h = divmod(s, GDN_HEADS)
            cur = [pairs[n][:, (s % 2) * HEAD_DIM:(s % 2 + 1) * HEAD_DIM] for n in ns]
            prev = halo[:, c0:c0 + HEAD_DIM]
            halo[:, c0:c0 + HEAD_DIM] = cur[-1][IN_PART_ROWS - CONV_HALO:, :]
            for n in ns:
                xs = jnp.concatenate([prev, cur[n]], axis=0)
                prev = cur[n][IN_PART_ROWS - CONV_HALO:, :]
                x1 = pltpu.roll(xs, 1, 0)
                acc = (cw[3] * xs + cw[2] * x1) + pltpu.roll(cw[1] * xs + cw[0] * x1, 2, 0)
                acc = acc[CONV_HALO:, :]
                y = acc * _sigmoid(acc)
                if which < 2:
                    inv_norm = lax.rsqrt(jnp.sum(y * y, axis=-1, keepdims=True) + RMS_EPS)
                    y = y * (inv_norm * QK_SCALE if which == 0 else inv_norm)
                (gq_ref, gk_ref, gv_ref)[which][parts[n], h * HEAD_DIM:(h + 1) * HEAD_DIM] = (
                    y.astype(BF16))

    def plain_pair(ref, c_base, p):
        vals = [proj(n, c_base + p * pair_w, pair_w) for n in ns]
        for n in ns:
            ref[parts[n], p * pair_w:(p + 1) * pair_w] = vals[n].astype(BF16)

    def rotary_pair(ref, c_base, scale, p):
        half = ROPE_DIMS // 2
        pairs = [proj(n, c_base + p * pair_w, pair_w) for n in ns]
        for n in ns:
            cos, sa, sb = cos_ref[parts[n], :], sa_ref[parts[n], :], sb_ref[parts[n], :]
            for h in (2 * p, 2 * p + 1):
                xr = pairs[n][:, (h % 2) * HEAD_DIM:(h % 2 + 1) * HEAD_DIM]
                rot = (xr * cos + pltpu.roll(xr, LANES - half, 1) * sa
                       + pltpu.roll(xr, half, 1) * sb)
                if scale is not None:
                    rot = rot * scale
                ref[parts[n], h * HEAD_DIM:(h + 1) * HEAD_DIM] = rot.astype(BF16)

    def gates():
        lane = lax.broadcasted_iota(jnp.int32, (IN_PART_ROWS, LANES), 1)
        bas = [proj(n, C_BA, LANES) for n in ns]
        for n in ns:
            xg = bas[n] + gp_ref[1:2, :]
            e = jnp.exp(-jnp.abs(xg))
            r = 1.0 / (1.0 + e)
            sig = jnp.where(xg >= 0.0, r, e * r)
            g = -jnp.exp(gp_ref[0:1, :]) * (jnp.maximum(xg, 0.0) + jnp.log1p(e))
            bg_ref[parts[n], :] = jnp.where(lane < GDN_HEADS, sig,
                                            jnp.where(lane < 2 * GDN_HEADS, g, 0.0))

    convs = [functools.partial(conv_pair, p) for p in range(3 * GDN_WIDTH // pair_w)]
    light = ([functools.partial(plain_pair, gz_ref, C_GZ, p) for p in range(GDN_WIDTH // pair_w)]
             + [functools.partial(plain_pair, mv_ref, C_MV, p) for p in range(MOBA_WIDTH // pair_w)]
             + [functools.partial(rotary_pair, mq_ref, C_MQ, MOBA_Q_SCALE, p)
                for p in range(MOBA_WIDTH // pair_w)]
             + [functools.partial(rotary_pair, mk_ref, C_MK, None, p)
                for p in range(MOBA_WIDTH // pair_w)]
             + [gates])
    done = 0
    for n, conv in enumerate(convs, start=1):
        conv()
        upto = len(light) * n // len(convs)
        for stage in light[done:upto]:
            stage()
        done = upto


def _in_proj(x, nw, w_in, layer, conv_w, gparams, cos_t, sa_t, sb_t, *, tm):
    B, T, D = x.shape
    tok = lambda width: pl.BlockSpec((None, tm, width), lambda b, i: (b, i, 0))
    const = lambda shape: pl.BlockSpec(shape, lambda b, i: (0,) * len(shape))
    table = pl.BlockSpec((tm, LANES), lambda b, i: (i, 0))
    o512 = jax.ShapeDtypeStruct((B, T, GDN_WIDTH), BF16)
    return pl.pallas_call(
        functools.partial(_in_proj_body, tm=tm),
        grid=(B, T // tm),
        in_specs=[tok(D), const((1, D)),
                  pl.BlockSpec((None,) + w_in.shape[1:], lambda b, i: (layer, 0, 0),
                               pipeline_mode=pl.Buffered(1)),
                  const((GDN_CONV, 3 * GDN_WIDTH)),
                  const((2, LANES)), table, table, table],
        out_specs=[tok(GDN_WIDTH)] * 4 + [tok(LANES)] + [tok(MOBA_WIDTH)] * 3,
        out_shape=[o512] * 4 + [jax.ShapeDtypeStruct((B, T, LANES), F32)] + [o512] * 3,
        scratch_shapes=[pltpu.VMEM((CONV_HALO, 3 * GDN_WIDTH), F32),
                        pltpu.VMEM((D, C_END), BF16)],
        compiler_params=pltpu.CompilerParams(
            dimension_semantics=("arbitrary", "arbitrary"), vmem_limit_bytes=VMEM_LIMIT),
        name="in_proj",
    )(x, nw, w_in, conv_w, gparams, cos_t, sa_t, sb_t)


def _gdn_body(q_ref, k_ref, v_ref, z_ref, bg_ref, nw_ref, o_ref,
              s_ref, mneg_ref, n_ref, o1_ref, o2_ref, a_ref, snap_ref, *, T):
    C = GDN_CHUNK
    H = GDN_HEADS
    G = GDN_GROUP
    rows_per_trip = G * C
    ntrips = T // rows_per_trip
    ri = lax.broadcasted_iota(jnp.int32, (C, C), 0)
    ci = lax.broadcasted_iota(jnp.int32, (C, C), 1)
    causal = ci <= ri
    strict = ci < ri
    eye = (ri == ci).astype(F32)
    cols = [slice(h * HEAD_DIM, (h + 1) * HEAD_DIM) for h in range(H)]
    s_ref[...] = jnp.zeros_like(s_ref)

    def phase_a(row0, slot, chunks):
        inst = [(c, h) for c in chunks for h in range(H)]
        rng = range(len(inst))
        r0s = {c: _aligned(row0 + c * C, C) for c in chunks}
        bgs = {c: bg_ref[pl.ds(r0s[c], C), :] for c in chunks}
        q = [q_ref[pl.ds(r0s[c], C), cols[h]].astype(F32) for c, h in inst]
        k = [k_ref[pl.ds(r0s[c], C), cols[h]].astype(F32) for c, h in inst]
        v = [v_ref[pl.ds(r0s[c], C), cols[h]].astype(F32) for c, h in inst]
        beta = [bgs[c][:, h:h + 1] for c, h in inst]
        gc_col, gc_last, decay = [], [], []
        for c, h in inst:
            gb = jnp.broadcast_to(bgs[c][:, H + h:H + h + 1], (C, C))
            row = jnp.sum(jnp.where(ri <= ci, gb, 0.0), axis=0, keepdims=True)
            g_row = jnp.sum(jnp.where(ri == ci, gb, 0.0), axis=0, keepdims=True)
            col = jnp.sum(jnp.where(causal, jnp.broadcast_to(g_row, (C, C)), 0.0),
                          axis=1, keepdims=True)
            gc_col.append(col)
            gc_last.append(row[:, C - 1:C])
            decay.append(jnp.exp(jnp.where(causal, col - row, -jnp.inf)))
        kb = [k[t] * beta[t] for t in rng]
        eg = [jnp.exp(gc_col[t]) for t in rng]
        kq = [_dot_nt(jnp.concatenate([kb[t], q[t]], axis=0), k[t]) for t in rng]
        yield
        low = [jnp.where(strict, kq[t][:C] * decay[t], 0.0) for t in rng]
        intra = [jnp.where(causal, kq[t][C:] * decay[t], 0.0) for t in rng]
        inv = [eye - low[t] for t in rng]
        pw = [_dot(low[t], low[t]) for t in rng]
        yield
        for _ in range(4):
            r = [_dot(jnp.concatenate([pw[t], inv[t]], axis=0), pw[t]) for t in rng]
            inv = [inv[t] + r[t][C:] for t in rng]
            pw = [r[t][:C] for t in rng]
            yield
        inv = [inv[t] + _dot(inv[t], pw[t]) for t in rng]
        yield
        wu = [_dot(inv[t], jnp.concatenate([kb[t] * eg[t], v[t] * beta[t]], axis=-1))
              for t in rng]
        yield
        kd = [k[t] * jnp.exp(gc_last[t] - gc_col[t]) for t in rng]
        r = [_dot(jnp.concatenate([kd[t].T, intra[t]], axis=0), wu[t]) for t in rng]
        mn = [r[t][:HEAD_DIM] for t in rng]
        io = [r[t][HEAD_DIM:] for t in rng]
        for t, (c, h) in enumerate(inst):
            rows = slice(c * C, (c + 1) * C)
            mneg_ref[slot, c, h] = (-mn[t][:, :HEAD_DIM]).astype(BF16)
            n_ref[slot, c, h] = mn[t][:, HEAD_DIM:]
            o1_ref[slot, rows, cols[h]] = (q[t] * eg[t] - io[t][:, :HEAD_DIM]).astype(BF16)
            o2_ref[slot, rows, cols[h]] = io[t][:, HEAD_DIM:]
            a_ref[slot, c, h] = jnp.broadcast_to(jnp.exp(gc_last[t]), (SUBLANES, HEAD_DIM))
        yield

    def phase_b_step(slot, c):
        hs = range(H)
        state = [s_ref[h] for h in hs]
        sb = [state[h].astype(BF16) for h in hs]
        ms = [jnp.dot(mneg_ref[slot, c, h], sb[h], preferred_element_type=F32) for h in hs]
        for h in hs:
            snap_ref[slot, c, h] = sb[h]
            s_ref[h] = state[h] * a_ref[slot, c, h, 0:1, :] + ms[h] + n_ref[slot, c, h]

    def phase_c(row0, slot, chunks):
        inst = [(c, h) for c in chunks for h in range(H)]
        os_ = [jnp.dot(o1_ref[slot, c * C:(c + 1) * C, cols[h]], snap_ref[slot, c, h],
                       preferred_element_type=F32) for c, h in inst]
        for t, (c, h) in enumerate(inst):
            r0 = _aligned(row0 + c * C, C)
            o = os_[t] + o2_ref[slot, c * C:(c + 1) * C, cols[h]]
            z = z_ref[pl.ds(r0, C), cols[h]].astype(F32)
            o_ref[pl.ds(r0, C), cols[h]] = (_rms(o, nw_ref[...])
                                            * (z * _sigmoid(z))).astype(o_ref.dtype)

    def trip(g, do_a, do_b, do_c):
        row0 = lambda d: _aligned((g - d) * rows_per_trip, rows_per_trip)
        slot = lambda d: (g - d) % GDN_RING
        c_steps = ([functools.partial(phase_c, row0(2), slot(2), (c,)) for c in range(G)]
                   if do_c else [])
        b_steps = [functools.partial(phase_b_step, slot(1), c) for c in range(G)] if do_b else []
        if do_a:
            per_wave = G // GDN_WAVES
            waves = [phase_a(row0(0), slot(0), tuple(range(w * per_wave, (w + 1) * per_wave)))
                     for w in range(GDN_WAVES)]
            n = 0
            while any(w is not None for w in waves):
                for i, w in enumerate(waves):
                    if w is not None and n >= i * GDN_WAVE_LAG:
                        if next(w, "done") == "done":
                            waves[i] = None
                if n % 2 == 0 and c_steps:
                    c_steps.pop(0)()
                if n % 2 == 1 and b_steps:
                    b_steps.pop(0)()
                n += 1
        while b_steps or c_steps:
            for steps in (b_steps, c_steps):
                if steps:
                    steps.pop(0)()

    def steady(g, carry):
        trip(g, True, True, True)
        return carry

    all_on = lambda g: g < ntrips and 1 <= g <= ntrips and 2 <= g <= ntrips + 1
    g = 0
    while g < ntrips + 2:
        if all_on(g):
            last = g
            while all_on(last + 1):
                last += 1
            lax.fori_loop(g, last + 1, steady, 0)
            g = last + 1
        else:
            trip(g, g < ntrips, 1 <= g <= ntrips, 2 <= g <= ntrips + 1)
            g += 1


def _gdn(gq, gk, gv, gz, bg, nw):
    B, T, _ = gq.shape
    rows_per_trip = GDN_GROUP * GDN_CHUNK
    assert T % rows_per_trip == 0
    tok = lambda width: pl.BlockSpec((None, T, width), lambda b: (b, 0, 0))
    per_chunk = (GDN_RING, GDN_GROUP, GDN_HEADS)
    return pl.pallas_call(
        functools.partial(_gdn_body, T=T),
        grid=(B,),
        in_specs=[tok(GDN_WIDTH)] * 4 + [tok(LANES), pl.BlockSpec((1, HEAD_DIM), lambda b: (0, 0))],
        out_specs=tok(GDN_WIDTH),
        out_shape=jax.ShapeDtypeStruct((B, T, GDN_WIDTH), BF16),
        scratch_shapes=[
            pltpu.VMEM((GDN_HEADS, HEAD_DIM, HEAD_DIM), F32),
            pltpu.VMEM(per_chunk + (HEAD_DIM, HEAD_DIM), BF16),
            pltpu.VMEM(per_chunk + (HEAD_DIM, HEAD_DIM), F32),
            pltpu.VMEM((GDN_RING, rows_per_trip, GDN_WIDTH), BF16),
            pltpu.VMEM((GDN_RING, rows_per_trip, GDN_WIDTH), F32),
            pltpu.VMEM(per_chunk + (SUBLANES, HEAD_DIM), F32),
            pltpu.VMEM(per_chunk + (HEAD_DIM, HEAD_DIM), BF16),
        ],
        compiler_params=pltpu.CompilerParams(
            dimension_semantics=("arbitrary",), vmem_limit_bytes=VMEM_LIMIT),
        name="gdn",
    )(gq, gk, gv, gz, bg, nw)


def _moba_body(q_ref, k_ref, v_ref, o_ref, vt_ref, *, T):
    BS = MOBA_BLOCK
    nb = T // BS
    nrow = -(-nb // SUBLANES) * SUBLANES
    neg = -jnp.inf
    heads = [slice(h * HEAD_DIM, (h + 1) * HEAD_DIM) for h in range(MOBA_HEADS_PER_STEP)]
    blk = lax.broadcasted_iota(jnp.int32, (nrow, BS), 0)
    key = lax.broadcasted_iota(jnp.int32, (BS, BS), 0)
    qry = lax.broadcasted_iota(jnp.int32, (BS, BS), 1)

    kmean = []
    for n, hd in enumerate(heads):
        means = [jnp.mean(k_ref[j * BS:(j + 1) * BS, hd].astype(F32), axis=0, keepdims=True)
                 for j in range(nb)]
        if nrow > nb:
            means.append(jnp.zeros((nrow - nb, HEAD_DIM), F32))
        kmean.append(jnp.concatenate(means, axis=0))
        for j in range(nb):
            vt_ref[n, 0:HEAD_DIM, j * BS:(j + 1) * BS] = v_ref[j * BS:(j + 1) * BS, hd].T
        vt_ref[n, HEAD_DIM:, :] = jnp.ones((BF16_ROWS, T), BF16)

    def block_scores(n, i):
        q = q_ref[i * BS:(i + 1) * BS, heads[n]]
        sel = None
        if i > MOBA_TOPK:
            gate = lax.dot_general(kmean[n], q.astype(F32), (((1,), (1,)), ((), ())),
                                   precision=lax.Precision.HIGHEST, preferred_element_type=F32)
            gate = jnp.where(blk < i, gate, neg)
            rank = jnp.zeros((nrow, BS), F32)
            for jp in range(i):
                gj = gate[jp:jp + 1, :]
                beats = (gj > gate) | ((gj == gate) & (blk > jp))
                rank = rank + jnp.where(beats, 1.0, 0.0)
            sel = jnp.where((rank < MOBA_TOPK) & (blk < i), 1.0, 0.0)
        scores = []
        for j in range(i + 1):
            s = _dot_nt(k_ref[j * BS:(j + 1) * BS, heads[n]], q)
            if j == i:
                s = jnp.where(key <= qry, s, neg)
            elif sel is not None:
                s = jnp.where(sel[j:j + 1, :] > 0.5, s, neg)
            scores.append(s)
        return scores

    def block_output(n, i, scores):
        m = scores[0].max(axis=0, keepdims=True)
        for s in scores[1:]:
            m = jnp.maximum(m, s.max(axis=0, keepdims=True))
        acc = jnp.zeros((HEAD_DIM + BF16_ROWS, BS), F32)
        for j, s in enumerate(scores):
            acc = acc + _dot(vt_ref[n, :, j * BS:(j + 1) * BS], jnp.exp2(s - m))
        out = acc[:HEAD_DIM, :] / acc[HEAD_DIM:HEAD_DIM + 1, :]
        o_ref[i * BS:(i + 1) * BS, heads[n]] = out.T.astype(o_ref.dtype)

    tasks = [(n, i) for i in range(nb) for n in range(MOBA_HEADS_PER_STEP)]
    pending = {t: block_scores(*tasks[t]) for t in range(min(MOBA_LOOKAHEAD, len(tasks)))}
    for t, (n, i) in enumerate(tasks):
        if t + MOBA_LOOKAHEAD < len(tasks):
            pending[t + MOBA_LOOKAHEAD] = block_scores(*tasks[t + MOBA_LOOKAHEAD])
        block_output(n, i, pending.pop(t))


def _moba(mq, mk, mv):
    B, T, _ = mq.shape
    assert T % MOBA_BLOCK == 0 and MOBA_HEADS % MOBA_HEADS_PER_STEP == 0
    width = MOBA_HEADS_PER_STEP * HEAD_DIM
    head = pl.BlockSpec((None, T, width), lambda b, h: (b, 0, h))
    return pl.pallas_call(
        functools.partial(_moba_body, T=T),
        grid=(B, MOBA_HEADS // MOBA_HEADS_PER_STEP),
        in_specs=[head] * 3,
        out_specs=head,
        out_shape=jax.ShapeDtypeStruct((B, T, MOBA_WIDTH), BF16),
        scratch_shapes=[pltpu.VMEM((MOBA_HEADS_PER_STEP, HEAD_DIM + BF16_ROWS, T), BF16)],
        compiler_params=pltpu.CompilerParams(
            dimension_semantics=("arbitrary", "arbitrary"), vmem_limit_bytes=VMEM_LIMIT),
        name="moba",
    )(mq, mk, mv)


def _mlp_body(og_ref, om_ref, x_ref, p_ref, wo_ref, n1_ref, n2_ref, wu_ref, wd_ref, n3_ref,
              wg_ref, wp_ref, o_ref, *, ff_chunk):
    parts = [slice(r0, r0 + MLP_PART_ROWS) for r0 in range(0, x_ref.shape[0], MLP_PART_ROWS)]
    ns = range(len(parts))
    mix = [_dot(og_ref[r, :], wo_ref[0:GDN_WIDTH, :])
           + _dot(om_ref[r, :], wo_ref[GDN_WIDTH:GDN_WIDTH + MOBA_WIDTH, :]) for r in parts]
    h = [x_ref[r, :] + _rms(mix[n], n1_ref[...]) for n, r in enumerate(parts)]
    a = [_rms(h[n], n2_ref[...]).astype(BF16) for n in ns]
    f = [None for _ in ns]
    for c0 in range(0, D_FF, ff_chunk):
        up = [jnp.dot(a[n], wu_ref[:, c0:c0 + ff_chunk], preferred_element_type=F32)
              for n in ns]
        for n in ns:
            part = _dot(jnp.square(jnp.maximum(up[n], 0.0)), wd_ref[c0:c0 + ff_chunk, :])
            f[n] = part if f[n] is None else f[n] + part
    h = [h[n] + _rms(f[n], n3_ref[...]) for n in ns]
    gate = [_dot(h[n], wg_ref[...]) for n in ns]
    ple = [_dot(p_ref[r, :], wp_ref[...]) for r in parts]
    for n, r in enumerate(parts):
        o_ref[r, :] = h[n] + _sigmoid(gate[n]) * ple[n]


def _mlp(og, om, x, p, wo, n1, n2, wu, wd, n3, wg, wp, *, tm):
    B, T, D = x.shape
    tok = lambda width: pl.BlockSpec((None, tm, width), lambda b, i: (b, i, 0))
    const = lambda shape: pl.BlockSpec(shape, lambda b, i: (0,) * len(shape),
                                       pipeline_mode=pl.Buffered(1))
    return pl.pallas_call(
        functools.partial(_mlp_body, ff_chunk=1024),
        grid=(B, T // tm),
        in_specs=[tok(GDN_WIDTH), tok(MOBA_WIDTH), tok(D), tok(PLE_DIM),
                  const((D, D)), const((1, D)), const((1, D)), const((D, D_FF)),
                  const((D_FF, D)), const((1, D)), const((D, D)), const((PLE_DIM, D))],
        out_specs=tok(D),
        out_shape=jax.ShapeDtypeStruct((B, T, D), F32),
        compiler_params=pltpu.CompilerParams(
            dimension_semantics=("arbitrary", "arbitrary"), vmem_limit_bytes=VMEM_LIMIT),
        name="mlp",
    )(og, om, x, p, wo, n1, n2, wu, wd, n3, wg, wp)


def _rope_tables(T):
    half = ROPE_DIMS // 2
    inv_freq = ROPE_THETA ** (-np.arange(half, dtype=np.float64) * (2.0 / ROPE_DIMS))
    ang = np.arange(T, dtype=np.float64)[:, None] * inv_freq[None, :]
    cos, sin = np.cos(ang), np.sin(ang)
    zeros = np.zeros((T, HEAD_DIM - ROPE_DIMS))
    z_half = np.zeros((T, half))
    cos_t = np.concatenate([cos, cos, np.ones((T, HEAD_DIM - ROPE_DIMS))], axis=-1)
    sa_t = np.concatenate([-sin, z_half, zeros], axis=-1)
    sb_t = np.concatenate([z_half, sin, zeros], axis=-1)
    cos_t, sa_t, sb_t = (jnp.asarray(t, dtype=F32) for t in (cos_t, sa_t, sb_t))
    return cos_t, sa_t, sb_t


def _layer(h, p_i, w_in, layer, conv_w, a_log, dt_bias, gdn_norm_w, w_out, attn_pre_norm,
           attn_post_norm, mlp_pre_norm, mlp_post_norm, w_up, w_down, w_ple, w_ple_gate, tables):
    B, T, D = h.shape
    tm = next(t for t in (512, MOBA_BLOCK) if T % t == 0)
    assert T % tm == 0 and T % MOBA_BLOCK == 0 and T % GDN_CHUNK == 0
    nh = 2 * GDN_HEADS
    gparams = jnp.zeros((2, LANES), F32)
    gparams = gparams.at[0, GDN_HEADS:nh].set(a_log.astype(F32))
    gparams = gparams.at[1, GDN_HEADS:nh].set(dt_bias.astype(F32))
    row = lambda v: v.reshape(1, -1).astype(F32)

    gq, gk, gv, gz, bg, mq, mk, mv = _in_proj(
        h, row(attn_pre_norm), jnp.swapaxes(w_in.astype(F32), 1, 2), layer, conv_w.astype(F32),
        gparams, *tables,
        tm=tm)
    o_gdn = _gdn(gq, gk, gv, gz, bg, row(gdn_norm_w))
    o_moba = _moba(mq, mk, mv)
    return _mlp(o_gdn, o_moba, h, p_i, w_out.astype(BF16), row(attn_post_norm),
                row(mlp_pre_norm), w_up.astype(BF16), w_down.astype(BF16),
                row(mlp_post_norm), w_ple_gate.astype(BF16), w_ple.astype(BF16), tm=tm)


def kernel(x, p, w_in, conv_w, a_log, dt_bias, gdn_norm_w, w_out, attn_pre_norm, attn_post_norm,
           mlp_pre_norm, mlp_post_norm, w_up, w_down, w_ple, w_ple_gate):
    tables = _rope_tables(x.shape[1])
    h = x
    for i in range(w_in.shape[0]):
        h = _layer(h, p[i], w_in, i, conv_w[i], a_log[i], dt_bias[i], gdn_norm_w[i], w_out[i],
                   attn_pre_norm[i], attn_post_norm[i], mlp_pre_norm[i], mlp_post_norm[i],
                   w_up[i], w_down[i], w_ple[i], w_ple_gate[i], tables)
    return h
```

```python
import functools
import math

import jax
import jax.numpy as jnp
import numpy as np
from jax import lax
from jax.experimental import pallas as pl
from jax.experimental.pallas import tpu as pltpu

F32 = jnp.float32
BF16 = jnp.bfloat16

D_MODEL = 1024
HEAD_DIM = 128
GDN_HEADS = 4
GDN_WIDTH = GDN_HEADS * HEAD_DIM
GDN_CONV = 4
GDN_CHUNK = 64
GDN_GROUP = 4
GDN_WAVES = 2
GDN_WAVE_LAG = 1
GDN_RING = 3
MOBA_HEADS = 4
MOBA_WIDTH = MOBA_HEADS * HEAD_DIM
MOBA_BLOCK = 256
MOBA_TOPK = 3
MOBA_HEADS_PER_STEP = 2
MOBA_LOOKAHEAD = 2
ROPE_DIMS = 32
ROPE_THETA = 500000.0
D_FF = 4 * D_MODEL
PLE_DIM = 256
RMS_EPS = 1e-6
LANES = 128
SUBLANES = 8
BF16_ROWS = 16
CONV_HALO = 8
REPACK_ROWS = 128
MLP_PART_ROWS = 256
IN_PART_ROWS = 128
QK_SCALE = HEAD_DIM ** -0.5
MOBA_Q_SCALE = QK_SCALE * math.log2(math.e)
VMEM_LIMIT = 56 * 1024 * 1024

C_GQKV = 0
C_GZ = 3 * GDN_WIDTH
C_MQ = 4 * GDN_WIDTH
C_MK = C_MQ + MOBA_WIDTH
C_MV = C_MK + MOBA_WIDTH
C_BA = C_MV + MOBA_WIDTH
C_END = C_BA + LANES


def _rms(x, w):
    return x * lax.rsqrt(jnp.mean(x * x, axis=-1, keepdims=True) + RMS_EPS) * w


def _sigmoid(x):
    return 1.0 / (1.0 + jnp.exp(-x))


def _aligned(x, m):
    return x if isinstance(x, int) else pl.multiple_of(x, m)


def _dot(a, b):
    return jnp.dot(a.astype(BF16), b.astype(BF16), preferred_element_type=F32)


def _dot_nt(a, b):
    return lax.dot_general(a.astype(BF16), b.astype(BF16), (((1,), (1,)), ((), ())),
                           preferred_element_type=F32)


def _in_proj_body(x_ref, nw_ref, win_ref, cw_ref, gp_ref, cos_ref, sa_ref, sb_ref,
                  gq_ref, gk_ref, gv_ref, gz_ref, bg_ref, mq_ref, mk_ref, mv_ref,
                  halo, w_ref, *, tm):
    i = pl.program_id(1)

    @pl.when((pl.program_id(0) == 0) & (i == 0))
    def _():
        n_gate = 2 * GDN_HEADS
        for c0 in range(0, C_BA, REPACK_ROWS):
            src = c0 if c0 < C_MQ else c0 + n_gate
            w_ref[:, c0:c0 + REPACK_ROWS] = win_ref[src:src + REPACK_ROWS, :].T.astype(BF16)
        gate_rows = jnp.concatenate([win_ref[C_MQ:C_MQ + n_gate, :],
                                     jnp.zeros((LANES - n_gate, D_MODEL), F32)], axis=0)
        w_ref[:, C_BA:C_END] = gate_rows.T.astype(BF16)

    parts = [slice(r0, r0 + IN_PART_ROWS) for r0 in range(0, tm, IN_PART_ROWS)]
    ns = range(len(parts))
    u = [None for _ in ns]

    def proj(n, c0, width):
        if u[n] is None:
            u[n] = _rms(x_ref[parts[n], :], nw_ref[...]).astype(BF16)
        return jnp.dot(u[n], w_ref[:, c0:c0 + width], preferred_element_type=F32)

    @pl.when(i == 0)
    def _():
        halo[...] = jnp.zeros_like(halo)

    pair_w = 2 * HEAD_DIM

    def conv_pair(p):
        pairs = [proj(n, C_GQKV + p * pair_w, pair_w) for n in ns]
        for s in (2 * p, 2 * p + 1):
            c0 = s * HEAD_DIM
            cw = [cw_ref[j:j + 1, c0:c0 + HEAD_DIM] for j in range(GDN_CONV)]
            which, h = divmod(s, GDN_HEADS)
            cur = [pairs[n][:, (s % 2) * HEAD_DIM:(s % 2 + 1) * HEAD_DIM] for n in ns]
            prev = halo[:, c0:c0 + HEAD_DIM]
            halo[:, c0:c0 + HEAD_DIM] = cur[-1][IN_PART_ROWS - CONV_HALO:, :]
            for n in ns:
                xs = jnp.concatenate([prev, cur[n]], axis=0)
                prev = cur[n][IN_PART_ROWS - CONV_HALO:, :]
                x1 = pltpu.roll(xs, 1, 0)
                acc = (cw[3] * xs + cw[2] * x1) + pltpu.roll(cw[1] * xs + cw[0] * x1, 2, 0)
                acc = acc[CONV_HALO:, :]
                y = acc * _sigmoid(acc)
                if which < 2:
                    inv_norm = lax.rsqrt(jnp.sum(y * y, axis=-1, keepdims=True) + RMS_EPS)
                    y = y * (inv_norm * QK_SCALE if which == 0 else inv_norm)
                (gq_ref, gk_ref, gv_ref)[which][parts[n], h * HEAD_DIM:(h + 1) * HEAD_DIM] = (
                    y.astype(BF16))

    def plain_pair(ref, c_base, p):
        vals = [proj(n, c_base + p * pair_w, pair_w) for n in ns]
        for n in ns:
            ref[parts[n], p * pair_w:(p + 1) * pair_w] = vals[n].astype(BF16)

    def rotary_pair(ref, c_base, scale, p):
        half = ROPE_DIMS // 2
        pairs = [proj(n, c_base + p * pair_w, pair_w) for n in ns]
        for n in ns:
            cos, sa, sb = cos_ref[parts[n], :], sa_ref[parts[n], :], sb_ref[parts[n], :]
            for h in (2 * p, 2 * p + 1):
                xr = pairs[n][:, (h % 2) * HEAD_DIM:(h % 2 + 1) * HEAD_DIM]
                rot = (xr * cos + pltpu.roll(xr, LANES - half, 1) * sa
                       + pltpu.roll(xr, half, 1) * sb)
                if scale is not None:
                    rot = rot * scale
                ref[parts[n], h * HEAD_DIM:(h + 1) * HEAD_DIM] = rot.astype(BF16)

    def gates():
        lane = lax.broadcasted_iota(jnp.int32, (IN_PART_ROWS, LANES), 1)
        bas = [proj(n, C_BA, LANES) for n in ns]
        for n in ns:
            xg = bas[n] + gp_ref[1:2, :]
            e = jnp.exp(-jnp.abs(xg))
            r = 1.0 / (1.0 + e)
            sig = jnp.where(xg >= 0.0, r, e * r)
            g = -jnp.exp(gp_ref[0:1, :]) * (jnp.maximum(xg, 0.0) + jnp.log1p(e))
            bg_ref[parts[n], :] = jnp.where(lane < GDN_HEADS, sig,
                                            jnp.where(lane < 2 * GDN_HEADS, g, 0.0))

    convs = [functools.partial(conv_pair, p) for p in range(3 * GDN_WIDTH // pair_w)]
    light = ([functools.partial(plain_pair, gz_ref, C_GZ, p) for p in range(GDN_WIDTH // pair_w)]
             + [functools.partial(plain_pair, mv_ref, C_MV, p) for p in range(MOBA_WIDTH // pair_w)]
             + [functools.partial(rotary_pair, mq_ref, C_MQ, MOBA_Q_SCALE, p)
                for p in range(MOBA_WIDTH // pair_w)]
             + [functools.partial(rotary_pair, mk_ref, C_MK, None, p)
                for p in range(MOBA_WIDTH // pair_w)]
             + [gates])
    done = 0
    for n, conv in enumerate(convs, start=1):
        conv()
        upto = len(light) * n // len(convs)
        for stage in light[done:upto]:
            stage()
        done = upto


def _in_proj(x, nw, w_in, layer, conv_w, gparams, cos_t, sa_t, sb_t, *, tm):
    B, T, D = x.shape
    tok = lambda width: pl.BlockSpec((None, tm, width), lambda b, i: (b, i, 0))
    const = lambda shape: pl.BlockSpec(shape, lambda b, i: (0,) * len(shape))
    table = pl.BlockSpec((tm, LANES), lambda b, i: (i, 0))
    o512 = jax.ShapeDtypeStruct((B, T, GDN_WIDTH), BF16)
    return pl.pallas_call(
        functools.partial(_in_proj_body, tm=tm),
        grid=(B, T // tm),
        in_specs=[tok(D), const((1, D)),
                  pl.BlockSpec((None,) + w_in.shape[1:], lambda b, i: (layer, 0, 0),
                               pipeline_mode=pl.Buffered(1)),
                  const((GDN_CONV, 3 * GDN_WIDTH)),
                  const((2, LANES)), table, table, table],
        out_specs=[tok(GDN_WIDTH)] * 4 + [tok(LANES)] + [tok(MOBA_WIDTH)] * 3,
        out_shape=[o512] * 4 + [jax.ShapeDtypeStruct((B, T, LANES), F32)] + [o512] * 3,
        scratch_shapes=[pltpu.VMEM((CONV_HALO, 3 * GDN_WIDTH), F32),
                        pltpu.VMEM((D, C_END), BF16)],
        compiler_params=pltpu.CompilerParams(
            dimension_semantics=("arbitrary", "arbitrary"), vmem_limit_bytes=VMEM_LIMIT),
        name="in_proj",
    )(x, nw, w_in, conv_w, gparams, cos_t, sa_t, sb_t)


def _gdn_body(q_ref, k_ref, v_ref, z_ref, bg_ref, nw_ref, o_ref,
              s_ref, mneg_ref, n_ref, o1_ref, o2_ref, a_ref, snap_ref, *, T):
    C = GDN_CHUNK
    H = GDN_HEADS
    G = GDN_GROUP
    rows_per_trip = G * C
    ntrips = T // rows_per_trip
    ri = lax.broadcasted_iota(jnp.int32, (C, C), 0)
    ci = lax.broadcasted_iota(jnp.int32, (C, C), 1)
    causal = ci <= ri
    strict = ci < ri
    eye = (ri == ci).astype(F32)
    cols = [slice(h * HEAD_DIM, (h + 1) * HEAD_DIM) for h in range(H)]
    s_ref[...] = jnp.zeros_like(s_ref)

    def phase_a(row0, slot, chunks):
        inst = [(c, h) for c in chunks for h in range(H)]
        rng = range(len(inst))
        r0s = {c: _aligned(row0 + c * C, C) for c in chunks}
        bgs = {c: bg_ref[pl.ds(r0s[c], C), :] for c in chunks}
        q = [q_ref[pl.ds(r0s[c], C), cols[h]].astype(F32) for c, h in inst]
        k = [k_ref[pl.ds(r0s[c], C), cols[h]].astype(F32) for c, h in inst]
        v = [v_ref[pl.ds(r0s[c], C), cols[h]].astype(F32) for c, h in inst]
        beta = [bgs[c][:, h:h + 1] for c, h in inst]
        gc_col, gc_last, decay = [], [], []
        for c, h in inst:
            gb = jnp.broadcast_to(bgs[c][:, H + h:H + h + 1], (C, C))
            row = jnp.sum(jnp.where(ri <= ci, gb, 0.0), axis=0, keepdims=True)
            g_row = jnp.sum(jnp.where(ri == ci, gb, 0.0), axis=0, keepdims=True)
            col = jnp.sum(jnp.where(causal, jnp.broadcast_to(g_row, (C, C)), 0.0),
                          axis=1, keepdims=True)
            gc_col.append(col)
            gc_last.append(row[:, C - 1:C])
            decay.append(jnp.exp(jnp.where(causal, col - row, -jnp.inf)))
        kb = [k[t] * beta[t] for t in rng]
        eg = [jnp.exp(gc_col[t]) for t in rng]
        kq = [_dot_nt(jnp.concatenate([kb[t], q[t]], axis=0), k[t]) for t in rng]
        yield
        low = [jnp.where(strict, kq[t][:C] * decay[t], 0.0) for t in rng]
        intra = [jnp.where(causal, kq[t][C:] * decay[t], 0.0) for t in rng]
        inv = [eye - low[t] for t in rng]
        pw = [_dot(low[t], low[t]) for t in rng]
        yield
        for _ in range(4):
            r = [_dot(jnp.concatenate([pw[t], inv[t]], axis=0), pw[t]) for t in rng]
            inv = [inv[t] + r[t][C:] for t in rng]
            pw = [r[t][:C] for t in rng]
            yield
        inv = [inv[t] + _dot(inv[t], pw[t]) for t in rng]
        yield
        wu = [_dot(inv[t], jnp.concatenate([kb[t] * eg[t], v[t] * beta[t]], axis=-1))
              for t in rng]
        yield
        kd = [k[t] * jnp.exp(gc_last[t] - gc_col[t]) for t in rng]
        r = [_dot(jnp.concatenate([kd[t].T, intra[t]], axis=0), wu[t]) for t in rng]
        mn = [r[t][:HEAD_DIM] for t in rng]
        io = [r[t][HEAD_DIM:] for t in rng]
        for t, (c, h) in enumerate(inst):
            rows = slice(c * C, (c + 1) * C)
            mneg_ref[slot, c, h] = (-mn[t][:, :HEAD_DIM]).astype(BF16)
            n_ref[slot, c, h] = mn[t][:, HEAD_DIM:]
            o1_ref[slot, rows, cols[h]] = (q[t] * eg[t] - io[t][:, :HEAD_DIM]).astype(BF16)
            o2_ref[slot, rows, cols[h]] = io[t][:, HEAD_DIM:]
            a_ref[slot, c, h] = jnp.broadcast_to(jnp.exp(gc_last[t]), (SUBLANES, HEAD_DIM))
        yield

    def phase_b_step(slot, c):
        hs = range(H)
        state = [s_ref[h] for h in hs]
        sb = [state[h].astype(BF16) for h in hs]
        ms = [jnp.dot(mneg_ref[slot, c, h], sb[h], preferred_element_type=F32) for h in hs]
        for h in hs:
            snap_ref[slot, c, h] = sb[h]
            s_ref[h] = state[h] * a_ref[slot, c, h, 0:1, :] + ms[h] + n_ref[slot, c, h]

    def phase_c(row0, slot, chunks):
        inst = [(c, h) for c in chunks for h in range(H)]
        os_ = [jnp.dot(o1_ref[slot, c * C:(c + 1) * C, cols[h]], snap_ref[slot, c, h],
                       preferred_element_type=F32) for c, h in inst]
        for t, (c, h) in enumerate(inst):
            r0 = _aligned(row0 + c * C, C)
            o = os_[t] + o2_ref[slot, c * C:(c + 1) * C, cols[h]]
            z = z_ref[pl.ds(r0, C), cols[h]].astype(F32)
            o_ref[pl.ds(r0, C), cols[h]] = (_rms(o, nw_ref[...])
                                            * (z * _sigmoid(z))).astype(o_ref.dtype)

    def trip(g, do_a, do_b, do_c):
        row0 = lambda d: _aligned((g - d) * rows_per_trip, rows_per_trip)
        slot = lambda d: (g - d) % GDN_RING
        c_steps = ([functools.partial(phase_c, row0(2), slot(2), (c,)) for c in range(G)]
                   if do_c else [])
        b_steps = [functools.partial(phase_b_step, slot(1), c) for c in range(G)] if do_b else []
        if do_a:
            per_wave = G // GDN_WAVES
            waves = [phase_a(row0(0), slot(0), tuple(range(w * per_wave, (w + 1) * per_wave)))
                     for w in range(GDN_WAVES)]
            n = 0
            while any(w is not None for w in waves):
                for i, w in enumerate(waves):
                    if w is not None and n >= i * GDN_WAVE_LAG:
                        if next(w, "done") == "done":
                            waves[i] = None
                if n % 2 == 0 and c_steps:
                    c_steps.pop(0)()
                if n % 2 == 1 and b_steps:
                    b_steps.pop(0)()
                n += 1
        while b_steps or c_steps:
            for steps in (b_steps, c_steps):
                if steps:
                    steps.pop(0)()

    def steady(g, carry):
        trip(g, True, True, True)
        return carry

    all_on = lambda g: g < ntrips and 1 <= g <= ntrips and 2 <= g <= ntrips + 1
    g = 0
    while g < ntrips + 2:
        if all_on(g):
            last = g
            while all_on(last + 1):
                last += 1
            lax.fori_loop(g, last + 1, steady, 0)
            g = last + 1
        else:
            trip(g, g < ntrips, 1 <= g <= ntrips, 2 <= g <= ntrips + 1)
            g += 1


def _gdn(gq, gk, gv, gz, bg, nw):
    B, T, _ = gq.shape
    rows_per_trip = GDN_GROUP * GDN_CHUNK
    assert T % rows_per_trip == 0
    tok = lambda width: pl.BlockSpec((None, T, width), lambda b: (b, 0, 0))
    per_chunk = (GDN_RING, GDN_GROUP, GDN_HEADS)
    return pl.pallas_call(
        functools.partial(_gdn_body, T=T),
        grid=(B,),
        in_specs=[tok(GDN_WIDTH)] * 4 + [tok(LANES), pl.BlockSpec((1, HEAD_DIM), lambda b: (0, 0))],
        out_specs=tok(GDN_WIDTH),
        out_shape=jax.ShapeDtypeStruct((B, T, GDN_WIDTH), BF16),
        scratch_shapes=[
            pltpu.VMEM((GDN_HEADS, HEAD_DIM, HEAD_DIM), F32),
            pltpu.VMEM(per_chunk + (HEAD_DIM, HEAD_DIM), BF16),
            pltpu.VMEM(per_chunk + (HEAD_DIM, HEAD_DIM), F32),
            pltpu.VMEM((GDN_RING, rows_per_trip, GDN_WIDTH), BF16),
            pltpu.VMEM((GDN_RING, rows_per_trip, GDN_WIDTH), F32),
            pltpu.VMEM(per_chunk + (SUBLANES, HEAD_DIM), F32),
            pltpu.VMEM(per_chunk + (HEAD_DIM, HEAD_DIM), BF16),
        ],
        compiler_params=pltpu.CompilerParams(
            dimension_semantics=("arbitrary",), vmem_limit_bytes=VMEM_LIMIT),
        name="gdn",
    )(gq, gk, gv, gz, bg, nw)


def _moba_body(q_ref, k_ref, v_ref, o_ref, vt_ref, *, T):
    BS = MOBA_BLOCK
    nb = T // BS
    nrow = -(-nb // SUBLANES) * SUBLANES
    neg = -jnp.inf
    heads = [slice(h * HEAD_DIM, (h + 1) * HEAD_DIM) for h in range(MOBA_HEADS_PER_STEP)]
    blk = lax.broadcasted_iota(jnp.int32, (nrow, BS), 0)
    key = lax.broadcasted_iota(jnp.int32, (BS, BS), 0)
    qry = lax.broadcasted_iota(jnp.int32, (BS, BS), 1)

    kmean = []
    for n, hd in enumerate(heads):
        means = [jnp.mean(k_ref[j * BS:(j + 1) * BS, hd].astype(F32), axis=0, keepdims=True)
                 for j in range(nb)]
        if nrow > nb:
            means.append(jnp.zeros((nrow - nb, HEAD_DIM), F32))
        kmean.append(jnp.concatenate(means, axis=0))
        for j in range(nb):
            vt_ref[n, 0:HEAD_DIM, j * BS:(j + 1) * BS] = v_ref[j * BS:(j + 1) * BS, hd].T
        vt_ref[n, HEAD_DIM:, :] = jnp.ones((BF16_ROWS, T), BF16)

    def block_scores(n, i):
        q = q_ref[i * BS:(i + 1) * BS, heads[n]]
        sel = None
        if i > MOBA_TOPK:
            gate = lax.dot_general(kmean[n], q.astype(F32), (((1,), (1,)), ((), ())),
                                   precision=lax.Precision.HIGHEST, preferred_element_type=F32)
            gate = jnp.where(blk < i, gate, neg)
            rank = jnp.zeros((nrow, BS), F32)
            for jp in range(i):
                gj = gate[jp:jp + 1, :]
                beats = (gj > gate) | ((gj == gate) & (blk > jp))
                rank = rank + jnp.where(beats, 1.0, 0.0)
            sel = jnp.where((rank < MOBA_TOPK) & (blk < i), 1.0, 0.0)
        scores = []
        for j in range(i + 1):
            s = _dot_nt(k_ref[j * BS:(j + 1) * BS, heads[n]], q)
            if j == i:
                s = jnp.where(key <= qry, s, neg)
            elif sel is not None:
                s = jnp.where(sel[j:j + 1, :] > 0.5, s, neg)
            scores.append(s)
        return scores

    def block_output(n, i, scores):
        m = scores[0].max(axis=0, keepdims=True)
        for s in scores[1:]:
            m = jnp.maximum(m, s.max(axis=0, keepdims=True))
        acc = jnp.zeros((HEAD_DIM + BF16_ROWS, BS), F32)
        for j, s in enumerate(scores):
            acc = acc + _dot(vt_ref[n, :, j * BS:(j + 1) * BS], jnp.exp2(s - m))
        out = acc[:HEAD_DIM, :] / acc[HEAD_DIM:HEAD_DIM + 1, :]
        o_ref[i * BS:(i + 1) * BS, heads[n]] = out.T.astype(o_ref.dtype)

    tasks = [(n, i) for i in range(nb) for n in range(MOBA_HEADS_PER_STEP)]
    pending = {t: block_scores(*tasks[t]) for t in range(min(MOBA_LOOKAHEAD, len(tasks)))}
    for t, (n, i) in enumerate(tasks):
        if t + MOBA_LOOKAHEAD < len(tasks):
            pending[t + MOBA_LOOKAHEAD] = block_scores(*tasks[t + MOBA_LOOKAHEAD])
        block_output(n, i, pending.pop(t))


def _moba(mq, mk, mv):
    B, T, _ = mq.shape
    assert T % MOBA_BLOCK == 0 and MOBA_HEADS % MOBA_HEADS_PER_STEP == 0
    width = MOBA_HEADS_PER_STEP * HEAD_DIM
    head = pl.BlockSpec((None, T, width), lambda b, h: (b, 0, h))
    return pl.pallas_call(
        functools.partial(_moba_body, T=T),
        grid=(B, MOBA_HEADS // MOBA_HEADS_PER_STEP),
        in_specs=[head] * 3,
        out_specs=head,
        out_shape=jax.ShapeDtypeStruct((B, T, MOBA_WIDTH), BF16),
        scratch_shapes=[pltpu.VMEM((MOBA_HEADS_PER_STEP, HEAD_DIM + BF16_ROWS, T), BF16)],
        compiler_params=pltpu.CompilerParams(
            dimension_semantics=("arbitrary", "arbitrary"), vmem_limit_bytes=VMEM_LIMIT),
        name="moba",
    )(mq, mk, mv)


def _mlp_body(og_ref, om_ref, x_ref, p_ref, wo_ref, n1_ref, n2_ref, wu_ref, wd_ref, n3_ref,
              wg_ref, wp_ref, o_ref, *, ff_chunk):
    parts = [slice(r0, r0 + MLP_PART_ROWS) for r0 in range(0, x_ref.shape[0], MLP_PART_ROWS)]
    ns = range(len(parts))
    mix = [_dot(og_ref[r, :], wo_ref[0:GDN_WIDTH, :])
           + _dot(om_ref[r, :], wo_ref[GDN_WIDTH:GDN_WIDTH + MOBA_WIDTH, :]) for r in parts]
    h = [x_ref[r, :] + _rms(mix[n], n1_ref[...]) for n, r in enumerate(parts)]
    a = [_rms(h[n], n2_ref[...]).astype(BF16) for n in ns]
    f = [None for _ in ns]
    for c0 in range(0, D_FF, ff_chunk):
        up = [jnp.dot(a[n], wu_ref[:, c0:c0 + ff_chunk], preferred_element_type=F32)
              for n in ns]
        for n in ns:
            part = _dot(jnp.square(jnp.maximum(up[n], 0.0)), wd_ref[c0:c0 + ff_chunk, :])
            f[n] = part if f[n] is None else f[n] + part
    h = [h[n] + _rms(f[n], n3_ref[...]) for n in ns]
    gate = [_dot(h[n], wg_ref[...]) for n in ns]
    ple = [_dot(p_ref[r, :], wp_ref[...]) for r in parts]
    for n, r in enumerate(parts):
        o_ref[r, :] = h[n] + _sigmoid(gate[n]) * ple[n]


def _mlp(og, om, x, p, wo, n1, n2, wu, wd, n3, wg, wp, *, tm):
    B, T, D = x.shape
    tok = lambda width: pl.BlockSpec((None, tm, width), lambda b, i: (b, i, 0))
    const = lambda shape: pl.BlockSpec(shape, lambda b, i: (0,) * len(shape),
                                       pipeline_mode=pl.Buffered(1))
    return pl.pallas_call(
        functools.partial(_mlp_body, ff_chunk=1024),
        grid=(B, T // tm),
        in_specs=[tok(GDN_WIDTH), tok(MOBA_WIDTH), tok(D), tok(PLE_DIM),
                  const((D, D)), const((1, D)), const((1, D)), const((D, D_FF)),
                  const((D_FF, D)), const((1, D)), const((D, D)), const((PLE_DIM, D))],
        out_specs=tok(D),
        out_shape=jax.ShapeDtypeStruct((B, T, D), F32),
        compiler_params=pltpu.CompilerParams(
            dimension_semantics=("arbitrary", "arbitrary"), vmem_limit_bytes=VMEM_LIMIT),
        name="mlp",
    )(og, om, x, p, wo, n1, n2, wu, wd, n3, wg, wp)


def _rope_tables(T):
    half = ROPE_DIMS // 2
    inv_freq = ROPE_THETA ** (-np.arange(half, dtype=np.float64) * (2.0 / ROPE_DIMS))
    ang = np.arange(T, dtype=np.float64)[:, None] * inv_freq[None, :]
    cos, sin = np.cos(ang), np.sin(ang)
    zeros = np.zeros((T, HEAD_DIM - ROPE_DIMS))
    z_half = np.zeros((T, half))
    cos_t = np.concatenate([cos, cos, np.ones((T, HEAD_DIM - ROPE_DIMS))], axis=-1)
    sa_t = np.concatenate([-sin, z_half, zeros], axis=-1)
    sb_t = np.concatenate([z_half, sin, zeros], axis=-1)
    cos_t, sa_t, sb_t = (jnp.asarray(t, dtype=F32) for t in (cos_t, sa_t, sb_t))
    return cos_t, sa_t, sb_t


def _layer(h, p_i, w_in, layer, conv_w, a_log, dt_bias, gdn_norm_w, w_out, attn_pre_norm,
           attn_post_norm, mlp_pre_norm, mlp_post_norm, w_up, w_down, w_ple, w_ple_gate, tables):
    B, T, D = h.shape
    tm = next(t for t in (512, MOBA_BLOCK) if T % t == 0)
    assert T % tm == 0 and T % MOBA_BLOCK == 0 and T % GDN_CHUNK == 0
    nh = 2 * GDN_HEADS
    gparams = jnp.zeros((2, LANES), F32)
    gparams = gparams.at[0, GDN_HEADS:nh].set(a_log.astype(F32))
    gparams = gparams.at[1, GDN_HEADS:nh].set(dt_bias.astype(F32))
    row = lambda v: v.reshape(1, -1).astype(F32)

    gq, gk, gv, gz, bg, mq, mk, mv = _in_proj(
        h, row(attn_pre_norm), jnp.swapaxes(w_in.astype(F32), 1, 2), layer, conv_w.astype(F32),
        gparams, *tables,
        tm=tm)
    o_gdn = _gdn(gq, gk, gv, gz, bg, row(gdn_norm_w))
    o_moba = _moba(mq, mk, mv)
    return _mlp(o_gdn, o_moba, h, p_i, w_out.astype(BF16), row(attn_post_norm),
                row(mlp_pre_norm), w_up.astype(BF16), w_down.astype(BF16),
                row(mlp_post_norm), w_ple_gate.astype(BF16), w_ple.astype(BF16), tm=tm)


def kernel(x, p, w_in, conv_w, a_log, dt_bias, gdn_norm_w, w_out, attn_pre_norm, attn_post_norm,
           mlp_pre_norm, mlp_post_norm, w_up, w_down, w_ple, w_ple_gate):
    tables = _rope_tables(x.shape[1])
    h = x
    for i in range(w_in.shape[0]):
        h = _layer(h, p[i], w_in, i, conv_w[i], a_log[i], dt_bias[i], gdn_norm_w[i], w_out[i],
                   attn_pre_norm[i], attn_post_norm[i], mlp_pre_norm[i], mlp_post_norm[i],
                   w_up[i], w_down[i], w_ple[i], w_ple_gate[i], tables)
    return h
```

```python
import functools
import math

import jax
import jax.numpy as jnp
import numpy as np
from jax import lax
from jax.experimental import pallas as pl
from jax.experimental.pallas import tpu as pltpu

F32 = jnp.float32
BF16 = jnp.bfloat16

D_MODEL = 1024
HEAD_DIM = 128
GDN_HEADS = 4
GDN_WIDTH = GDN_HEADS * HEAD_DIM
GDN_CONV = 4
GDN_CHUNK = 64
GDN_GROUP = 4
GDN_WAVES = 2
GDN_WAVE_LAG = 1
GDN_RING = 3
MOBA_HEADS = 4
MOBA_WIDTH = MOBA_HEADS * HEAD_DIM
MOBA_BLOCK = 256
MOBA_TOPK = 3
MOBA_HEADS_PER_STEP = 2
MOBA_LOOKAHEAD = 2
ROPE_DIMS = 32
ROPE_THETA = 500000.0
D_FF = 4 * D_MODEL
PLE_DIM = 256
RMS_EPS = 1e-6
LANES = 128
SUBLANES = 8
BF16_ROWS = 16
CONV_HALO = 8
REPACK_ROWS = 128
MLP_PART_ROWS = 256
IN_PART_ROWS = 128
IN_PARTS_TOGETHER = 1
QK_SCALE = HEAD_DIM ** -0.5
MOBA_Q_SCALE = QK_SCALE * math.log2(math.e)
VMEM_LIMIT = 56 * 1024 * 1024

C_GQKV = 0
C_GZ = 3 * GDN_WIDTH
C_MQ = 4 * GDN_WIDTH
C_MK = C_MQ + MOBA_WIDTH
C_MV = C_MK + MOBA_WIDTH
C_BA = C_MV + MOBA_WIDTH
C_END = C_BA + LANES


def _rms(x, w):
    return x * lax.rsqrt(jnp.mean(x * x, axis=-1, keepdims=True) + RMS_EPS) * w


def _sigmoid(x):
    return 1.0 / (1.0 + jnp.exp(-x))


def _aligned(x, m):
    return x if isinstance(x, int) else pl.multiple_of(x, m)


def _dot(a, b):
    return jnp.dot(a.astype(BF16), b.astype(BF16), preferred_element_type=F32)


def _dot_nt(a, b):
    return lax.dot_general(a.astype(BF16), b.astype(BF16), (((1,), (1,)), ((), ())),
                           preferred_element_type=F32)


def _in_proj_body(x_ref, nw_ref, win_ref, cw_ref, gp_ref, cos_ref, sa_ref, sb_ref,
                  gq_ref, gk_ref, gv_ref, gz_ref, bg_ref, mq_ref, mk_ref, mv_ref,
                  halo, w_ref, *, tm):
    i = pl.program_id(1)

    @pl.when((pl.program_id(0) == 0) & (i == 0))
    def _():
        n_gate = 2 * GDN_HEADS
        for c0 in range(0, C_BA, REPACK_ROWS):
            src = c0 if c0 < C_MQ else c0 + n_gate
            w_ref[:, c0:c0 + REPACK_ROWS] = win_ref[src:src + REPACK_ROWS, :].T.astype(BF16)
        gate_rows = jnp.concatenate([win_ref[C_MQ:C_MQ + n_gate, :],
                                     jnp.zeros((LANES - n_gate, D_MODEL), F32)], axis=0)
        w_ref[:, C_BA:C_END] = gate_rows.T.astype(BF16)

    parts = [slice(r0, r0 + IN_PART_ROWS) for r0 in range(0, tm, IN_PART_ROWS)]
    ns = []
    u = [None for _ in parts]

    def proj(n, c0, width):
        if u[n] is None:
            u[n] = _rms(x_ref[parts[n], :], nw_ref[...]).astype(BF16)
        return jnp.dot(u[n], w_ref[:, c0:c0 + width], preferred_element_type=F32)

    @pl.when(i == 0)
    def _():
        halo[...] = jnp.zeros_like(halo)

    pair_w = 2 * HEAD_DIM

    def conv_pair(p):
        pairs = {n: proj(n, C_GQKV + p * pair_w, pair_w) for n in ns}
        for s in (2 * p, 2 * p + 1):
            c0 = s * HEAD_DIM
            cw = [cw_ref[j:j + 1, c0:c0 + HEAD_DIM] for j in range(GDN_CONV)]
            which, h = divmod(s, GDN_HEADS)
            cur = {n: pairs[n][:, (s % 2) * HEAD_DIM:(s % 2 + 1) * HEAD_DIM] for n in ns}
            prev = halo[:, c0:c0 + HEAD_DIM]
            halo[:, c0:c0 + HEAD_DIM] = cur[ns[-1]][IN_PART_ROWS - CONV_HALO:, :]
            for n in ns:
                xs = jnp.concatenate([prev, cur[n]], axis=0)
                prev = cur[n][IN_PART_ROWS - CONV_HALO:, :]
                x1 = pltpu.roll(xs, 1, 0)
                acc = (cw[3] * xs + cw[2] * x1) + pltpu.roll(cw[1] * xs + cw[0] * x1, 2, 0)
                acc = acc[CONV_HALO:, :]
                y = acc * _sigmoid(acc)
                if which < 2:
                    inv_norm = lax.rsqrt(jnp.sum(y * y, axis=-1, keepdims=True) + RMS_EPS)
                    y = y * (inv_norm * QK_SCALE if which == 0 else inv_norm)
                (gq_ref, gk_ref, gv_ref)[which][parts[n], h * HEAD_DIM:(h + 1) * HEAD_DIM] = (
                    y.astype(BF16))

    def plain_pair(ref, c_base, p):
        vals = {n: proj(n, c_base + p * pair_w, pair_w) for n in ns}
        for n in ns:
            ref[parts[n], p * pair_w:(p + 1) * pair_w] = vals[n].astype(BF16)

    def rotary_pair(ref, c_base, scale, p):
        half = ROPE_DIMS // 2
        pairs = {n: proj(n, c_base + p * pair_w, pair_w) for n in ns}
        for n in ns:
            cos, sa, sb = cos_ref[parts[n], :], sa_ref[parts[n], :], sb_ref[parts[n], :]
            for h in (2 * p, 2 * p + 1):
                xr = pairs[n][:, (h % 2) * HEAD_DIM:(h % 2 + 1) * HEAD_DIM]
                rot = (xr * cos + pltpu.roll(xr, LANES - half, 1) * sa
                       + pltpu.roll(xr, half, 1) * sb)
                if scale is not None:
                    rot = rot * scale
                ref[parts[n], h * HEAD_DIM:(h + 1) * HEAD_DIM] = rot.astype(BF16)

    def gates():
        lane = lax.broadcasted_iota(jnp.int32, (IN_PART_ROWS, LANES), 1)
        bas = {n: proj(n, C_BA, LANES) for n in ns}
        for n in ns:
            xg = bas[n] + gp_ref[1:2, :]
            e = jnp.exp(-jnp.abs(xg))
            r = 1.0 / (1.0 + e)
            sig = jnp.where(xg >= 0.0, r, e * r)
            g = -jnp.exp(gp_ref[0:1, :]) * (jnp.maximum(xg, 0.0) + jnp.log1p(e))
            bg_ref[parts[n], :] = jnp.where(lane < GDN_HEADS, sig,
                                            jnp.where(lane < 2 * GDN_HEADS, g, 0.0))

    convs = [functools.partial(conv_pair, p) for p in range(3 * GDN_WIDTH // pair_w)]
    light = ([functools.partial(plain_pair, gz_ref, C_GZ, p) for p in range(GDN_WIDTH // pair_w)]
             + [functools.partial(plain_pair, mv_ref, C_MV, p) for p in range(MOBA_WIDTH // pair_w)]
             + [functools.partial(rotary_pair, mq_ref, C_MQ, MOBA_Q_SCALE, p)
                for p in range(MOBA_WIDTH // pair_w)]
             + [functools.partial(rotary_pair, mk_ref, C_MK, None, p)
                for p in range(MOBA_WIDTH // pair_w)]
             + [gates])
    for first in range(0, len(parts), IN_PARTS_TOGETHER):
        ns[:] = range(first, first + IN_PARTS_TOGETHER)
        done = 0
        for n, conv in enumerate(convs, start=1):
            conv()
            upto = len(light) * n // len(convs)
            for stage in light[done:upto]:
                stage()
            done = upto


def _in_proj(x, nw, w_in, layer, conv_w, gparams, cos_t, sa_t, sb_t, *, tm):
    B, T, D = x.shape
    tok = lambda width: pl.BlockSpec((None, tm, width), lambda b, i: (b, i, 0))
    const = lambda shape: pl.BlockSpec(shape, lambda b, i: (0,) * len(shape))
    table = pl.BlockSpec((tm, LANES), lambda b, i: (i, 0))
    o512 = jax.ShapeDtypeStruct((B, T, GDN_WIDTH), BF16)
    return pl.pallas_call(
        functools.partial(_in_proj_body, tm=tm),
        grid=(B, T // tm),
        in_specs=[tok(D), const((1, D)),
                  pl.BlockSpec((None,) + w_in.shape[1:], lambda b, i: (layer, 0, 0),
                               pipeline_mode=pl.Buffered(1)),
                  const((GDN_CONV, 3 * GDN_WIDTH)),
                  const((2, LANES)), table, table, table],
        out_specs=[tok(GDN_WIDTH)] * 4 + [tok(LANES)] + [tok(MOBA_WIDTH)] * 3,
        out_shape=[o512] * 4 + [jax.ShapeDtypeStruct((B, T, LANES), F32)] + [o512] * 3,
        scratch_shapes=[pltpu.VMEM((CONV_HALO, 3 * GDN_WIDTH), F32),
                        pltpu.VMEM((D, C_END), BF16)],
        compiler_params=pltpu.CompilerParams(
            dimension_semantics=("arbitrary", "arbitrary"), vmem_limit_bytes=VMEM_LIMIT),
        name="in_proj",
    )(x, nw, w_in, conv_w, gparams, cos_t, sa_t, sb_t)


def _gdn_body(q_ref, k_ref, v_ref, z_ref, bg_ref, nw_ref, o_ref,
              s_ref, mneg_ref, n_ref, o1_ref, o2_ref, a_ref, snap_ref, *, T):
    C = GDN_CHUNK
    H = GDN_HEADS
    G = GDN_GROUP
    rows_per_trip = G * C
    ntrips = T // rows_per_trip
    ri = lax.broadcasted_iota(jnp.int32, (C, C), 0)
    ci = lax.broadcasted_iota(jnp.int32, (C, C), 1)
    causal = ci <= ri
    strict = ci < ri
    eye = (ri == ci).astype(F32)
    cols = [slice(h * HEAD_DIM, (h + 1) * HEAD_DIM) for h in range(H)]
    s_ref[...] = jnp.zeros_like(s_ref)

    def phase_a(row0, slot, chunks):
        inst = [(c, h) for c in chunks for h in range(H)]
        rng = range(len(inst))
        r0s = {c: _aligned(row0 + c * C, C) for c in chunks}
        bgs = {c: bg_ref[pl.ds(r0s[c], C), :] for c in chunks}
        q = [q_ref[pl.ds(r0s[c], C), cols[h]].astype(F32) for c, h in inst]
        k = [k_ref[pl.ds(r0s[c], C), cols[h]].astype(F32) for c, h in inst]
        v = [v_ref[pl.ds(r0s[c], C), cols[h]].astype(F32) for c, h in inst]
        beta = [bgs[c][:, h:h + 1] for c, h in inst]
        gc_col, gc_last, decay = [], [], []
        for c, h in inst:
            gb = jnp.broadcast_to(bgs[c][:, H + h:H + h + 1], (C, C))
            row = jnp.sum(jnp.where(ri <= ci, gb, 0.0), axis=0, keepdims=True)
            g_row = jnp.sum(jnp.where(ri == ci, gb, 0.0), axis=0, keepdims=True)
            col = jnp.sum(jnp.where(causal, jnp.broadcast_to(g_row, (C, C)), 0.0),
                          axis=1, keepdims=True)
            gc_col.append(col)
            gc_last.append(row[:, C - 1:C])
            decay.append(jnp.exp(jnp.where(causal, col - row, -jnp.inf)))
        kb = [k[t] * beta[t] for t in rng]
        eg = [jnp.exp(gc_col[t]) for t in rng]
        kq = [_dot_nt(jnp.concatenate([kb[t], q[t]], axis=0), k[t]) for t in rng]
        yield
        low = [jnp.where(strict, kq[t][:C] * decay[t], 0.0) for t in rng]
        intra = [jnp.where(causal, kq[t][C:] * decay[t], 0.0) for t in rng]
        inv = [eye - low[t] for t in rng]
        pw = [_dot(low[t], low[t]) for t in rng]
        yield
        for _ in range(4):
            r = [_dot(jnp.concatenate([pw[t], inv[t]], axis=0), pw[t]) for t in rng]
            inv = [inv[t] + r[t][C:] for t in rng]
            pw = [r[t][:C] for t in rng]
            yield
        inv = [inv[t] + _dot(inv[t], pw[t]) for t in rng]
        yield
        wu = [_dot(inv[t], jnp.concatenate([kb[t] * eg[t], v[t] * beta[t]], axis=-1))
              for t in rng]
        yield
        kd = [k[t] * jnp.exp(gc_last[t] - gc_col[t]) for t in rng]
        r = [_dot(jnp.concatenate([kd[t].T, intra[t]], axis=0), wu[t]) for t in rng]
        mn = [r[t][:HEAD_DIM] for t in rng]
        io = [r[t][HEAD_DIM:] for t in rng]
        for t, (c, h) in enumerate(inst):
            rows = slice(c * C, (c + 1) * C)
            mneg_ref[slot, c, h] = (-mn[t][:, :HEAD_DIM]).astype(BF16)
            n_ref[slot, c, h] = mn[t][:, HEAD_DIM:]
            o1_ref[slot, rows, cols[h]] = (q[t] * eg[t] - io[t][:, :HEAD_DIM]).astype(BF16)
            o2_ref[slot, rows, cols[h]] = io[t][:, HEAD_DIM:]
            a_ref[slot, c, h] = jnp.broadcast_to(jnp.exp(gc_last[t]), (SUBLANES, HEAD_DIM))
        yield

    def phase_b_step(slot, c):
        hs = range(H)
        state = [s_ref[h] for h in hs]
        sb = [state[h].astype(BF16) for h in hs]
        ms = [jnp.dot(mneg_ref[slot, c, h], sb[h], preferred_element_type=F32) for h in hs]
        for h in hs:
            snap_ref[slot, c, h] = sb[h]
            s_ref[h] = state[h] * a_ref[slot, c, h, 0:1, :] + ms[h] + n_ref[slot, c, h]

    def phase_c(row0, slot, chunks):
        inst = [(c, h) for c in chunks for h in range(H)]
        os_ = [jnp.dot(o1_ref[slot, c * C:(c + 1) * C, cols[h]], snap_ref[slot, c, h],
                       preferred_element_type=F32) for c, h in inst]
        for t, (c, h) in enumerate(inst):
            r0 = _aligned(row0 + c * C, C)
            o = os_[t] + o2_ref[slot, c * C:(c + 1) * C, cols[h]]
            z = z_ref[pl.ds(r0, C), cols[h]].astype(F32)
            o_ref[pl.ds(r0, C), cols[h]] = (_rms(o, nw_ref[...])
                                            * (z * _sigmoid(z))).astype(o_ref.dtype)

    def trip(g, do_a, do_b, do_c):
        row0 = lambda d: _aligned((g - d) * rows_per_trip, rows_per_trip)
        slot = lambda d: (g - d) % GDN_RING
        c_steps = ([functools.partial(phase_c, row0(2), slot(2), (c,)) for c in range(G)]
                   if do_c else [])
        b_steps = [functools.partial(phase_b_step, slot(1), c) for c in range(G)] if do_b else []
        if do_a:
            per_wave = G // GDN_WAVES
            waves = [phase_a(row0(0), slot(0), tuple(range(w * per_wave, (w + 1) * per_wave)))
                     for w in range(GDN_WAVES)]
            n = 0
            while any(w is not None for w in waves):
                for i, w in enumerate(waves):
                    if w is not None and n >= i * GDN_WAVE_LAG:
                        if next(w, "done") == "done":
                            waves[i] = None
                if n % 2 == 0 and c_steps:
                    c_steps.pop(0)()
                if n % 2 == 1 and b_steps:
                    b_steps.pop(0)()
                n += 1
        while b_steps or c_steps:
            for steps in (b_steps, c_steps):
                if steps:
                    steps.pop(0)()

    def steady(g, carry):
        trip(g, True, True, True)
        return carry

    all_on = lambda g: g < ntrips and 1 <= g <= ntrips and 2 <= g <= ntrips + 1
    g = 0
    while g < ntrips + 2:
        if all_on(g):
            last = g
            while all_on(last + 1):
                last += 1
            lax.fori_loop(g, last + 1, steady, 0)
            g = last + 1
        else:
            trip(g, g < ntrips, 1 <= g <= ntrips, 2 <= g <= ntrips + 1)
            g += 1


def _gdn(gq, gk, gv, gz, bg, nw):
    B, T, _ = gq.shape
    rows_per_trip = GDN_GROUP * GDN_CHUNK
    assert T % rows_per_trip == 0
    tok = lambda width: pl.BlockSpec((None, T, width), lambda b: (b, 0, 0))
    per_chunk = (GDN_RING, GDN_GROUP, GDN_HEADS)
    return pl.pallas_call(
        functools.partial(_gdn_body, T=T),
        grid=(B,),
        in_specs=[tok(GDN_WIDTH)] * 4 + [tok(LANES), pl.BlockSpec((1, HEAD_DIM), lambda b: (0, 0))],
        out_specs=tok(GDN_WIDTH),
        out_shape=jax.ShapeDtypeStruct((B, T, GDN_WIDTH), BF16),
        scratch_shapes=[
            pltpu.VMEM((GDN_HEADS, HEAD_DIM, HEAD_DIM), F32),
            pltpu.VMEM(per_chunk + (HEAD_DIM, HEAD_DIM), BF16),
            pltpu.VMEM(per_chunk + (HEAD_DIM, HEAD_DIM), F32),
            pltpu.VMEM((GDN_RING, rows_per_trip, GDN_WIDTH), BF16),
            pltpu.VMEM((GDN_RING, rows_per_trip, GDN_WIDTH), F32),
            pltpu.VMEM(per_chunk + (SUBLANES, HEAD_DIM), F32),
            pltpu.VMEM(per_chunk + (HEAD_DIM, HEAD_DIM), BF16),
        ],
        compiler_params=pltpu.CompilerParams(
            dimension_semantics=("arbitrary",), vmem_limit_bytes=VMEM_LIMIT),
        name="gdn",
    )(gq, gk, gv, gz, bg, nw)


def _moba_body(q_ref, k_ref, v_ref, o_ref, vt_ref, *, T):
    BS = MOBA_BLOCK
    nb = T // BS
    nrow = -(-nb // SUBLANES) * SUBLANES
    neg = -jnp.inf
    heads = [slice(h * HEAD_DIM, (h + 1) * HEAD_DIM) for h in range(MOBA_HEADS_PER_STEP)]
    blk = lax.broadcasted_iota(jnp.int32, (nrow, BS), 0)
    key = lax.broadcasted_iota(jnp.int32, (BS, BS), 0)
    qry = lax.broadcasted_iota(jnp.int32, (BS, BS), 1)

    kmean = []
    for n, hd in enumerate(heads):
        means = [jnp.mean(k_ref[j * BS:(j + 1) * BS, hd].astype(F32), axis=0, keepdims=True)
                 for j in range(nb)]
        if nrow > nb:
            means.append(jnp.zeros((nrow - nb, HEAD_DIM), F32))
        kmean.append(jnp.concatenate(means, axis=0))
        for j in range(nb):
            vt_ref[n, 0:HEAD_DIM, j * BS:(j + 1) * BS] = v_ref[j * BS:(j + 1) * BS, hd].T
        vt_ref[n, HEAD_DIM:, :] = jnp.ones((BF16_ROWS, T), BF16)

    def block_scores(n, i):
        q = q_ref[i * BS:(i + 1) * BS, heads[n]]
        sel = None
        if i > MOBA_TOPK:
            gate = lax.dot_general(kmean[n], q.astype(F32), (((1,), (1,)), ((), ())),
                                   precision=lax.Precision.HIGHEST, preferred_element_type=F32)
            gate = jnp.where(blk < i, gate, neg)
            rank = jnp.zeros((nrow, BS), F32)
            for jp in range(i):
                gj = gate[jp:jp + 1, :]
                beats = (gj > gate) | ((gj == gate) & (blk > jp))
                rank = rank + jnp.where(beats, 1.0, 0.0)
            sel = jnp.where((rank < MOBA_TOPK) & (blk < i), 1.0, 0.0)
        scores = []
        for j in range(i + 1):
            s = _dot_nt(k_ref[j * BS:(j + 1) * BS, heads[n]], q)
            if j == i:
                s = jnp.where(key <= qry, s, neg)
            elif sel is not None:
                s = jnp.where(sel[j:j + 1, :] > 0.5, s, neg)
            scores.append(s)
        return scores

    def block_output(n, i, scores):
        m = scores[0].max(axis=0, keepdims=True)
        for s in scores[1:]:
            m = jnp.maximum(m, s.max(axis=0, keepdims=True))
        acc = jnp.zeros((HEAD_DIM + BF16_ROWS, BS), F32)
        for j, s in enumerate(scores):
            acc = acc + _dot(vt_ref[n, :, j * BS:(j + 1) * BS], jnp.exp2(s - m))
        out = acc[:HEAD_DIM, :] / acc[HEAD_DIM:HEAD_DIM + 1, :]
        o_ref[i * BS:(i + 1) * BS, heads[n]] = out.T.astype(o_ref.dtype)

    tasks = [(n, i) for i in range(nb) for n in range(MOBA_HEADS_PER_STEP)]
    pending = {t: block_scores(*tasks[t]) for t in range(min(MOBA_LOOKAHEAD, len(tasks)))}
    for t, (n, i) in enumerate(tasks):
        if t + MOBA_LOOKAHEAD < len(tasks):
            pending[t + MOBA_LOOKAHEAD] = block_scores(*tasks[t + MOBA_LOOKAHEAD])
        block_output(n, i, pending.pop(t))


def _moba(mq, mk, mv):
    B, T, _ = mq.shape
    assert T % MOBA_BLOCK == 0 and MOBA_HEADS % MOBA_HEADS_PER_STEP == 0
    width = MOBA_HEADS_PER_STEP * HEAD_DIM
    head = pl.BlockSpec((None, T, width), lambda b, h: (b, 0, h))
    return pl.pallas_call(
        functools.partial(_moba_body, T=T),
        grid=(B, MOBA_HEADS // MOBA_HEADS_PER_STEP),
        in_specs=[head] * 3,
        out_specs=head,
        out_shape=jax.ShapeDtypeStruct((B, T, MOBA_WIDTH), BF16),
        scratch_shapes=[pltpu.VMEM((MOBA_HEADS_PER_STEP, HEAD_DIM + BF16_ROWS, T), BF16)],
        compiler_params=pltpu.CompilerParams(
            dimension_semantics=("arbitrary", "arbitrary"), vmem_limit_bytes=VMEM_LIMIT),
        name="moba",
    )(mq, mk, mv)


def _mlp_body(og_ref, om_ref, x_ref, p_ref, wo_ref, n1_ref, n2_ref, wu_ref, wd_ref, n3_ref,
              wg_ref, wp_ref, o_ref, *, ff_chunk):
    parts = [slice(r0, r0 + MLP_PART_ROWS) for r0 in range(0, x_ref.shape[0], MLP_PART_ROWS)]
    ns = range(len(parts))
    mix = [_dot(og_ref[r, :], wo_ref[0:GDN_WIDTH, :])
           + _dot(om_ref[r, :], wo_ref[GDN_WIDTH:GDN_WIDTH + MOBA_WIDTH, :]) for r in parts]
    h = [x_ref[r, :] + _rms(mix[n], n1_ref[...]) for n, r in enumerate(parts)]
    a = [_rms(h[n], n2_ref[...]).astype(BF16) for n in ns]
    f = [None for _ in ns]
    for c0 in range(0, D_FF, ff_chunk):
        up = [jnp.dot(a[n], wu_ref[:, c0:c0 + ff_chunk], preferred_element_type=F32)
              for n in ns]
        for n in ns:
            part = _dot(jnp.square(jnp.maximum(up[n], 0.0)), wd_ref[c0:c0 + ff_chunk, :])
            f[n] = part if f[n] is None else f[n] + part
    h = [h[n] + _rms(f[n], n3_ref[...]) for n in ns]
    gate = [_dot(h[n], wg_ref[...]) for n in ns]
    ple = [_dot(p_ref[r, :], wp_ref[...]) for r in parts]
    for n, r in enumerate(parts):
        o_ref[r, :] = h[n] + _sigmoid(gate[n]) * ple[n]


def _mlp(og, om, x, p, wo, n1, n2, wu, wd, n3, wg, wp, *, tm):
    B, T, D = x.shape
    tok = lambda width: pl.BlockSpec((None, tm, width), lambda b, i: (b, i, 0))
    const = lambda shape: pl.BlockSpec(shape, lambda b, i: (0,) * len(shape),
                                       pipeline_mode=pl.Buffered(1))
    return pl.pallas_call(
        functools.partial(_mlp_body, ff_chunk=1024),
        grid=(B, T // tm),
        in_specs=[tok(GDN_WIDTH), tok(MOBA_WIDTH), tok(D), tok(PLE_DIM),
                  const((D, D)), const((1, D)), const((1, D)), const((D, D_FF)),
                  const((D_FF, D)), const((1, D)), const((D, D)), const((PLE_DIM, D))],
        out_specs=tok(D),
        out_shape=jax.ShapeDtypeStruct((B, T, D), F32),
        compiler_params=pltpu.CompilerParams(
            dimension_semantics=("arbitrary", "arbitrary"), vmem_limit_bytes=VMEM_LIMIT),
        name="mlp",
    )(og, om, x, p, wo, n1, n2, wu, wd, n3, wg, wp)


def _rope_tables(T):
    half = ROPE_DIMS // 2
    inv_freq = ROPE_THETA ** (-np.arange(half, dtype=np.float64) * (2.0 / ROPE_DIMS))
    ang = np.arange(T, dtype=np.float64)[:, None] * inv_freq[None, :]
    cos, sin = np.cos(ang), np.sin(ang)
    zeros = np.zeros((T, HEAD_DIM - ROPE_DIMS))
    z_half = np.zeros((T, half))
    cos_t = np.concatenate([cos, cos, np.ones((T, HEAD_DIM - ROPE_DIMS))], axis=-1)
    sa_t = np.concatenate([-sin, z_half, zeros], axis=-1)
    sb_t = np.concatenate([z_half, sin, zeros], axis=-1)
    cos_t, sa_t, sb_t = (jnp.asarray(t, dtype=F32) for t in (cos_t, sa_t, sb_t))
    return cos_t, sa_t, sb_t


def _layer(h, p_i, w_in, layer, conv_w, a_log, dt_bias, gdn_norm_w, w_out, attn_pre_norm,
           attn_post_norm, mlp_pre_norm, mlp_post_norm, w_up, w_down, w_ple, w_ple_gate, tables):
    B, T, D = h.shape
    tm = next(t for t in (512, MOBA_BLOCK) if T % t == 0)
    assert T % tm == 0 and T % MOBA_BLOCK == 0 and T % GDN_CHUNK == 0
    nh = 2 * GDN_HEADS
    gparams = jnp.zeros((2, LANES), F32)
    gparams = gparams.at[0, GDN_HEADS:nh].set(a_log.astype(F32))
    gparams = gparams.at[1, GDN_HEADS:nh].set(dt_bias.astype(F32))
    row = lambda v: v.reshape(1, -1).astype(F32)

    gq, gk, gv, gz, bg, mq, mk, mv = _in_proj(
        h, row(attn_pre_norm), jnp.swapaxes(w_in.astype(F32), 1, 2), layer, conv_w.astype(F32),
        gparams, *tables,
        tm=tm)
    o_gdn = _gdn(gq, gk, gv, gz, bg, row(gdn_norm_w))
    o_moba = _moba(mq, mk, mv)
    return _mlp(o_gdn, o_moba, h, p_i, w_out.astype(BF16), row(attn_post_norm),
                row(mlp_pre_norm), w_up.astype(BF16), w_down.astype(BF16),
                row(mlp_post_norm), w_ple_gate.astype(BF16), w_ple.astype(BF16), tm=tm)


def kernel(x, p, w_in, conv_w, a_log, dt_bias, gdn_norm_w, w_out, attn_pre_norm, attn_post_norm,
           mlp_pre_norm, mlp_post_norm, w_up, w_down, w_ple, w_ple_gate):
    tables = _rope_tables(x.shape[1])
    h = x
    for i in range(w_in.shape[0]):
        h = _layer(h, p[i], w_in, i, conv_w[i], a_log[i], dt_bias[i], gdn_norm_w[i], w_out[i],
                   attn_pre_norm[i], attn_post_norm[i], mlp_pre_norm[i], mlp_post_norm[i],
                   w_up[i], w_down[i], w_ple[i], w_ple_gate[i], tables)
    return h
```

```python
import functools
import math

import jax
import jax.numpy as jnp
import numpy as np
from jax import lax
from jax.experimental import pallas as pl
from jax.experimental.pallas import tpu as pltpu

F32 = jnp.float32
BF16 = jnp.bfloat16

D_MODEL = 1024
HEAD_DIM = 128
GDN_HEADS = 4
GDN_WIDTH = GDN_HEADS * HEAD_DIM
GDN_CONV = 4
GDN_CHUNK = 64
GDN_GROUP = 4
GDN_WAVES = 2
GDN_WAVE_LAG = 1
GDN_RING = 3
MOBA_HEADS = 4
MOBA_WIDTH = MOBA_HEADS * HEAD_DIM
MOBA_BLOCK = 256
MOBA_TOPK = 3
MOBA_HEADS_PER_STEP = 2
MOBA_LOOKAHEAD = 2
ROPE_DIMS = 32
ROPE_THETA = 500000.0
D_FF = 4 * D_MODEL
PLE_DIM = 256
RMS_EPS = 1e-6
LANES = 128
SUBLANES = 8
BF16_ROWS = 16
CONV_HALO = 8
REPACK_ROWS = 128
MLP_PART_ROWS = 256
IN_PART_ROWS = 128
IN_PARTS_TOGETHER = 1
QK_SCALE = HEAD_DIM ** -0.5
MOBA_Q_SCALE = QK_SCALE * math.log2(math.e)
VMEM_LIMIT = 56 * 1024 * 1024

C_GQKV = 0
C_GZ = 3 * GDN_WIDTH
C_MQ = 4 * GDN_WIDTH
C_MK = C_MQ + MOBA_WIDTH
C_MV = C_MK + MOBA_WIDTH
C_BA = C_MV + MOBA_WIDTH
C_END = C_BA + LANES


def _rms(x, w):
    return x * lax.rsqrt(jnp.mean(x * x, axis=-1, keepdims=True) + RMS_EPS) * w


def _sigmoid(x):
    return 1.0 / (1.0 + jnp.exp(-x))


def _aligned(x, m):
    return x if isinstance(x, int) else pl.multiple_of(x, m)


def _dot(a, b):
    return jnp.dot(a.astype(BF16), b.astype(BF16), preferred_element_type=F32)


def _dot_nt(a, b):
    return lax.dot_general(a.astype(BF16), b.astype(BF16), (((1,), (1,)), ((), ())),
                           preferred_element_type=F32)


def _in_proj_body(x_ref, nw_ref, win_ref, cw_ref, gp_ref, cos_ref, sa_ref, sb_ref,
                  gq_ref, gk_ref, gv_ref, gz_ref, bg_ref, mq_ref, mk_ref, mv_ref,
                  halo, w_ref, *, tm):
    i = pl.program_id(1)

    @pl.when((pl.program_id(0) == 0) & (i == 0))
    def _():
        n_gate = 2 * GDN_HEADS
        for c0 in range(0, C_BA, REPACK_ROWS):
            src = c0 if c0 < C_MQ else c0 + n_gate
            w_ref[:, c0:c0 + REPACK_ROWS] = win_ref[src:src + REPACK_ROWS, :].T.astype(BF16)
        gate_rows = jnp.concatenate([win_ref[C_MQ:C_MQ + n_gate, :],
                                     jnp.zeros((LANES - n_gate, D_MODEL), F32)], axis=0)
        w_ref[:, C_BA:C_END] = gate_rows.T.astype(BF16)

    parts = [slice(r0, r0 + IN_PART_ROWS) for r0 in range(0, tm, IN_PART_ROWS)]
    ns = []
    u = [None for _ in parts]

    def proj(n, c0, width):
        if u[n] is None:
            u[n] = _rms(x_ref[parts[n], :], nw_ref[...]).astype(BF16)
        return jnp.dot(u[n], w_ref[:, c0:c0 + width], preferred_element_type=F32)

    @pl.when(i == 0)
    def _():
        halo[...] = jnp.zeros_like(halo)

    pair_w = 2 * HEAD_DIM

    def conv_pair(p):
        pairs = {n: proj(n, C_GQKV + p * pair_w, pair_w) for n in ns}
        for s in (2 * p, 2 * p + 1):
            c0 = s * HEAD_DIM
            cw = [cw_ref[j:j + 1, c0:c0 + HEAD_DIM] for j in range(GDN_CONV)]
            which, h = divmod(s, GDN_HEADS)
            cur = {n: pairs[n][:, (s % 2) * HEAD_DIM:(s % 2 + 1) * HEAD_DIM] for n in ns}
            prev = halo[:, c0:c0 + HEAD_DIM]
            halo[:, c0:c0 + HEAD_DIM] = cur[ns[-1]][IN_PART_ROWS - CONV_HALO:, :]
            for n in ns:
                xs = jnp.concatenate([prev, cur[n]], axis=0)
                prev = cur[n][IN_PART_ROWS - CONV_HALO:, :]
                x1 = pltpu.roll(xs, 1, 0)
                acc = (cw[3] * xs + cw[2] * x1) + pltpu.roll(cw[1] * xs + cw[0] * x1, 2, 0)
                acc = acc[CONV_HALO:, :]
                y = acc * _sigmoid(acc)
                if which < 2:
                    inv_norm = lax.rsqrt(jnp.sum(y * y, axis=-1, keepdims=True) + RMS_EPS)
                    y = y * (inv_norm * QK_SCALE if which == 0 else inv_norm)
                (gq_ref, gk_ref, gv_ref)[which][parts[n], h * HEAD_DIM:(h + 1) * HEAD_DIM] = (
                    y.astype(BF16))

    def plain_pair(ref, c_base, p):
        vals = {n: proj(n, c_base + p * pair_w, pair_w) for n in ns}
        for n in ns:
            ref[parts[n], p * pair_w:(p + 1) * pair_w] = vals[n].astype(BF16)

    def rotary_pair(ref, c_base, scale, p):
        half = ROPE_DIMS // 2
        pairs = {n: proj(n, c_base + p * pair_w, pair_w) for n in ns}
        for n in ns:
            cos, sa, sb = cos_ref[parts[n], :], sa_ref[parts[n], :], sb_ref[parts[n], :]
            for h in (2 * p, 2 * p + 1):
                xr = pairs[n][:, (h % 2) * HEAD_DIM:(h % 2 + 1) * HEAD_DIM]
                rot = (xr * cos + pltpu.roll(xr, LANES - half, 1) * sa
                       + pltpu.roll(xr, half, 1) * sb)
                if scale is not None:
                    rot = rot * scale
                ref[parts[n], h * HEAD_DIM:(h + 1) * HEAD_DIM] = rot.astype(BF16)

    def gates():
        lane = lax.broadcasted_iota(jnp.int32, (IN_PART_ROWS, LANES), 1)
        bas = {n: proj(n, C_BA, LANES) for n in ns}
        for n in ns:
            xg = bas[n] + gp_ref[1:2, :]
            e = jnp.exp(-jnp.abs(xg))
            r = 1.0 / (1.0 + e)
            sig = jnp.where(xg >= 0.0, r, e * r)
            g = -jnp.exp(gp_ref[0:1, :]) * (jnp.maximum(xg, 0.0) + jnp.log1p(e))
            bg_ref[parts[n], :] = jnp.where(lane < GDN_HEADS, sig,
                                            jnp.where(lane < 2 * GDN_HEADS, g, 0.0))

    convs = [functools.partial(conv_pair, p) for p in range(3 * GDN_WIDTH // pair_w)]
    light = ([functools.partial(plain_pair, gz_ref, C_GZ, p) for p in range(GDN_WIDTH // pair_w)]
             + [functools.partial(plain_pair, mv_ref, C_MV, p) for p in range(MOBA_WIDTH // pair_w)]
             + [functools.partial(rotary_pair, mq_ref, C_MQ, MOBA_Q_SCALE, p)
                for p in range(MOBA_WIDTH // pair_w)]
             + [functools.partial(rotary_pair, mk_ref, C_MK, None, p)
                for p in range(MOBA_WIDTH // pair_w)]
             + [gates])
    for first in range(0, len(parts), IN_PARTS_TOGETHER):
        ns[:] = range(first, first + IN_PARTS_TOGETHER)
        done = 0
        for n, conv in enumerate(convs, start=1):
            conv()
            upto = len(light) * n // len(convs)
            for stage in light[done:upto]:
                stage()
            done = upto


def _in_proj(x, nw, w_in, layer, conv_w, gparams, cos_t, sa_t, sb_t, *, tm):
    B, T, D = x.shape
    tok = lambda width: pl.BlockSpec((None, tm, width), lambda b, i: (b, i, 0))
    const = lambda shape: pl.BlockSpec(shape, lambda b, i: (0,) * len(shape))
    table = pl.BlockSpec((tm, LANES), lambda b, i: (i, 0))
    o512 = jax.ShapeDtypeStruct((B, T, GDN_WIDTH), BF16)
    return pl.pallas_call(
        functools.partial(_in_proj_body, tm=tm),
        grid=(B, T // tm),
        in_specs=[tok(D), const((1, D)),
                  pl.BlockSpec((None,) + w_in.shape[1:], lambda b, i: (layer, 0, 0),
                               pipeline_mode=pl.Buffered(1)),
                  const((GDN_CONV, 3 * GDN_WIDTH)),
                  const((2, LANES)), table, table, table],
        out_specs=[tok(GDN_WIDTH)] * 4 + [tok(LANES)] + [tok(MOBA_WIDTH)] * 3,
        out_shape=[o512] * 4 + [jax.ShapeDtypeStruct((B, T, LANES), F32)] + [o512] * 3,
        scratch_shapes=[pltpu.VMEM((CONV_HALO, 3 * GDN_WIDTH), F32),
                        pltpu.VMEM((D, C_END), BF16)],
        compiler_params=pltpu.CompilerParams(
            dimension_semantics=("arbitrary", "arbitrary"), vmem_limit_bytes=VMEM_LIMIT),
        name="in_proj",
    )(x, nw, w_in, conv_w, gparams, cos_t, sa_t, sb_t)


def _gdn_body(q_ref, k_ref, v_ref, z_ref, bg_ref, nw_ref, o_ref,
              s_ref, mneg_ref, n_ref, o1_ref, o2_ref, a_ref, snap_ref, *, T):
    C = GDN_CHUNK
    H = GDN_HEADS
    G = GDN_GROUP
    rows_per_trip = G * C
    ntrips = T // rows_per_trip
    ri = lax.broadcasted_iota(jnp.int32, (C, C), 0)
    ci = lax.broadcasted_iota(jnp.int32, (C, C), 1)
    causal = ci <= ri
    strict = ci < ri
    eye = (ri == ci).astype(F32)
    cols = [slice(h * HEAD_DIM, (h + 1) * HEAD_DIM) for h in range(H)]
    s_ref[...] = jnp.zeros_like(s_ref)

    def phase_a(row0, slot, chunks):
        inst = [(c, h) for c in chunks for h in range(H)]
        rng = range(len(inst))
        r0s = {c: _aligned(row0 + c * C, C) for c in chunks}
        bgs = {c: bg_ref[pl.ds(r0s[c], C), :] for c in chunks}
        q = [q_ref[pl.ds(r0s[c], C), cols[h]].astype(F32) for c, h in inst]
        k = [k_ref[pl.ds(r0s[c], C), cols[h]].astype(F32) for c, h in inst]
        v = [v_ref[pl.ds(r0s[c], C), cols[h]].astype(F32) for c, h in inst]
        beta = [bgs[c][:, h:h + 1] for c, h in inst]
        gc_col, gc_last, decay = [], [], []
        for c, h in inst:
            gb = jnp.broadcast_to(bgs[c][:, H + h:H + h + 1], (C, C))
            row = jnp.sum(jnp.where(ri <= ci, gb, 0.0), axis=0, keepdims=True)
            g_row = jnp.sum(jnp.where(ri == ci, gb, 0.0), axis=0, keepdims=True)
            col = jnp.sum(jnp.where(causal, jnp.broadcast_to(g_row, (C, C)), 0.0),
                          axis=1, keepdims=True)
            gc_col.append(col)
            gc_last.append(row[:, C - 1:C])
            decay.append(jnp.exp(jnp.where(causal, col - row, -jnp.inf)))
        kb = [k[t] * beta[t] for t in rng]
        eg = [jnp.exp(gc_col[t]) for t in rng]
        kq = [_dot_nt(jnp.concatenate([kb[t], q[t]], axis=0), k[t]) for t in rng]
        yield
        low = [jnp.where(strict, kq[t][:C] * decay[t], 0.0) for t in rng]
        intra = [jnp.where(causal, kq[t][C:] * decay[t], 0.0) for t in rng]
        inv = [eye - low[t] for t in rng]
        pw = [_dot(low[t], low[t]) for t in rng]
        yield
        for _ in range(4):
            r = [_dot(jnp.concatenate([pw[t], inv[t]], axis=0), pw[t]) for t in rng]
            inv = [inv[t] + r[t][C:] for t in rng]
            pw = [r[t][:C] for t in rng]
            yield
        inv = [inv[t] + _dot(inv[t], pw[t]) for t in rng]
        yield
        wu = [_dot(inv[t], jnp.concatenate([kb[t] * eg[t], v[t] * beta[t]], axis=-1))
              for t in rng]
        yield
        kd = [k[t] * jnp.exp(gc_last[t] - gc_col[t]) for t in rng]
        r = [_dot(jnp.concatenate([kd[t].T, intra[t]], axis=0), wu[t]) for t in rng]
        mn = [r[t][:HEAD_DIM] for t in rng]
        io = [r[t][HEAD_DIM:] for t in rng]
        for t, (c, h) in enumerate(inst):
            rows = slice(c * C, (c + 1) * C)
            mneg_ref[slot, c, h] = (-mn[t][:, :HEAD_DIM]).astype(BF16)
            n_ref[slot, c, h] = mn[t][:, HEAD_DIM:]
            o1_ref[slot, rows, cols[h]] = (q[t] * eg[t] - io[t][:, :HEAD_DIM]).astype(BF16)
            o2_ref[slot, rows, cols[h]] = io[t][:, HEAD_DIM:]
            a_ref[slot, c, h] = jnp.broadcast_to(jnp.exp(gc_last[t]), (SUBLANES, HEAD_DIM))
        yield

    def phase_b_step(slot, c):
        hs = range(H)
        state = [s_ref[h] for h in hs]
        sb = [state[h].astype(BF16) for h in hs]
        ms = [jnp.dot(mneg_ref[slot, c, h], sb[h], preferred_element_type=F32) for h in hs]
        for h in hs:
            snap_ref[slot, c, h] = sb[h]
            s_ref[h] = state[h] * a_ref[slot, c, h, 0:1, :] + ms[h] + n_ref[slot, c, h]

    def phase_c(row0, slot, chunks):
        inst = [(c, h) for c in chunks for h in range(H)]
        os_ = [jnp.dot(o1_ref[slot, c * C:(c + 1) * C, cols[h]], snap_ref[slot, c, h],
                       preferred_element_type=F32) for c, h in inst]
        for t, (c, h) in enumerate(inst):
            r0 = _aligned(row0 + c * C, C)
            o = os_[t] + o2_ref[slot, c * C:(c + 1) * C, cols[h]]
            z = z_ref[pl.ds(r0, C), cols[h]].astype(F32)
            o_ref[pl.ds(r0, C), cols[h]] = (_rms(o, nw_ref[...])
                                            * (z * _sigmoid(z))).astype(o_ref.dtype)

    def trip(g, do_a, do_b, do_c):
        row0 = lambda d: _aligned((g - d) * rows_per_trip, rows_per_trip)
        slot = lambda d: (g - d) % GDN_RING
        c_steps = ([functools.partial(phase_c, row0(2), slot(2), (c,)) for c in range(G)]
                   if do_c else [])
        b_steps = [functools.partial(phase_b_step, slot(1), c) for c in range(G)] if do_b else []
        if do_a:
            per_wave = G // GDN_WAVES
            waves = [phase_a(row0(0), slot(0), tuple(range(w * per_wave, (w + 1) * per_wave)))
                     for w in range(GDN_WAVES)]
            n = 0
            while any(w is not None for w in waves):
                for i, w in enumerate(waves):
                    if w is not None and n >= i * GDN_WAVE_LAG:
                        if next(w, "done") == "done":
                            waves[i] = None
                if n % 2 == 0 and c_steps:
                    c_steps.pop(0)()
                if n % 2 == 1 and b_steps:
                    b_steps.pop(0)()
                n += 1
        while b_steps or c_steps:
            for steps in (b_steps, c_steps):
                if steps:
                    steps.pop(0)()

    def steady(g, carry):
        trip(g, True, True, True)
        return carry

    all_on = lambda g: g < ntrips and 1 <= g <= ntrips and 2 <= g <= ntrips + 1
    g = 0
    while g < ntrips + 2:
        if all_on(g):
            last = g
            while all_on(last + 1):
                last += 1
            lax.fori_loop(g, last + 1, steady, 0)
            g = last + 1
        else:
            trip(g, g < ntrips, 1 <= g <= ntrips, 2 <= g <= ntrips + 1)
            g += 1


def _gdn(gq, gk, gv, gz, bg, nw):
    B, T, _ = gq.shape
    rows_per_trip = GDN_GROUP * GDN_CHUNK
    assert T % rows_per_trip == 0
    tok = lambda width: pl.BlockSpec((None, T, width), lambda b: (b, 0, 0))
    per_chunk = (GDN_RING, GDN_GROUP, GDN_HEADS)
    return pl.pallas_call(
        functools.partial(_gdn_body, T=T),
        grid=(B,),
        in_specs=[tok(GDN_WIDTH)] * 4 + [tok(LANES), pl.BlockSpec((1, HEAD_DIM), lambda b: (0, 0))],
        out_specs=tok(GDN_WIDTH),
        out_shape=jax.ShapeDtypeStruct((B, T, GDN_WIDTH), BF16),
        scratch_shapes=[
            pltpu.VMEM((GDN_HEADS, HEAD_DIM, HEAD_DIM), F32),
            pltpu.VMEM(per_chunk + (HEAD_DIM, HEAD_DIM), BF16),
            pltpu.VMEM(per_chunk + (HEAD_DIM, HEAD_DIM), F32),
            pltpu.VMEM((GDN_RING, rows_per_trip, GDN_WIDTH), BF16),
            pltpu.VMEM((GDN_RING, rows_per_trip, GDN_WIDTH), F32),
            pltpu.VMEM(per_chunk + (SUBLANES, HEAD_DIM), F32),
            pltpu.VMEM(per_chunk + (HEAD_DIM, HEAD_DIM), BF16),
        ],
        compiler_params=pltpu.CompilerParams(
            dimension_semantics=("arbitrary",), vmem_limit_bytes=VMEM_LIMIT),
        name="gdn",
    )(gq, gk, gv, gz, bg, nw)


def _moba_body(q_ref, k_ref, v_ref, o_ref, vt_ref, *, T):
    BS = MOBA_BLOCK
    nb = T // BS
    nrow = -(-nb // SUBLANES) * SUBLANES
    neg = -jnp.inf
    heads = [slice(h * HEAD_DIM, (h + 1) * HEAD_DIM) for h in range(MOBA_HEADS_PER_STEP)]
    blk = lax.broadcasted_iota(jnp.int32, (nrow, BS), 0)
    key = lax.broadcasted_iota(jnp.int32, (BS, BS), 0)
    qry = lax.broadcasted_iota(jnp.int32, (BS, BS), 1)

    kmean = []
    for n, hd in enumerate(heads):
        means = [jnp.mean(k_ref[j * BS:(j + 1) * BS, hd].astype(F32), axis=0, keepdims=True)
                 for j in range(nb)]
        if nrow > nb:
            means.append(jnp.zeros((nrow - nb, HEAD_DIM), F32))
        kmean.append(jnp.concatenate(means, axis=0))
        for j in range(nb):
            vt_ref[n, 0:HEAD_DIM, j * BS:(j + 1) * BS] = v_ref[j * BS:(j + 1) * BS, hd].T
        vt_ref[n, HEAD_DIM:, :] = jnp.ones((BF16_ROWS, T), BF16)

    def block_scores(n, i):
        q = q_ref[i * BS:(i + 1) * BS, heads[n]]
        sel = None
        if i > MOBA_TOPK:
            gate = lax.dot_general(kmean[n], q.astype(F32), (((1,), (1,)), ((), ())),
                                   precision=lax.Precision.HIGHEST, preferred_element_type=F32)
            gate = jnp.where(blk < i, gate, neg)
            rank = jnp.zeros((nrow, BS), F32)
            for jp in range(i):
                gj = gate[jp:jp + 1, :]
                beats = (gj > gate) | ((gj == gate) & (blk > jp))
                rank = rank + jnp.where(beats, 1.0, 0.0)
            sel = jnp.where((rank < MOBA_TOPK) & (blk < i), 1.0, 0.0)
        scores = []
        for j in range(i + 1):
            s = _dot_nt(k_ref[j * BS:(j + 1) * BS, heads[n]], q)
            if j == i:
                s = jnp.where(key <= qry, s, neg)
            elif sel is not None:
                s = jnp.where(sel[j:j + 1, :] > 0.5, s, neg)
            scores.append(s)
        return scores

    def block_output(n, i, scores):
        m = scores[0].max(axis=0, keepdims=True)
        for s in scores[1:]:
            m = jnp.maximum(m, s.max(axis=0, keepdims=True))
        acc = jnp.zeros((HEAD_DIM + BF16_ROWS, BS), F32)
        for j, s in enumerate(scores):
            acc = acc + _dot(vt_ref[n, :, j * BS:(j + 1) * BS], jnp.exp2(s - m))
        out = acc[:HEAD_DIM, :] / acc[HEAD_DIM:HEAD_DIM + 1, :]
        o_ref[i * BS:(i + 1) * BS, heads[n]] = out.T.astype(o_ref.dtype)

    tasks = [(n, i) for i in range(nb) for n in range(MOBA_HEADS_PER_STEP)]
    pending = {t: block_scores(*tasks[t]) for t in range(min(MOBA_LOOKAHEAD, len(tasks)))}
    for t, (n, i) in enumerate(tasks):
        if t + MOBA_LOOKAHEAD < len(tasks):
            pending[t + MOBA_LOOKAHEAD] = block_scores(*tasks[t + MOBA_LOOKAHEAD])
        block_output(n, i, pending.pop(t))


def _moba(mq, mk, mv):
    B, T, _ = mq.shape
    assert T % MOBA_BLOCK == 0 and MOBA_HEADS % MOBA_HEADS_PER_STEP == 0
    width = MOBA_HEADS_PER_STEP * HEAD_DIM
    head = pl.BlockSpec((None, T, width), lambda b, h: (b, 0, h))
    return pl.pallas_call(
        functools.partial(_moba_body, T=T),
        grid=(B, MOBA_HEADS // MOBA_HEADS_PER_STEP),
        in_specs=[head] * 3,
        out_specs=head,
        out_shape=jax.ShapeDtypeStruct((B, T, MOBA_WIDTH), BF16),
        scratch_shapes=[pltpu.VMEM((MOBA_HEADS_PER_STEP, HEAD_DIM + BF16_ROWS, T), BF16)],
        compiler_params=pltpu.CompilerParams(
            dimension_semantics=("arbitrary", "arbitrary"), vmem_limit_bytes=VMEM_LIMIT),
        name="moba",
    )(mq, mk, mv)


def _mlp_body(og_ref, om_ref, x_ref, p_ref, wo_ref, n1_ref, n2_ref, wu_ref, wd_ref, n3_ref,
              wg_ref, wp_ref, o_ref, *, ff_chunk):
    parts = [slice(r0, r0 + MLP_PART_ROWS) for r0 in range(0, x_ref.shape[0], MLP_PART_ROWS)]
    ns = range(len(parts))
    mix = [_dot(og_ref[r, :], wo_ref[0:GDN_WIDTH, :])
           + _dot(om_ref[r, :], wo_ref[GDN_WIDTH:GDN_WIDTH + MOBA_WIDTH, :]) for r in parts]
    h = [x_ref[r, :] + _rms(mix[n], n1_ref[...]) for n, r in enumerate(parts)]
    a = [_rms(h[n], n2_ref[...]).astype(BF16) for n in ns]
    f = [None for _ in ns]
    for c0 in range(0, D_FF, ff_chunk):
        up = [jnp.dot(a[n], wu_ref[:, c0:c0 + ff_chunk], preferred_element_type=F32)
              for n in ns]
        for n in ns:
            part = _dot(jnp.square(jnp.maximum(up[n], 0.0)), wd_ref[c0:c0 + ff_chunk, :])
            f[n] = part if f[n] is None else f[n] + part
    h = [h[n] + _rms(f[n], n3_ref[...]) for n in ns]
    gate = [_dot(h[n], wg_ref[...]) for n in ns]
    ple = [_dot(p_ref[r, :], wp_ref[...]) for r in parts]
    for n, r in enumerate(parts):
        o_ref[r, :] = h[n] + _sigmoid(gate[n]) * ple[n]


def _mlp(og, om, x, p, wo, n1, n2, wu, wd, n3, wg, wp, *, tm):
    B, T, D = x.shape
    tok = lambda width: pl.BlockSpec((None, tm, width), lambda b, i: (b, i, 0))
    const = lambda shape: pl.BlockSpec(shape, lambda b, i: (0,) * len(shape),
                                       pipeline_mode=pl.Buffered(1))
    return pl.pallas_call(
        functools.partial(_mlp_body, ff_chunk=1024),
        grid=(B, T // tm),
        in_specs=[tok(GDN_WIDTH), tok(MOBA_WIDTH), tok(D), tok(PLE_DIM),
                  const((D, D)), const((1, D)), const((1, D)), const((D, D_FF)),
                  const((D_FF, D)), const((1, D)), const((D, D)), const((PLE_DIM, D))],
        out_specs=tok(D),
        out_shape=jax.ShapeDtypeStruct((B, T, D), F32),
        compiler_params=pltpu.CompilerParams(
            dimension_semantics=("arbitrary", "arbitrary"), vmem_limit_bytes=VMEM_LIMIT),
        name="mlp",
    )(og, om, x, p, wo, n1, n2, wu, wd, n3, wg, wp)


def _rope_tables(T):
    half = ROPE_DIMS // 2
    inv_freq = ROPE_THETA ** (-np.arange(half, dtype=np.float64) * (2.0 / ROPE_DIMS))
    ang = np.arange(T, dtype=np.float64)[:, None] * inv_freq[None, :]
    cos, sin = np.cos(ang), np.sin(ang)
    zeros = np.zeros((T, HEAD_DIM - ROPE_DIMS))
    z_half = np.zeros((T, half))
    cos_t = np.concatenate([cos, cos, np.ones((T, HEAD_DIM - ROPE_DIMS))], axis=-1)
    sa_t = np.concatenate([-sin, z_half, zeros], axis=-1)
    sb_t = np.concatenate([z_half, sin, zeros], axis=-1)
    cos_t, sa_t, sb_t = (jnp.asarray(t, dtype=F32) for t in (cos_t, sa_t, sb_t))
    return cos_t, sa_t, sb_t


def _layer(h, p_i, w_in, layer, conv_w, a_log, dt_bias, gdn_norm_w, w_out, attn_pre_norm,
           attn_post_norm, mlp_pre_norm, mlp_post_norm, w_up, w_down, w_ple, w_ple_gate, tables):
    B, T, D = h.shape
    tm = next(t for t in (512, MOBA_BLOCK) if T % t == 0)
    assert T % tm == 0 and T % MOBA_BLOCK == 0 and T % GDN_CHUNK == 0
    nh = 2 * GDN_HEADS
    gparams = jnp.zeros((2, LANES), F32)
    gparams = gparams.at[0, GDN_HEADS:nh].set(a_log.astype(F32))
    gparams = gparams.at[1, GDN_HEADS:nh].set(dt_bias.astype(F32))
    row = lambda v: v.reshape(1, -1).astype(F32)

    gq, gk, gv, gz, bg, mq, mk, mv = _in_proj(
        h, row(attn_pre_norm), jnp.swapaxes(w_in.astype(F32), 1, 2), layer, conv_w.astype(F32),
        gparams, *tables,
        tm=next(t for t in (1024, 512, MOBA_BLOCK) if T % t == 0))
    o_gdn = _gdn(gq, gk, gv, gz, bg, row(gdn_norm_w))
    o_moba = _moba(mq, mk, mv)
    return _mlp(o_gdn, o_moba, h, p_i, w_out.astype(BF16), row(attn_post_norm),
                row(mlp_pre_norm), w_up.astype(BF16), w_down.astype(BF16),
                row(mlp_post_norm), w_ple_gate.astype(BF16), w_ple.astype(BF16), tm=tm)


def kernel(x, p, w_in, conv_w, a_log, dt_bias, gdn_norm_w, w_out, attn_pre_norm, attn_post_norm,
           mlp_pre_norm, mlp_post_norm, w_up, w_down, w_ple, w_ple_gate):
    tables = _rope_tables(x.shape[1])
    h = x
    for i in range(w_in.shape[0]):
        h = _layer(h, p[i], w_in, i, conv_w[i], a_log[i], dt_bias[i], gdn_norm_w[i], w_out[i],
                   attn_pre_norm[i], attn_post_norm[i], mlp_pre_norm[i], mlp_post_norm[i],
                   w_up[i], w_down[i], w_ple[i], w_ple_gate[i], tables)
    return h
```

```python
import functools
import math

import jax
import jax.numpy as jnp
import numpy as np
from jax import lax
from jax.experimental import pallas as pl
from jax.experimental.pallas import tpu as pltpu

F32 = jnp.float32
BF16 = jnp.bfloat16

D_MODEL = 1024
HEAD_DIM = 128
GDN_HEADS = 4
GDN_WIDTH = GDN_HEADS * HEAD_DIM
GDN_CONV = 4
GDN_CHUNK = 64
GDN_GROUP = 4
GDN_WAVES = 2
GDN_WAVE_LAG = 1
GDN_RING = 3
MOBA_HEADS = 4
MOBA_WIDTH = MOBA_HEADS * HEAD_DIM
MOBA_BLOCK = 256
MOBA_TOPK = 3
MOBA_HEADS_PER_STEP = 2
MOBA_LOOKAHEAD = 2
ROPE_DIMS = 32
ROPE_THETA = 500000.0
D_FF = 4 * D_MODEL
PLE_DIM = 256
RMS_EPS = 1e-6
LANES = 128
SUBLANES = 8
BF16_ROWS = 16
CONV_HALO = 8
REPACK_ROWS = 128
MLP_PART_ROWS = 256
IN_PART_ROWS = 128
IN_PARTS_TOGETHER = 1
QK_SCALE = HEAD_DIM ** -0.5
MOBA_Q_SCALE = QK_SCALE * math.log2(math.e)
VMEM_LIMIT = 56 * 1024 * 1024

C_GQKV = 0
C_GZ = 3 * GDN_WIDTH
C_MQ = 4 * GDN_WIDTH
C_MK = C_MQ + MOBA_WIDTH
C_MV = C_MK + MOBA_WIDTH
C_BA = C_MV + MOBA_WIDTH
C_END = C_BA + LANES


def _rms(x, w):
    return x * lax.rsqrt(jnp.mean(x * x, axis=-1, keepdims=True) + RMS_EPS) * w


def _sigmoid(x):
    return 1.0 / (1.0 + jnp.exp(-x))


def _aligned(x, m):
    return x if isinstance(x, int) else pl.multiple_of(x, m)


def _dot(a, b):
    return jnp.dot(a.astype(BF16), b.astype(BF16), preferred_element_type=F32)


def _dot_nt(a, b):
    return lax.dot_general(a.astype(BF16), b.astype(BF16), (((1,), (1,)), ((), ())),
                           preferred_element_type=F32)


def _in_proj_body(x_ref, nw_ref, win_ref, cw_ref, gp_ref, cos_ref, sa_ref, sb_ref,
                  gq_ref, gk_ref, gv_ref, gz_ref, bg_ref, mq_ref, mk_ref, mv_ref,
                  halo, w_ref, *, tm):
    i = pl.program_id(1)

    @pl.when((pl.program_id(0) == 0) & (i == 0))
    def _():
        n_gate = 2 * GDN_HEADS
        for c0 in range(0, C_BA, REPACK_ROWS):
            src = c0 if c0 < C_MQ else c0 + n_gate
            w_ref[:, c0:c0 + REPACK_ROWS] = win_ref[src:src + REPACK_ROWS, :].T.astype(BF16)
        gate_rows = jnp.concatenate([win_ref[C_MQ:C_MQ + n_gate, :],
                                     jnp.zeros((LANES - n_gate, D_MODEL), F32)], axis=0)
        w_ref[:, C_BA:C_END] = gate_rows.T.astype(BF16)

    parts = [slice(r0, r0 + IN_PART_ROWS) for r0 in range(0, tm, IN_PART_ROWS)]
    ns = []
    u = [None for _ in parts]

    def proj(n, c0, width):
        if u[n] is None:
            u[n] = _rms(x_ref[parts[n], :], nw_ref[...]).astype(BF16)
        return jnp.dot(u[n], w_ref[:, c0:c0 + width], preferred_element_type=F32)

    @pl.when(i == 0)
    def _():
        halo[...] = jnp.zeros_like(halo)

    pair_w = 2 * HEAD_DIM

    def conv_pair(p):
        pairs = {n: proj(n, C_GQKV + p * pair_w, pair_w) for n in ns}
        for s in (2 * p, 2 * p + 1):
            c0 = s * HEAD_DIM
            cw = [cw_ref[j:j + 1, c0:c0 + HEAD_DIM] for j in range(GDN_CONV)]
            which, h = divmod(s, GDN_HEADS)
            cur = {n: pairs[n][:, (s % 2) * HEAD_DIM:(s % 2 + 1) * HEAD_DIM] for n in ns}
            prev = halo[:, c0:c0 + HEAD_DIM]
            halo[:, c0:c0 + HEAD_DIM] = cur[ns[-1]][IN_PART_ROWS - CONV_HALO:, :]
            for n in ns:
                xs = jnp.concatenate([prev, cur[n]], axis=0)
                prev = cur[n][IN_PART_ROWS - CONV_HALO:, :]
                x1 = pltpu.roll(xs, 1, 0)
                acc = (cw[3] * xs + cw[2] * x1) + pltpu.roll(cw[1] * xs + cw[0] * x1, 2, 0)
                acc = acc[CONV_HALO:, :]
                y = acc * _sigmoid(acc)
                if which < 2:
                    inv_norm = lax.rsqrt(jnp.sum(y * y, axis=-1, keepdims=True) + RMS_EPS)
                    y = y * (inv_norm * QK_SCALE if which == 0 else inv_norm)
                (gq_ref, gk_ref, gv_ref)[which][parts[n], h * HEAD_DIM:(h + 1) * HEAD_DIM] = (
                    y.astype(BF16))

    def plain_pair(ref, c_base, p):
        vals = {n: proj(n, c_base + p * pair_w, pair_w) for n in ns}
        for n in ns:
            ref[parts[n], p * pair_w:(p + 1) * pair_w] = vals[n].astype(BF16)

    def rotary_pair(ref, c_base, scale, p):
        half = ROPE_DIMS // 2
        pairs = {n: proj(n, c_base + p * pair_w, pair_w) for n in ns}
        for n in ns:
            cos, sa, sb = cos_ref[parts[n], :], sa_ref[parts[n], :], sb_ref[parts[n], :]
            for h in (2 * p, 2 * p + 1):
                xr = pairs[n][:, (h % 2) * HEAD_DIM:(h % 2 + 1) * HEAD_DIM]
                rot = (xr * cos + pltpu.roll(xr, LANES - half, 1) * sa
                       + pltpu.roll(xr, half, 1) * sb)
                if scale is not None:
                    rot = rot * scale
                ref[parts[n], h * HEAD_DIM:(h + 1) * HEAD_DIM] = rot.astype(BF16)

    def gates():
        lane = lax.broadcasted_iota(jnp.int32, (IN_PART_ROWS, LANES), 1)
        bas = {n: proj(n, C_BA, LANES) for n in ns}
        for n in ns:
            xg = bas[n] + gp_ref[1:2, :]
            e = jnp.exp(-jnp.abs(xg))
            r = 1.0 / (1.0 + e)
            sig = jnp.where(xg >= 0.0, r, e * r)
            g = -jnp.exp(gp_ref[0:1, :]) * (jnp.maximum(xg, 0.0) + jnp.log1p(e))
            bg_ref[parts[n], :] = jnp.where(lane < GDN_HEADS, sig,
                                            jnp.where(lane < 2 * GDN_HEADS, g, 0.0))

    convs = [functools.partial(conv_pair, p) for p in range(3 * GDN_WIDTH // pair_w)]
    light = ([functools.partial(plain_pair, gz_ref, C_GZ, p) for p in range(GDN_WIDTH // pair_w)]
             + [functools.partial(plain_pair, mv_ref, C_MV, p) for p in range(MOBA_WIDTH // pair_w)]
             + [functools.partial(rotary_pair, mq_ref, C_MQ, MOBA_Q_SCALE, p)
                for p in range(MOBA_WIDTH // pair_w)]
             + [functools.partial(rotary_pair, mk_ref, C_MK, None, p)
                for p in range(MOBA_WIDTH // pair_w)]
             + [gates])
    for first in range(0, len(parts), IN_PARTS_TOGETHER):
        ns[:] = range(first, first + IN_PARTS_TOGETHER)
        done = 0
        for n, conv in enumerate(convs, start=1):
            conv()
            upto = len(light) * n // len(convs)
            for stage in light[done:upto]:
                stage()
            done = upto


def _in_proj(x, nw, w_in, layer, conv_w, gparams, cos_t, sa_t, sb_t, *, tm):
    B, T, D = x.shape
    tok = lambda width: pl.BlockSpec((None, tm, width), lambda b, i: (b, i, 0))
    const = lambda shape: pl.BlockSpec(shape, lambda b, i: (0,) * len(shape))
    table = pl.BlockSpec((tm, LANES), lambda b, i: (i, 0))
    o512 = jax.ShapeDtypeStruct((B, T, GDN_WIDTH), BF16)
    return pl.pallas_call(
        functools.partial(_in_proj_body, tm=tm),
        grid=(B, T // tm),
        in_specs=[tok(D), const((1, D)),
                  pl.BlockSpec((None,) + w_in.shape[1:], lambda b, i: (layer, 0, 0),
                               pipeline_mode=pl.Buffered(1)),
                  const((GDN_CONV, 3 * GDN_WIDTH)),
                  const((2, LANES)), table, table, table],
        out_specs=[tok(GDN_WIDTH)] * 4 + [tok(LANES)] + [tok(MOBA_WIDTH)] * 3,
        out_shape=[o512] * 4 + [jax.ShapeDtypeStruct((B, T, LANES), F32)] + [o512] * 3,
        scratch_shapes=[pltpu.VMEM((CONV_HALO, 3 * GDN_WIDTH), F32),
                        pltpu.VMEM((D, C_END), BF16)],
        compiler_params=pltpu.CompilerParams(
            dimension_semantics=("arbitrary", "arbitrary"), vmem_limit_bytes=VMEM_LIMIT),
        name="in_proj",
    )(x, nw, w_in, conv_w, gparams, cos_t, sa_t, sb_t)


def _gdn_body(q_ref, k_ref, v_ref, z_ref, bg_ref, nw_ref, *rest, T, n_cast):
    C = GDN_CHUNK
    H = GDN_HEADS
    G = GDN_GROUP
    rows_per_trip = G * C
    ntrips = T // rows_per_trip
    ri = lax.broadcasted_iota(jnp.int32, (C, C), 0)
    ci = lax.broadcasted_iota(jnp.int32, (C, C), 1)
    causal = ci <= ri
    strict = ci < ri
    eye = (ri == ci).astype(F32)
    cols = [slice(h * HEAD_DIM, (h + 1) * HEAD_DIM) for h in range(H)]
    for src, dst in zip(rest[:n_cast], rest[n_cast:2 * n_cast]):
        dst[...] = src[...].astype(dst.dtype)
    o_ref, s_ref, mneg_ref, n_ref, o1_ref, o2_ref, a_ref, snap_ref = rest[2 * n_cast:]
    s_ref[...] = jnp.zeros_like(s_ref)

    def phase_a(row0, slot, chunks):
        inst = [(c, h) for c in chunks for h in range(H)]
        rng = range(len(inst))
        r0s = {c: _aligned(row0 + c * C, C) for c in chunks}
        bgs = {c: bg_ref[pl.ds(r0s[c], C), :] for c in chunks}
        q = [q_ref[pl.ds(r0s[c], C), cols[h]].astype(F32) for c, h in inst]
        k = [k_ref[pl.ds(r0s[c], C), cols[h]].astype(F32) for c, h in inst]
        v = [v_ref[pl.ds(r0s[c], C), cols[h]].astype(F32) for c, h in inst]
        beta = [bgs[c][:, h:h + 1] for c, h in inst]
        gc_col, gc_last, decay = [], [], []
        for c, h in inst:
            gb = jnp.broadcast_to(bgs[c][:, H + h:H + h + 1], (C, C))
            row = jnp.sum(jnp.where(ri <= ci, gb, 0.0), axis=0, keepdims=True)
            g_row = jnp.sum(jnp.where(ri == ci, gb, 0.0), axis=0, keepdims=True)
            col = jnp.sum(jnp.where(causal, jnp.broadcast_to(g_row, (C, C)), 0.0),
                          axis=1, keepdims=True)
            gc_col.append(col)
            gc_last.append(row[:, C - 1:C])
            decay.append(jnp.exp(jnp.where(causal, col - row, -jnp.inf)))
        kb = [k[t] * beta[t] for t in rng]
        eg = [jnp.exp(gc_col[t]) for t in rng]
        kq = [_dot_nt(jnp.concatenate([kb[t], q[t]], axis=0), k[t]) for t in rng]
        yield
        low = [jnp.where(strict, kq[t][:C] * decay[t], 0.0) for t in rng]
        intra = [jnp.where(causal, kq[t][C:] * decay[t], 0.0) for t in rng]
        inv = [eye - low[t] for t in rng]
        pw = [_dot(low[t], low[t]) for t in rng]
        yield
        for _ in range(4):
            r = [_dot(jnp.concatenate([pw[t], inv[t]], axis=0), pw[t]) for t in rng]
            inv = [inv[t] + r[t][C:] for t in rng]
            pw = [r[t][:C] for t in rng]
            yield
        inv = [inv[t] + _dot(inv[t], pw[t]) for t in rng]
        yield
        wu = [_dot(inv[t], jnp.concatenate([kb[t] * eg[t], v[t] * beta[t]], axis=-1))
              for t in rng]
        yield
        kd = [k[t] * jnp.exp(gc_last[t] - gc_col[t]) for t in rng]
        r = [_dot(jnp.concatenate([kd[t].T, intra[t]], axis=0), wu[t]) for t in rng]
        mn = [r[t][:HEAD_DIM] for t in rng]
        io = [r[t][HEAD_DIM:] for t in rng]
        for t, (c, h) in enumerate(inst):
            rows = slice(c * C, (c + 1) * C)
            mneg_ref[slot, c, h] = (-mn[t][:, :HEAD_DIM]).astype(BF16)
            n_ref[slot, c, h] = mn[t][:, HEAD_DIM:]
            o1_ref[slot, rows, cols[h]] = (q[t] * eg[t] - io[t][:, :HEAD_DIM]).astype(BF16)
            o2_ref[slot, rows, cols[h]] = io[t][:, HEAD_DIM:]
            a_ref[slot, c, h] = jnp.broadcast_to(jnp.exp(gc_last[t]), (SUBLANES, HEAD_DIM))
        yield

    def phase_b_step(slot, c):
        hs = range(H)
        state = [s_ref[h] for h in hs]
        sb = [state[h].astype(BF16) for h in hs]
        ms = [jnp.dot(mneg_ref[slot, c, h], sb[h], preferred_element_type=F32) for h in hs]
        for h in hs:
            snap_ref[slot, c, h] = sb[h]
            s_ref[h] = state[h] * a_ref[slot, c, h, 0:1, :] + ms[h] + n_ref[slot, c, h]

    def phase_c(row0, slot, chunks):
        inst = [(c, h) for c in chunks for h in range(H)]
        os_ = [jnp.dot(o1_ref[slot, c * C:(c + 1) * C, cols[h]], snap_ref[slot, c, h],
                       preferred_element_type=F32) for c, h in inst]
        for t, (c, h) in enumerate(inst):
            r0 = _aligned(row0 + c * C, C)
            o = os_[t] + o2_ref[slot, c * C:(c + 1) * C, cols[h]]
            z = z_ref[pl.ds(r0, C), cols[h]].astype(F32)
            o_ref[pl.ds(r0, C), cols[h]] = (_rms(o, nw_ref[...])
                                            * (z * _sigmoid(z))).astype(o_ref.dtype)

    def trip(g, do_a, do_b, do_c):
        row0 = lambda d: _aligned((g - d) * rows_per_trip, rows_per_trip)
        slot = lambda d: (g - d) % GDN_RING
        c_steps = ([functools.partial(phase_c, row0(2), slot(2), (c,)) for c in range(G)]
                   if do_c else [])
        b_steps = [functools.partial(phase_b_step, slot(1), c) for c in range(G)] if do_b else []
        if do_a:
            per_wave = G // GDN_WAVES
            waves = [phase_a(row0(0), slot(0), tuple(range(w * per_wave, (w + 1) * per_wave)))
                     for w in range(GDN_WAVES)]
            n = 0
            while any(w is not None for w in waves):
                for i, w in enumerate(waves):
                    if w is not None and n >= i * GDN_WAVE_LAG:
                        if next(w, "done") == "done":
                            waves[i] = None
                if n % 2 == 0 and c_steps:
                    c_steps.pop(0)()
                if n % 2 == 1 and b_steps:
                    b_steps.pop(0)()
                n += 1
        while b_steps or c_steps:
            for steps in (b_steps, c_steps):
                if steps:
                    steps.pop(0)()

    def steady(g, carry):
        trip(g, True, True, True)
        return carry

    all_on = lambda g: g < ntrips and 1 <= g <= ntrips and 2 <= g <= ntrips + 1
    g = 0
    while g < ntrips + 2:
        if all_on(g):
            last = g
            while all_on(last + 1):
                last += 1
            lax.fori_loop(g, last + 1, steady, 0)
            g = last + 1
        else:
            trip(g, g < ntrips, 1 <= g <= ntrips, 2 <= g <= ntrips + 1)
            g += 1


def _gdn(gq, gk, gv, gz, bg, nw, weights):
    B, T, _ = gq.shape
    banded = all(w.shape[0] % (B * BF16_ROWS) == 0 for w in weights)
    cast_in = list(weights) if banded else []
    band = lambda w: pl.BlockSpec((w.shape[0] // B, w.shape[1]), lambda b: (b, 0))
    rows_per_trip = GDN_GROUP * GDN_CHUNK
    assert T % rows_per_trip == 0
    tok = lambda width: pl.BlockSpec((None, T, width), lambda b: (b, 0, 0))
    per_chunk = (GDN_RING, GDN_GROUP, GDN_HEADS)
    outs = pl.pallas_call(
        functools.partial(_gdn_body, T=T, n_cast=len(cast_in)),
        grid=(B,),
        in_specs=([tok(GDN_WIDTH)] * 4 + [tok(LANES), pl.BlockSpec((1, HEAD_DIM), lambda b: (0, 0))]
                  + [band(w) for w in cast_in]),
        out_specs=[band(w) for w in cast_in] + [tok(GDN_WIDTH)],
        out_shape=([jax.ShapeDtypeStruct(w.shape, BF16) for w in cast_in]
                   + [jax.ShapeDtypeStruct((B, T, GDN_WIDTH), BF16)]),
        scratch_shapes=[
            pltpu.VMEM((GDN_HEADS, HEAD_DIM, HEAD_DIM), F32),
            pltpu.VMEM(per_chunk + (HEAD_DIM, HEAD_DIM), BF16),
            pltpu.VMEM(per_chunk + (HEAD_DIM, HEAD_DIM), F32),
            pltpu.VMEM((GDN_RING, rows_per_trip, GDN_WIDTH), BF16),
            pltpu.VMEM((GDN_RING, rows_per_trip, GDN_WIDTH), F32),
            pltpu.VMEM(per_chunk + (SUBLANES, HEAD_DIM), F32),
            pltpu.VMEM(per_chunk + (HEAD_DIM, HEAD_DIM), BF16),
        ],
        compiler_params=pltpu.CompilerParams(
            dimension_semantics=("arbitrary",), vmem_limit_bytes=VMEM_LIMIT),
        name="gdn",
    )(gq, gk, gv, gz, bg, nw, *cast_in)
    cast = outs[:-1] if banded else [w.astype(BF16) for w in weights]
    return outs[-1], cast


def _moba_body(q_ref, k_ref, v_ref, o_ref, vt_ref, *, T):
    BS = MOBA_BLOCK
    nb = T // BS
    nrow = -(-nb // SUBLANES) * SUBLANES
    neg = -jnp.inf
    heads = [slice(h * HEAD_DIM, (h + 1) * HEAD_DIM) for h in range(MOBA_HEADS_PER_STEP)]
    blk = lax.broadcasted_iota(jnp.int32, (nrow, BS), 0)
    key = lax.broadcasted_iota(jnp.int32, (BS, BS), 0)
    qry = lax.broadcasted_iota(jnp.int32, (BS, BS), 1)

    kmean = []
    for n, hd in enumerate(heads):
        means = [jnp.mean(k_ref[j * BS:(j + 1) * BS, hd].astype(F32), axis=0, keepdims=True)
                 for j in range(nb)]
        if nrow > nb:
            means.append(jnp.zeros((nrow - nb, HEAD_DIM), F32))
        kmean.append(jnp.concatenate(means, axis=0))
        for j in range(nb):
            vt_ref[n, 0:HEAD_DIM, j * BS:(j + 1) * BS] = v_ref[j * BS:(j + 1) * BS, hd].T
        vt_ref[n, HEAD_DIM:, :] = jnp.ones((BF16_ROWS, T), BF16)

    def block_scores(n, i):
        q = q_ref[i * BS:(i + 1) * BS, heads[n]]
        sel = None
        if i > MOBA_TOPK:
            gate = lax.dot_general(kmean[n], q.astype(F32), (((1,), (1,)), ((), ())),
                                   precision=lax.Precision.HIGHEST, preferred_element_type=F32)
            gate = jnp.where(blk < i, gate, neg)
            rank = jnp.zeros((nrow, BS), F32)
            for jp in range(i):
                gj = gate[jp:jp + 1, :]
                beats = (gj > gate) | ((gj == gate) & (blk > jp))
                rank = rank + jnp.where(beats, 1.0, 0.0)
            sel = jnp.where((rank < MOBA_TOPK) & (blk < i), 1.0, 0.0)
        scores = []
        for j in range(i + 1):
            s = _dot_nt(k_ref[j * BS:(j + 1) * BS, heads[n]], q)
            if j == i:
                s = jnp.where(key <= qry, s, neg)
            elif sel is not None:
                s = jnp.where(sel[j:j + 1, :] > 0.5, s, neg)
            scores.append(s)
        return scores

    def block_output(n, i, scores):
        m = scores[0].max(axis=0, keepdims=True)
        for s in scores[1:]:
            m = jnp.maximum(m, s.max(axis=0, keepdims=True))
        acc = jnp.zeros((HEAD_DIM + BF16_ROWS, BS), F32)
        for j, s in enumerate(scores):
            acc = acc + _dot(vt_ref[n, :, j * BS:(j + 1) * BS], jnp.exp2(s - m))
        out = acc[:HEAD_DIM, :] / acc[HEAD_DIM:HEAD_DIM + 1, :]
        o_ref[i * BS:(i + 1) * BS, heads[n]] = out.T.astype(o_ref.dtype)

    tasks = [(n, i) for i in range(nb) for n in range(MOBA_HEADS_PER_STEP)]
    pending = {t: block_scores(*tasks[t]) for t in range(min(MOBA_LOOKAHEAD, len(tasks)))}
    for t, (n, i) in enumerate(tasks):
        if t + MOBA_LOOKAHEAD < len(tasks):
            pending[t + MOBA_LOOKAHEAD] = block_scores(*tasks[t + MOBA_LOOKAHEAD])
        block_output(n, i, pending.pop(t))


def _moba(mq, mk, mv):
    B, T, _ = mq.shape
    assert T % MOBA_BLOCK == 0 and MOBA_HEADS % MOBA_HEADS_PER_STEP == 0
    width = MOBA_HEADS_PER_STEP * HEAD_DIM
    head = pl.BlockSpec((None, T, width), lambda b, h: (b, 0, h))
    return pl.pallas_call(
        functools.partial(_moba_body, T=T),
        grid=(B, MOBA_HEADS // MOBA_HEADS_PER_STEP),
        in_specs=[head] * 3,
        out_specs=head,
        out_shape=jax.ShapeDtypeStruct((B, T, MOBA_WIDTH), BF16),
        scratch_shapes=[pltpu.VMEM((MOBA_HEADS_PER_STEP, HEAD_DIM + BF16_ROWS, T), BF16)],
        compiler_params=pltpu.CompilerParams(
            dimension_semantics=("arbitrary", "arbitrary"), vmem_limit_bytes=VMEM_LIMIT),
        name="moba",
    )(mq, mk, mv)


def _mlp_body(og_ref, om_ref, x_ref, p_ref, wo_ref, n1_ref, n2_ref, wu_ref, wd_ref, n3_ref,
              wg_ref, wp_ref, o_ref, *, ff_chunk):
    parts = [slice(r0, r0 + MLP_PART_ROWS) for r0 in range(0, x_ref.shape[0], MLP_PART_ROWS)]
    ns = range(len(parts))
    mix = [_dot(og_ref[r, :], wo_ref[0:GDN_WIDTH, :])
           + _dot(om_ref[r, :], wo_ref[GDN_WIDTH:GDN_WIDTH + MOBA_WIDTH, :]) for r in parts]
    h = [x_ref[r, :] + _rms(mix[n], n1_ref[...]) for n, r in enumerate(parts)]
    a = [_rms(h[n], n2_ref[...]).astype(BF16) for n in ns]
    f = [None for _ in ns]
    for c0 in range(0, D_FF, ff_chunk):
        up = [jnp.dot(a[n], wu_ref[:, c0:c0 + ff_chunk], preferred_element_type=F32)
              for n in ns]
        for n in ns:
            part = _dot(jnp.square(jnp.maximum(up[n], 0.0)), wd_ref[c0:c0 + ff_chunk, :])
            f[n] = part if f[n] is None else f[n] + part
    h = [h[n] + _rms(f[n], n3_ref[...]) for n in ns]
    gate = [_dot(h[n], wg_ref[...]) for n in ns]
    ple = [_dot(p_ref[r, :], wp_ref[...]) for r in parts]
    for n, r in enumerate(parts):
        o_ref[r, :] = h[n] + _sigmoid(gate[n]) * ple[n]


def _mlp(og, om, x, p, wo, n1, n2, wu, wd, n3, wg, wp, *, tm):
    B, T, D = x.shape
    tok = lambda width: pl.BlockSpec((None, tm, width), lambda b, i: (b, i, 0))
    const = lambda shape: pl.BlockSpec(shape, lambda b, i: (0,) * len(shape),
                                       pipeline_mode=pl.Buffered(1))
    return pl.pallas_call(
        functools.partial(_mlp_body, ff_chunk=1024),
        grid=(B, T // tm),
        in_specs=[tok(GDN_WIDTH), tok(MOBA_WIDTH), tok(D), tok(PLE_DIM),
                  const((D, D)), const((1, D)), const((1, D)), const((D, D_FF)),
                  const((D_FF, D)), const((1, D)), const((D, D)), const((PLE_DIM, D))],
        out_specs=tok(D),
        out_shape=jax.ShapeDtypeStruct((B, T, D), F32),
        compiler_params=pltpu.CompilerParams(
            dimension_semantics=("arbitrary", "arbitrary"), vmem_limit_bytes=VMEM_LIMIT),
        name="mlp",
    )(og, om, x, p, wo, n1, n2, wu, wd, n3, wg, wp)


def _rope_tables(T):
    half = ROPE_DIMS // 2
    inv_freq = ROPE_THETA ** (-np.arange(half, dtype=np.float64) * (2.0 / ROPE_DIMS))
    ang = np.arange(T, dtype=np.float64)[:, None] * inv_freq[None, :]
    cos, sin = np.cos(ang), np.sin(ang)
    zeros = np.zeros((T, HEAD_DIM - ROPE_DIMS))
    z_half = np.zeros((T, half))
    cos_t = np.concatenate([cos, cos, np.ones((T, HEAD_DIM - ROPE_DIMS))], axis=-1)
    sa_t = np.concatenate([-sin, z_half, zeros], axis=-1)
    sb_t = np.concatenate([z_half, sin, zeros], axis=-1)
    cos_t, sa_t, sb_t = (jnp.asarray(t, dtype=F32) for t in (cos_t, sa_t, sb_t))
    return cos_t, sa_t, sb_t


def _layer(h, p_i, w_in, layer, conv_w, a_log, dt_bias, gdn_norm_w, w_out, attn_pre_norm,
           attn_post_norm, mlp_pre_norm, mlp_post_norm, w_up, w_down, w_ple, w_ple_gate, tables):
    B, T, D = h.shape
    tm = next(t for t in (512, MOBA_BLOCK) if T % t == 0)
    assert T % tm == 0 and T % MOBA_BLOCK == 0 and T % GDN_CHUNK == 0
    nh = 2 * GDN_HEADS
    gparams = jnp.zeros((2, LANES), F32)
    gparams = gparams.at[0, GDN_HEADS:nh].set(a_log.astype(F32))
    gparams = gparams.at[1, GDN_HEADS:nh].set(dt_bias.astype(F32))
    row = lambda v: v.reshape(1, -1).astype(F32)

    gq, gk, gv, gz, bg, mq, mk, mv = _in_proj(
        h, row(attn_pre_norm), jnp.swapaxes(w_in.astype(F32), 1, 2), layer, conv_w.astype(F32),
        gparams, *tables,
        tm=next(t for t in (1024, 512, MOBA_BLOCK) if T % t == 0))
    o_gdn, (wo, wu, wd, wg, wp) = _gdn(
        gq, gk, gv, gz, bg, row(gdn_norm_w),
        [w.astype(F32) for w in (w_out, w_up, w_down, w_ple_gate, w_ple)])
    o_moba = _moba(mq, mk, mv)
    return _mlp(o_gdn, o_moba, h, p_i, wo, row(attn_post_norm), row(mlp_pre_norm), wu, wd,
                row(mlp_post_norm), wg, wp, tm=tm)


def kernel(x, p, w_in, conv_w, a_log, dt_bias, gdn_norm_w, w_out, attn_pre_norm, attn_post_norm,
           mlp_pre_norm, mlp_post_norm, w_up, w_down, w_ple, w_ple_gate):
    tables = _rope_tables(x.shape[1])
    h = x
    for i in range(w_in.shape[0]):
        h = _layer(h, p[i], w_in, i, conv_w[i], a_log[i], dt_bias[i], gdn_norm_w[i], w_out[i],
                   attn_pre_norm[i], attn_post_norm[i], mlp_pre_norm[i], mlp_post_norm[i],
                   w_up[i], w_down[i], w_ple[i], w_ple_gate[i], tables)
    return h
```

```python
import functools
import math

import jax
import jax.numpy as jnp
import numpy as np
from jax import lax
from jax.experimental import pallas as pl
from jax.experimental.pallas import tpu as pltpu

F32 = jnp.float32
BF16 = jnp.bfloat16

D_MODEL = 1024
HEAD_DIM = 128
GDN_HEADS = 4
GDN_WIDTH = GDN_HEADS * HEAD_DIM
GDN_CONV = 4
GDN_CHUNK = 64
GDN_GROUP = 4
GDN_WAVES = 2
GDN_WAVE_LAG = 1
GDN_RING = 3
MOBA_HEADS = 4
MOBA_WIDTH = MOBA_HEADS * HEAD_DIM
MOBA_BLOCK = 256
MOBA_TOPK = 3
MOBA_HEADS_PER_STEP = 4
MOBA_LOOKAHEAD = 3
ROPE_DIMS = 32
ROPE_THETA = 500000.0
D_FF = 4 * D_MODEL
PLE_DIM = 256
RMS_EPS = 1e-6
LANES = 128
SUBLANES = 8
BF16_ROWS = 16
CONV_HALO = 8
REPACK_ROWS = 128
MLP_PART_ROWS = 256
IN_PART_ROWS = 128
IN_PARTS_TOGETHER = 1
QK_SCALE = HEAD_DIM ** -0.5
MOBA_Q_SCALE = QK_SCALE * math.log2(math.e)
VMEM_LIMIT = 56 * 1024 * 1024

C_GQKV = 0
C_GZ = 3 * GDN_WIDTH
C_MQ = 4 * GDN_WIDTH
C_MK = C_MQ + MOBA_WIDTH
C_MV = C_MK + MOBA_WIDTH
C_BA = C_MV + MOBA_WIDTH
C_END = C_BA + LANES


def _rms(x, w):
    return x * lax.rsqrt(jnp.mean(x * x, axis=-1, keepdims=True) + RMS_EPS) * w


def _sigmoid(x):
    return 1.0 / (1.0 + jnp.exp(-x))


def _aligned(x, m):
    return x if isinstance(x, int) else pl.multiple_of(x, m)


def _dot(a, b):
    return jnp.dot(a.astype(BF16), b.astype(BF16), preferred_element_type=F32)


def _dot_nt(a, b):
    return lax.dot_general(a.astype(BF16), b.astype(BF16), (((1,), (1,)), ((), ())),
                           preferred_element_type=F32)


def _in_proj_body(x_ref, nw_ref, win_ref, cw_ref, gp_ref, cos_ref, sa_ref, sb_ref,
                  gq_ref, gk_ref, gv_ref, gz_ref, bg_ref, mq_ref, mk_ref, mv_ref,
                  halo, w_ref, *, tm):
    i = pl.program_id(1)

    @pl.when((pl.program_id(0) == 0) & (i == 0))
    def _():
        n_gate = 2 * GDN_HEADS
        for c0 in range(0, C_BA, REPACK_ROWS):
            src = c0 if c0 < C_MQ else c0 + n_gate
            w_ref[:, c0:c0 + REPACK_ROWS] = win_ref[src:src + REPACK_ROWS, :].T.astype(BF16)
        gate_rows = jnp.concatenate([win_ref[C_MQ:C_MQ + n_gate, :],
                                     jnp.zeros((LANES - n_gate, D_MODEL), F32)], axis=0)
        w_ref[:, C_BA:C_END] = gate_rows.T.astype(BF16)

    parts = [slice(r0, r0 + IN_PART_ROWS) for r0 in range(0, tm, IN_PART_ROWS)]
    ns = []
    u = [None for _ in parts]

    def proj(n, c0, width):
        if u[n] is None:
            u[n] = _rms(x_ref[parts[n], :], nw_ref[...]).astype(BF16)
        return jnp.dot(u[n], w_ref[:, c0:c0 + width], preferred_element_type=F32)

    @pl.when(i == 0)
    def _():
        halo[...] = jnp.zeros_like(halo)

    pair_w = 2 * HEAD_DIM

    def conv_pair(p):
        pairs = {n: proj(n, C_GQKV + p * pair_w, pair_w) for n in ns}
        for s in (2 * p, 2 * p + 1):
            c0 = s * HEAD_DIM
            cw = [cw_ref[j:j + 1, c0:c0 + HEAD_DIM] for j in range(GDN_CONV)]
            which, h = divmod(s, GDN_HEADS)
            cur = {n: pairs[n][:, (s % 2) * HEAD_DIM:(s % 2 + 1) * HEAD_DIM] for n in ns}
            prev = halo[:, c0:c0 + HEAD_DIM]
            halo[:, c0:c0 + HEAD_DIM] = cur[ns[-1]][IN_PART_ROWS - CONV_HALO:, :]
            for n in ns:
                xs = jnp.concatenate([prev, cur[n]], axis=0)
                prev = cur[n][IN_PART_ROWS - CONV_HALO:, :]
                x1 = pltpu.roll(xs, 1, 0)
                acc = (cw[3] * xs + cw[2] * x1) + pltpu.roll(cw[1] * xs + cw[0] * x1, 2, 0)
                acc = acc[CONV_HALO:, :]
                y = acc * _sigmoid(acc)
                if which < 2:
                    inv_norm = lax.rsqrt(jnp.sum(y * y, axis=-1, keepdims=True) + RMS_EPS)
                    y = y * (inv_norm * QK_SCALE if which == 0 else inv_norm)
                (gq_ref, gk_ref, gv_ref)[which][parts[n], h * HEAD_DIM:(h + 1) * HEAD_DIM] = (
                    y.astype(BF16))

    def plain_pair(ref, c_base, p):
        vals = {n: proj(n, c_base + p * pair_w, pair_w) for n in ns}
        for n in ns:
            ref[parts[n], p * pair_w:(p + 1) * pair_w] = vals[n].astype(BF16)

    def rotary_pair(ref, c_base, scale, p):
        half = ROPE_DIMS // 2
        pairs = {n: proj(n, c_base + p * pair_w, pair_w) for n in ns}
        for n in ns:
            cos, sa, sb = cos_ref[parts[n], :], sa_ref[parts[n], :], sb_ref[parts[n], :]
            for h in (2 * p, 2 * p + 1):
                xr = pairs[n][:, (h % 2) * HEAD_DIM:(h % 2 + 1) * HEAD_DIM]
                rot = (xr * cos + pltpu.roll(xr, LANES - half, 1) * sa
                       + pltpu.roll(xr, half, 1) * sb)
                if scale is not None:
                    rot = rot * scale
                ref[parts[n], h * HEAD_DIM:(h + 1) * HEAD_DIM] = rot.astype(BF16)

    def gates():
        lane = lax.broadcasted_iota(jnp.int32, (IN_PART_ROWS, LANES), 1)
        bas = {n: proj(n, C_BA, LANES) for n in ns}
        for n in ns:
            xg = bas[n] + gp_ref[1:2, :]
            e = jnp.exp(-jnp.abs(xg))
            r = 1.0 / (1.0 + e)
            sig = jnp.where(xg >= 0.0, r, e * r)
            g = -jnp.exp(gp_ref[0:1, :]) * (jnp.maximum(xg, 0.0) + jnp.log1p(e))
            bg_ref[parts[n], :] = jnp.where(lane < GDN_HEADS, sig,
                                            jnp.where(lane < 2 * GDN_HEADS, g, 0.0))

    convs = [functools.partial(conv_pair, p) for p in range(3 * GDN_WIDTH // pair_w)]
    light = ([functools.partial(plain_pair, gz_ref, C_GZ, p) for p in range(GDN_WIDTH // pair_w)]
             + [functools.partial(plain_pair, mv_ref, C_MV, p) for p in range(MOBA_WIDTH // pair_w)]
             + [functools.partial(rotary_pair, mq_ref, C_MQ, MOBA_Q_SCALE, p)
                for p in range(MOBA_WIDTH // pair_w)]
             + [functools.partial(rotary_pair, mk_ref, C_MK, None, p)
                for p in range(MOBA_WIDTH // pair_w)]
             + [gates])
    for first in range(0, len(parts), IN_PARTS_TOGETHER):
        ns[:] = range(first, first + IN_PARTS_TOGETHER)
        done = 0
        for n, conv in enumerate(convs, start=1):
            conv()
            upto = len(light) * n // len(convs)
            for stage in light[done:upto]:
                stage()
            done = upto


def _in_proj(x, nw, w_in, layer, conv_w, gparams, cos_t, sa_t, sb_t, *, tm):
    B, T, D = x.shape
    tok = lambda width: pl.BlockSpec((None, tm, width), lambda b, i: (b, i, 0))
    const = lambda shape: pl.BlockSpec(shape, lambda b, i: (0,) * len(shape))
    table = pl.BlockSpec((tm, LANES), lambda b, i: (i, 0))
    o512 = jax.ShapeDtypeStruct((B, T, GDN_WIDTH), BF16)
    return pl.pallas_call(
        functools.partial(_in_proj_body, tm=tm),
        grid=(B, T // tm),
        in_specs=[tok(D), const((1, D)),
                  pl.BlockSpec((None,) + w_in.shape[1:], lambda b, i: (layer, 0, 0),
                               pipeline_mode=pl.Buffered(1)),
                  const((GDN_CONV, 3 * GDN_WIDTH)),
                  const((2, LANES)), table, table, table],
        out_specs=[tok(GDN_WIDTH)] * 4 + [tok(LANES)] + [tok(MOBA_WIDTH)] * 3,
        out_shape=[o512] * 4 + [jax.ShapeDtypeStruct((B, T, LANES), F32)] + [o512] * 3,
        scratch_shapes=[pltpu.VMEM((CONV_HALO, 3 * GDN_WIDTH), F32),
                        pltpu.VMEM((D, C_END), BF16)],
        compiler_params=pltpu.CompilerParams(
            dimension_semantics=("arbitrary", "arbitrary"), vmem_limit_bytes=VMEM_LIMIT),
        name="in_proj",
    )(x, nw, w_in, conv_w, gparams, cos_t, sa_t, sb_t)


def _gdn_body(q_ref, k_ref, v_ref, z_ref, bg_ref, nw_ref, *rest, T, n_cast):
    C = GDN_CHUNK
    H = GDN_HEADS
    G = GDN_GROUP
    rows_per_trip = G * C
    ntrips = T // rows_per_trip
    ri = lax.broadcasted_iota(jnp.int32, (C, C), 0)
    ci = lax.broadcasted_iota(jnp.int32, (C, C), 1)
    causal = ci <= ri
    strict = ci < ri
    eye = (ri == ci).astype(F32)
    cols = [slice(h * HEAD_DIM, (h + 1) * HEAD_DIM) for h in range(H)]
    for src, dst in zip(rest[:n_cast], rest[n_cast:2 * n_cast]):
        dst[...] = src[...].astype(dst.dtype)
    o_ref, s_ref, mneg_ref, n_ref, o1_ref, o2_ref, a_ref, snap_ref = rest[2 * n_cast:]
    s_ref[...] = jnp.zeros_like(s_ref)

    def phase_a(row0, slot, chunks):
        inst = [(c, h) for c in chunks for h in range(H)]
        rng = range(len(inst))
        r0s = {c: _aligned(row0 + c * C, C) for c in chunks}
        bgs = {c: bg_ref[pl.ds(r0s[c], C), :] for c in chunks}
        q = [q_ref[pl.ds(r0s[c], C), cols[h]].astype(F32) for c, h in inst]
        k = [k_ref[pl.ds(r0s[c], C), cols[h]].astype(F32) for c, h in inst]
        v = [v_ref[pl.ds(r0s[c], C), cols[h]].astype(F32) for c, h in inst]
        beta = [bgs[c][:, h:h + 1] for c, h in inst]
        gc_col, gc_last, decay = [], [], []
        for c, h in inst:
            gb = jnp.broadcast_to(bgs[c][:, H + h:H + h + 1], (C, C))
            row = jnp.sum(jnp.where(ri <= ci, gb, 0.0), axis=0, keepdims=True)
            g_row = jnp.sum(jnp.where(ri == ci, gb, 0.0), axis=0, keepdims=True)
            col = jnp.sum(jnp.where(causal, jnp.broadcast_to(g_row, (C, C)), 0.0),
                          axis=1, keepdims=True)
            gc_col.append(col)
            gc_last.append(row[:, C - 1:C])
            decay.append(jnp.exp(jnp.where(causal, col - row, -jnp.inf)))
        kb = [k[t] * beta[t] for t in rng]
        eg = [jnp.exp(gc_col[t]) for t in rng]
        kq = [_dot_nt(jnp.concatenate([kb[t], q[t]], axis=0), k[t]) for t in rng]
        yield
        low = [jnp.where(strict, kq[t][:C] * decay[t], 0.0) for t in rng]
        intra = [jnp.where(causal, kq[t][C:] * decay[t], 0.0) for t in rng]
        inv = [eye - low[t] for t in rng]
        pw = [_dot(low[t], low[t]) for t in rng]
        yield
        for _ in range(4):
            r = [_dot(jnp.concatenate([pw[t], inv[t]], axis=0), pw[t]) for t in rng]
            inv = [inv[t] + r[t][C:] for t in rng]
            pw = [r[t][:C] for t in rng]
            yield
        inv = [inv[t] + _dot(inv[t], pw[t]) for t in rng]
        yield
        wu = [_dot(inv[t], jnp.concatenate([kb[t] * eg[t], v[t] * beta[t]], axis=-1))
              for t in rng]
        yield
        kd = [k[t] * jnp.exp(gc_last[t] - gc_col[t]) for t in rng]
        r = [_dot(jnp.concatenate([kd[t].T, intra[t]], axis=0), wu[t]) for t in rng]
        mn = [r[t][:HEAD_DIM] for t in rng]
        io = [r[t][HEAD_DIM:] for t in rng]
        for t, (c, h) in enumerate(inst):
            rows = slice(c * C, (c + 1) * C)
            mneg_ref[slot, c, h] = (-mn[t][:, :HEAD_DIM]).astype(BF16)
            n_ref[slot, c, h] = mn[t][:, HEAD_DIM:]
            o1_ref[slot, rows, cols[h]] = (q[t] * eg[t] - io[t][:, :HEAD_DIM]).astype(BF16)
            o2_ref[slot, rows, cols[h]] = io[t][:, HEAD_DIM:]
            a_ref[slot, c, h] = jnp.broadcast_to(jnp.exp(gc_last[t]), (SUBLANES, HEAD_DIM))
        yield

    def phase_b_step(slot, c):
        hs = range(H)
        state = [s_ref[h] for h in hs]
        sb = [state[h].astype(BF16) for h in hs]
        ms = [jnp.dot(mneg_ref[slot, c, h], sb[h], preferred_element_type=F32) for h in hs]
        for h in hs:
            snap_ref[slot, c, h] = sb[h]
            s_ref[h] = state[h] * a_ref[slot, c, h, 0:1, :] + ms[h] + n_ref[slot, c, h]

    def phase_c(row0, slot, chunks):
        inst = [(c, h) for c in chunks for h in range(H)]
        os_ = [jnp.dot(o1_ref[slot, c * C:(c + 1) * C, cols[h]], snap_ref[slot, c, h],
                       preferred_element_type=F32) for c, h in inst]
        for t, (c, h) in enumerate(inst):
            r0 = _aligned(row0 + c * C, C)
            o = os_[t] + o2_ref[slot, c * C:(c + 1) * C, cols[h]]
            z = z_ref[pl.ds(r0, C), cols[h]].astype(F32)
            o_ref[pl.ds(r0, C), cols[h]] = (_rms(o, nw_ref[...])
                                            * (z * _sigmoid(z))).astype(o_ref.dtype)

    def trip(g, do_a, do_b, do_c):
        row0 = lambda d: _aligned((g - d) * rows_per_trip, rows_per_trip)
        slot = lambda d: (g - d) % GDN_RING
        c_steps = ([functools.partial(phase_c, row0(2), slot(2), (c,)) for c in range(G)]
                   if do_c else [])
        b_steps = [functools.partial(phase_b_step, slot(1), c) for c in range(G)] if do_b else []
        if do_a:
            per_wave = G // GDN_WAVES
            waves = [phase_a(row0(0), slot(0), tuple(range(w * per_wave, (w + 1) * per_wave)))
                     for w in range(GDN_WAVES)]
            n = 0
            while any(w is not None for w in waves):
                for i, w in enumerate(waves):
                    if w is not None and n >= i * GDN_WAVE_LAG:
                        if next(w, "done") == "done":
                            waves[i] = None
                if n % 2 == 0 and c_steps:
                    c_steps.pop(0)()
                if n % 2 == 1 and b_steps:
                    b_steps.pop(0)()
                n += 1
        while b_steps or c_steps:
            for steps in (b_steps, c_steps):
                if steps:
                    steps.pop(0)()

    def steady(g, carry):
        trip(g, True, True, True)
        return carry

    all_on = lambda g: g < ntrips and 1 <= g <= ntrips and 2 <= g <= ntrips + 1
    g = 0
    while g < ntrips + 2:
        if all_on(g):
            last = g
            while all_on(last + 1):
                last += 1
            lax.fori_loop(g, last + 1, steady, 0)
            g = last + 1
        else:
            trip(g, g < ntrips, 1 <= g <= ntrips, 2 <= g <= ntrips + 1)
            g += 1


def _gdn(gq, gk, gv, gz, bg, nw, weights):
    B, T, _ = gq.shape
    banded = all(w.shape[0] % (B * BF16_ROWS) == 0 for w in weights)
    cast_in = list(weights) if banded else []
    band = lambda w: pl.BlockSpec((w.shape[0] // B, w.shape[1]), lambda b: (b, 0))
    rows_per_trip = GDN_GROUP * GDN_CHUNK
    assert T % rows_per_trip == 0
    tok = lambda width: pl.BlockSpec((None, T, width), lambda b: (b, 0, 0))
    per_chunk = (GDN_RING, GDN_GROUP, GDN_HEADS)
    outs = pl.pallas_call(
        functools.partial(_gdn_body, T=T, n_cast=len(cast_in)),
        grid=(B,),
        in_specs=([tok(GDN_WIDTH)] * 4 + [tok(LANES), pl.BlockSpec((1, HEAD_DIM), lambda b: (0, 0))]
                  + [band(w) for w in cast_in]),
        out_specs=[band(w) for w in cast_in] + [tok(GDN_WIDTH)],
        out_shape=([jax.ShapeDtypeStruct(w.shape, BF16) for w in cast_in]
                   + [jax.ShapeDtypeStruct((B, T, GDN_WIDTH), BF16)]),
        scratch_shapes=[
            pltpu.VMEM((GDN_HEADS, HEAD_DIM, HEAD_DIM), F32),
            pltpu.VMEM(per_chunk + (HEAD_DIM, HEAD_DIM), BF16),
            pltpu.VMEM(per_chunk + (HEAD_DIM, HEAD_DIM), F32),
            pltpu.VMEM((GDN_RING, rows_per_trip, GDN_WIDTH), BF16),
            pltpu.VMEM((GDN_RING, rows_per_trip, GDN_WIDTH), F32),
            pltpu.VMEM(per_chunk + (SUBLANES, HEAD_DIM), F32),
            pltpu.VMEM(per_chunk + (HEAD_DIM, HEAD_DIM), BF16),
        ],
        compiler_params=pltpu.CompilerParams(
            dimension_semantics=("arbitrary",), vmem_limit_bytes=VMEM_LIMIT),
        name="gdn",
    )(gq, gk, gv, gz, bg, nw, *cast_in)
    cast = outs[:-1] if banded else [w.astype(BF16) for w in weights]
    return outs[-1], cast


def _moba_body(q_ref, k_ref, v_ref, o_ref, vt_ref, *, T):
    BS = MOBA_BLOCK
    nb = T // BS
    nrow = -(-nb // SUBLANES) * SUBLANES
    neg = -jnp.inf
    heads = [slice(h * HEAD_DIM, (h + 1) * HEAD_DIM) for h in range(MOBA_HEADS_PER_STEP)]
    blk = lax.broadcasted_iota(jnp.int32, (nrow, BS), 0)
    key = lax.broadcasted_iota(jnp.int32, (BS, BS), 0)
    qry = lax.broadcasted_iota(jnp.int32, (BS, BS), 1)

    kmean = []
    for n, hd in enumerate(heads):
        means = [jnp.mean(k_ref[j * BS:(j + 1) * BS, hd].astype(F32), axis=0, keepdims=True)
                 for j in range(nb)]
        if nrow > nb:
            means.append(jnp.zeros((nrow - nb, HEAD_DIM), F32))
        kmean.append(jnp.concatenate(means, axis=0))
        for j in range(nb):
            vt_ref[n, 0:HEAD_DIM, j * BS:(j + 1) * BS] = v_ref[j * BS:(j + 1) * BS, hd].T
        vt_ref[n, HEAD_DIM:, :] = jnp.ones((BF16_ROWS, T), BF16)

    def block_scores(n, i):
        q = q_ref[i * BS:(i + 1) * BS, heads[n]]
        sel = None
        if i > MOBA_TOPK:
            gate = lax.dot_general(kmean[n], q.astype(F32), (((1,), (1,)), ((), ())),
                                   precision=lax.Precision.HIGHEST, preferred_element_type=F32)
            gate = jnp.where(blk < i, gate, neg)
            rank = jnp.zeros((nrow, BS), F32)
            for jp in range(i):
                gj = gate[jp:jp + 1, :]
                beats = (gj > gate) | ((gj == gate) & (blk > jp))
                rank = rank + jnp.where(beats, 1.0, 0.0)
            sel = jnp.where((rank < MOBA_TOPK) & (blk < i), 1.0, 0.0)
        scores = []
        for j in range(i + 1):
            s = _dot_nt(k_ref[j * BS:(j + 1) * BS, heads[n]], q)
            if j == i:
                s = jnp.where(key <= qry, s, neg)
            elif sel is not None:
                s = jnp.where(sel[j:j + 1, :] > 0.5, s, neg)
            scores.append(s)
        return scores

    def block_output(n, i, scores):
        m = scores[0].max(axis=0, keepdims=True)
        for s in scores[1:]:
            m = jnp.maximum(m, s.max(axis=0, keepdims=True))
        acc = jnp.zeros((HEAD_DIM + BF16_ROWS, BS), F32)
        for j, s in enumerate(scores):
            acc = acc + _dot(vt_ref[n, :, j * BS:(j + 1) * BS], jnp.exp2(s - m))
        out = acc[:HEAD_DIM, :] / acc[HEAD_DIM:HEAD_DIM + 1, :]
        o_ref[i * BS:(i + 1) * BS, heads[n]] = out.T.astype(o_ref.dtype)

    tasks = [(n, i) for i in range(nb) for n in range(MOBA_HEADS_PER_STEP)]
    pending = {t: block_scores(*tasks[t]) for t in range(min(MOBA_LOOKAHEAD, len(tasks)))}
    for t, (n, i) in enumerate(tasks):
        if t + MOBA_LOOKAHEAD < len(tasks):
            pending[t + MOBA_LOOKAHEAD] = block_scores(*tasks[t + MOBA_LOOKAHEAD])
        block_output(n, i, pending.pop(t))


def _moba(mq, mk, mv):
    B, T, _ = mq.shape
    assert T % MOBA_BLOCK == 0 and MOBA_HEADS % MOBA_HEADS_PER_STEP == 0
    width = MOBA_HEADS_PER_STEP * HEAD_DIM
    head = pl.BlockSpec((None, T, width), lambda b, h: (b, 0, h))
    return pl.pallas_call(
        functools.partial(_moba_body, T=T),
        grid=(B, MOBA_HEADS // MOBA_HEADS_PER_STEP),
        in_specs=[head] * 3,
        out_specs=head,
        out_shape=jax.ShapeDtypeStruct((B, T, MOBA_WIDTH), BF16),
        scratch_shapes=[pltpu.VMEM((MOBA_HEADS_PER_STEP, HEAD_DIM + BF16_ROWS, T), BF16)],
        compiler_params=pltpu.CompilerParams(
            dimension_semantics=("arbitrary", "arbitrary"), vmem_limit_bytes=VMEM_LIMIT),
        name="moba",
    )(mq, mk, mv)


def _mlp_body(og_ref, om_ref, x_ref, p_ref, wo_ref, n1_ref, n2_ref, wu_ref, wd_ref, n3_ref,
              wg_ref, wp_ref, o_ref, *, ff_chunk):
    parts = [slice(r0, r0 + MLP_PART_ROWS) for r0 in range(0, x_ref.shape[0], MLP_PART_ROWS)]
    ns = range(len(parts))
    mix = [_dot(og_ref[r, :], wo_ref[0:GDN_WIDTH, :])
           + _dot(om_ref[r, :], wo_ref[GDN_WIDTH:GDN_WIDTH + MOBA_WIDTH, :]) for r in parts]
    h = [x_ref[r, :] + _rms(mix[n], n1_ref[...]) for n, r in enumerate(parts)]
    a = [_rms(h[n], n2_ref[...]).astype(BF16) for n in ns]
    f = [None for _ in ns]
    for c0 in range(0, D_FF, ff_chunk):
        up = [jnp.dot(a[n], wu_ref[:, c0:c0 + ff_chunk], preferred_element_type=F32)
              for n in ns]
        for n in ns:
            part = _dot(jnp.square(jnp.maximum(up[n], 0.0)), wd_ref[c0:c0 + ff_chunk, :])
            f[n] = part if f[n] is None else f[n] + part
    h = [h[n] + _rms(f[n], n3_ref[...]) for n in ns]
    gate = [_dot(h[n], wg_ref[...]) for n in ns]
    ple = [_dot(p_ref[r, :], wp_ref[...]) for r in parts]
    for n, r in enumerate(parts):
        o_ref[r, :] = h[n] + _sigmoid(gate[n]) * ple[n]


def _mlp(og, om, x, p, wo, n1, n2, wu, wd, n3, wg, wp, *, tm):
    B, T, D = x.shape
    tok = lambda width: pl.BlockSpec((None, tm, width), lambda b, i: (b, i, 0))
    const = lambda shape: pl.BlockSpec(shape, lambda b, i: (0,) * len(shape),
                                       pipeline_mode=pl.Buffered(1))
    return pl.pallas_call(
        functools.partial(_mlp_body, ff_chunk=1024),
        grid=(B, T // tm),
        in_specs=[tok(GDN_WIDTH), tok(MOBA_WIDTH), tok(D), tok(PLE_DIM),
                  const((D, D)), const((1, D)), const((1, D)), const((D, D_FF)),
                  const((D_FF, D)), const((1, D)), const((D, D)), const((PLE_DIM, D))],
        out_specs=tok(D),
        out_shape=jax.ShapeDtypeStruct((B, T, D), F32),
        compiler_params=pltpu.CompilerParams(
            dimension_semantics=("arbitrary", "arbitrary"), vmem_limit_bytes=VMEM_LIMIT),
        name="mlp",
    )(og, om, x, p, wo, n1, n2, wu, wd, n3, wg, wp)


def _rope_tables(T):
    half = ROPE_DIMS // 2
    inv_freq = ROPE_THETA ** (-np.arange(half, dtype=np.float64) * (2.0 / ROPE_DIMS))
    ang = np.arange(T, dtype=np.float64)[:, None] * inv_freq[None, :]
    cos, sin = np.cos(ang), np.sin(ang)
    zeros = np.zeros((T, HEAD_DIM - ROPE_DIMS))
    z_half = np.zeros((T, half))
    cos_t = np.concatenate([cos, cos, np.ones((T, HEAD_DIM - ROPE_DIMS))], axis=-1)
    sa_t = np.concatenate([-sin, z_half, zeros], axis=-1)
    sb_t = np.concatenate([z_half, sin, zeros], axis=-1)
    cos_t, sa_t, sb_t = (jnp.asarray(t, dtype=F32) for t in (cos_t, sa_t, sb_t))
    return cos_t, sa_t, sb_t


def _layer(h, p_i, w_in, layer, conv_w, a_log, dt_bias, gdn_norm_w, w_out, attn_pre_norm,
           attn_post_norm, mlp_pre_norm, mlp_post_norm, w_up, w_down, w_ple, w_ple_gate, tables):
    B, T, D = h.shape
    tm = next(t for t in (512, MOBA_BLOCK) if T % t == 0)
    assert T % tm == 0 and T % MOBA_BLOCK == 0 and T % GDN_CHUNK == 0
    nh = 2 * GDN_HEADS
    gparams = jnp.pad(jnp.stack([a_log.astype(F32), dt_bias.astype(F32)]),
                      ((0, 0), (GDN_HEADS, LANES - nh)))
    row = lambda v: v.reshape(1, -1).astype(F32)

    gq, gk, gv, gz, bg, mq, mk, mv = _in_proj(
        h, row(attn_pre_norm), jnp.swapaxes(w_in.astype(F32), 1, 2), layer, conv_w.astype(F32),
        gparams, *tables,
        tm=next(t for t in (1024, 512, MOBA_BLOCK) if T % t == 0))
    o_gdn, (wo, wu, wd, wg, wp) = _gdn(
        gq, gk, gv, gz, bg, row(gdn_norm_w),
        [w.astype(F32) for w in (w_out, w_up, w_down, w_ple_gate, w_ple)])
    o_moba = _moba(mq, mk, mv)
    return _mlp(o_gdn, o_moba, h, p_i, wo, row(attn_post_norm), row(mlp_pre_norm), wu, wd,
                row(mlp_post_norm), wg, wp, tm=tm)


def kernel(x, p, w_in, conv_w, a_log, dt_bias, gdn_norm_w, w_out, attn_pre_norm, attn_post_norm,
           mlp_pre_norm, mlp_post_norm, w_up, w_down, w_ple, w_ple_gate):
    tables = _rope_tables(x.shape[1])
    h = x
    for i in range(w_in.shape[0]):
        h = _layer(h, p[i], w_in, i, conv_w[i], a_log[i], dt_bias[i], gdn_norm_w[i], w_out[i],
                   attn_pre_norm[i], attn_post_norm[i], mlp_pre_norm[i], mlp_post_norm[i],
                   w_up[i], w_down[i], w_ple[i], w_ple_gate[i], tables)
    return h
```

```python
import functools
import math

import jax
import jax.numpy as jnp
import numpy as np
from jax import lax
from jax.experimental import pallas as pl
from jax.experimental.pallas import tpu as pltpu

F32 = jnp.float32
BF16 = jnp.bfloat16

D_MODEL = 1024
HEAD_DIM = 128
GDN_HEADS = 4
GDN_WIDTH = GDN_HEADS * HEAD_DIM
GDN_CONV = 4
GDN_CHUNK = 64
GDN_GROUP = 4
GDN_WAVES = 2
GDN_WAVE_LAG = 1
GDN_RING = 3
MOBA_HEADS = 4
MOBA_WIDTH = MOBA_HEADS * HEAD_DIM
MOBA_BLOCK = 256
MOBA_TOPK = 3
MOBA_HEADS_PER_STEP = 4
MOBA_LOOKAHEAD = 3
ROPE_DIMS = 32
ROPE_THETA = 500000.0
D_FF = 4 * D_MODEL
PLE_DIM = 256
RMS_EPS = 1e-6
LANES = 128
SUBLANES = 8
BF16_ROWS = 16
CONV_HALO = 8
REPACK_ROWS = 128
MLP_PART_ROWS = 256
IN_PART_ROWS = 128
IN_PARTS_TOGETHER = 1
QK_SCALE = HEAD_DIM ** -0.5
MOBA_Q_SCALE = QK_SCALE * math.log2(math.e)
VMEM_LIMIT = 56 * 1024 * 1024

C_GQKV = 0
C_GZ = 3 * GDN_WIDTH
C_MQ = 4 * GDN_WIDTH
C_MK = C_MQ + MOBA_WIDTH
C_MV = C_MK + MOBA_WIDTH
C_BA = C_MV + MOBA_WIDTH
C_END = C_BA + LANES


def _rms(x, w):
    return x * lax.rsqrt(jnp.mean(x * x, axis=-1, keepdims=True) + RMS_EPS) * w


def _sigmoid(x):
    return 1.0 / (1.0 + jnp.exp(-x))


def _aligned(x, m):
    return x if isinstance(x, int) else pl.multiple_of(x, m)


def _dot(a, b):
    return jnp.dot(a.astype(BF16), b.astype(BF16), preferred_element_type=F32)


def _dot_nt(a, b):
    return lax.dot_general(a.astype(BF16), b.astype(BF16), (((1,), (1,)), ((), ())),
                           preferred_element_type=F32)


def _in_proj_body(x_ref, nw_ref, win_ref, cw_ref, gp_ref, cos_ref, sa_ref, sb_ref,
                  gq_ref, gk_ref, gv_ref, gz_ref, bg_ref, mq_ref, mk_ref, mv_ref,
                  halo, w_ref, *, tm):
    i = pl.program_id(1)

    @pl.when((pl.program_id(0) == 0) & (i == 0))
    def _():
        n_gate = 2 * GDN_HEADS
        for c0 in range(0, C_BA, REPACK_ROWS):
            src = c0 if c0 < C_MQ else c0 + n_gate
            w_ref[:, c0:c0 + REPACK_ROWS] = win_ref[src:src + REPACK_ROWS, :].T.astype(BF16)
        gate_rows = jnp.concatenate([win_ref[C_MQ:C_MQ + n_gate, :],
                                     jnp.zeros((LANES - n_gate, D_MODEL), F32)], axis=0)
        w_ref[:, C_BA:C_END] = gate_rows.T.astype(BF16)

    parts = [slice(r0, r0 + IN_PART_ROWS) for r0 in range(0, tm, IN_PART_ROWS)]
    ns = []
    u = [None for _ in parts]

    def proj(n, c0, width):
        if u[n] is None:
            u[n] = _rms(x_ref[parts[n], :], nw_ref[...]).astype(BF16)
        return jnp.dot(u[n], w_ref[:, c0:c0 + width], preferred_element_type=F32)

    @pl.when(i == 0)
    def _():
        halo[...] = jnp.zeros_like(halo)

    pair_w = 2 * HEAD_DIM

    def conv_pair(p):
        pairs = {n: proj(n, C_GQKV + p * pair_w, pair_w) for n in ns}
        for s in (2 * p, 2 * p + 1):
            c0 = s * HEAD_DIM
            cw = [cw_ref[j:j + 1, c0:c0 + HEAD_DIM] for j in range(GDN_CONV)]
            which, h = divmod(s, GDN_HEADS)
            cur = {n: pairs[n][:, (s % 2) * HEAD_DIM:(s % 2 + 1) * HEAD_DIM] for n in ns}
            prev = halo[:, c0:c0 + HEAD_DIM]
            halo[:, c0:c0 + HEAD_DIM] = cur[ns[-1]][IN_PART_ROWS - CONV_HALO:, :]
            for n in ns:
                xs = jnp.concatenate([prev, cur[n]], axis=0)
                prev = cur[n][IN_PART_ROWS - CONV_HALO:, :]
                x1 = pltpu.roll(xs, 1, 0)
                acc = (cw[3] * xs + cw[2] * x1) + pltpu.roll(cw[1] * xs + cw[0] * x1, 2, 0)
                acc = acc[CONV_HALO:, :]
                y = acc * _sigmoid(acc)
                if which < 2:
                    inv_norm = lax.rsqrt(jnp.sum(y * y, axis=-1, keepdims=True) + RMS_EPS)
                    y = y * (inv_norm * QK_SCALE if which == 0 else inv_norm)
                (gq_ref, gk_ref, gv_ref)[which][parts[n], h * HEAD_DIM:(h + 1) * HEAD_DIM] = (
                    y.astype(BF16))

    def plain_pair(ref, c_base, p):
        vals = {n: proj(n, c_base + p * pair_w, pair_w) for n in ns}
        for n in ns:
            ref[parts[n], p * pair_w:(p + 1) * pair_w] = vals[n].astype(BF16)

    def rotary_pair(ref, c_base, scale, p):
        half = ROPE_DIMS // 2
        pairs = {n: proj(n, c_base + p * pair_w, pair_w) for n in ns}
        for n in ns:
            cos, sa, sb = cos_ref[parts[n], :], sa_ref[parts[n], :], sb_ref[parts[n], :]
            for h in (2 * p, 2 * p + 1):
                xr = pairs[n][:, (h % 2) * HEAD_DIM:(h % 2 + 1) * HEAD_DIM]
                rot = (xr * cos + pltpu.roll(xr, LANES - half, 1) * sa
                       + pltpu.roll(xr, half, 1) * sb)
                if scale is not None:
                    rot = rot * scale
                ref[parts[n], h * HEAD_DIM:(h + 1) * HEAD_DIM] = rot.astype(BF16)

    def gates():
        lane = lax.broadcasted_iota(jnp.int32, (IN_PART_ROWS, LANES), 1)
        bas = {n: proj(n, C_BA, LANES) for n in ns}
        for n in ns:
            xg = bas[n] + gp_ref[1:2, :]
            e = jnp.exp(-jnp.abs(xg))
            r = 1.0 / (1.0 + e)
            sig = jnp.where(xg >= 0.0, r, e * r)
            g = -jnp.exp(gp_ref[0:1, :]) * (jnp.maximum(xg, 0.0) + jnp.log1p(e))
            bg_ref[parts[n], :] = jnp.where(lane < GDN_HEADS, sig,
                                            jnp.where(lane < 2 * GDN_HEADS, g, 0.0))

    convs = [functools.partial(conv_pair, p) for p in range(3 * GDN_WIDTH // pair_w)]
    light = ([functools.partial(plain_pair, gz_ref, C_GZ, p) for p in range(GDN_WIDTH // pair_w)]
             + [functools.partial(plain_pair, mv_ref, C_MV, p) for p in range(MOBA_WIDTH // pair_w)]
             + [functools.partial(rotary_pair, mq_ref, C_MQ, MOBA_Q_SCALE, p)
                for p in range(MOBA_WIDTH // pair_w)]
             + [functools.partial(rotary_pair, mk_ref, C_MK, None, p)
                for p in range(MOBA_WIDTH // pair_w)]
             + [gates])
    for first in range(0, len(parts), IN_PARTS_TOGETHER):
        ns[:] = range(first, first + IN_PARTS_TOGETHER)
        done = 0
        for n, conv in enumerate(convs, start=1):
            conv()
            upto = len(light) * n // len(convs)
            for stage in light[done:upto]:
                stage()
            done = upto


def _in_proj(x, nw, w_in, layer, conv_w, gparams, cos_t, sa_t, sb_t, *, tm):
    B, T, D = x.shape
    tok = lambda width: pl.BlockSpec((None, tm, width), lambda b, i: (b, i, 0))
    const = lambda shape: pl.BlockSpec(shape, lambda b, i: (0,) * len(shape))
    table = pl.BlockSpec((tm, LANES), lambda b, i: (i, 0))
    o512 = jax.ShapeDtypeStruct((B, T, GDN_WIDTH), BF16)
    return pl.pallas_call(
        functools.partial(_in_proj_body, tm=tm),
        grid=(B, T // tm),
        in_specs=[tok(D), const((1, D)),
                  pl.BlockSpec((None,) + w_in.shape[1:], lambda b, i: (layer, 0, 0),
                               pipeline_mode=pl.Buffered(1)),
                  const((GDN_CONV, 3 * GDN_WIDTH)),
                  const((2, LANES)), table, table, table],
        out_specs=[tok(GDN_WIDTH)] * 4 + [tok(LANES)] + [tok(MOBA_WIDTH)] * 3,
        out_shape=[o512] * 4 + [jax.ShapeDtypeStruct((B, T, LANES), F32)] + [o512] * 3,
        scratch_shapes=[pltpu.VMEM((CONV_HALO, 3 * GDN_WIDTH), F32),
                        pltpu.VMEM((D, C_END), BF16)],
        compiler_params=pltpu.CompilerParams(
            dimension_semantics=("arbitrary", "arbitrary"), vmem_limit_bytes=VMEM_LIMIT),
        name="in_proj",
    )(x, nw, w_in, conv_w, gparams, cos_t, sa_t, sb_t)


def _gdn_body(q_ref, k_ref, v_ref, z_ref, bg_ref, nw_ref, *rest, T, n_cast):
    C = GDN_CHUNK
    H = GDN_HEADS
    G = GDN_GROUP
    rows_per_trip = G * C
    ntrips = T // rows_per_trip
    ri = lax.broadcasted_iota(jnp.int32, (C, C), 0)
    ci = lax.broadcasted_iota(jnp.int32, (C, C), 1)
    causal = ci <= ri
    strict = ci < ri
    eye = (ri == ci).astype(F32)
    cols = [slice(h * HEAD_DIM, (h + 1) * HEAD_DIM) for h in range(H)]
    for src, dst in zip(rest[:n_cast], rest[n_cast:2 * n_cast]):
        dst[...] = src[...].astype(dst.dtype)
    o_ref, s_ref, mneg_ref, n_ref, o1_ref, o2_ref, a_ref, snap_ref = rest[2 * n_cast:]
    s_ref[...] = jnp.zeros_like(s_ref)

    def phase_a(row0, slot, chunks):
        inst = [(c, h) for c in chunks for h in range(H)]
        rng = range(len(inst))
        r0s = {c: _aligned(row0 + c * C, C) for c in chunks}
        bgs = {c: bg_ref[pl.ds(r0s[c], C), :] for c in chunks}
        q = [q_ref[pl.ds(r0s[c], C), cols[h]].astype(F32) for c, h in inst]
        k = [k_ref[pl.ds(r0s[c], C), cols[h]].astype(F32) for c, h in inst]
        v = [v_ref[pl.ds(r0s[c], C), cols[h]].astype(F32) for c, h in inst]
        beta = [bgs[c][:, h:h + 1] for c, h in inst]
        gc_col, gc_last, decay = [], [], []
        for c, h in inst:
            gb = jnp.broadcast_to(bgs[c][:, H + h:H + h + 1], (C, C))
            row = jnp.sum(jnp.where(ri <= ci, gb, 0.0), axis=0, keepdims=True)
            g_row = jnp.sum(jnp.where(ri == ci, gb, 0.0), axis=0, keepdims=True)
            col = jnp.sum(jnp.where(causal, jnp.broadcast_to(g_row, (C, C)), 0.0),
                          axis=1, keepdims=True)
            gc_col.append(col)
            gc_last.append(row[:, C - 1:C])
            decay.append(jnp.exp(jnp.where(causal, col - row, -jnp.inf)))
        kb = [k[t] * beta[t] for t in rng]
        eg = [jnp.exp(gc_col[t]) for t in rng]
        kq = [_dot_nt(jnp.concatenate([kb[t], q[t]], axis=0), k[t]) for t in rng]
        yield
        low = [jnp.where(strict, kq[t][:C] * decay[t], 0.0) for t in rng]
        intra = [jnp.where(causal, kq[t][C:] * decay[t], 0.0) for t in rng]
        inv = [eye - low[t] for t in rng]
        pw = [_dot(low[t], low[t]) for t in rng]
        yield
        for _ in range(4):
            r = [_dot(jnp.concatenate([pw[t], inv[t]], axis=0), pw[t]) for t in rng]
            inv = [inv[t] + r[t][C:] for t in rng]
            pw = [r[t][:C] for t in rng]
            yield
        inv = [inv[t] + _dot(inv[t], pw[t]) for t in rng]
        yield
        wu = [_dot(inv[t], jnp.concatenate([kb[t] * eg[t], v[t] * beta[t]], axis=-1))
              for t in rng]
        yield
        kd = [k[t] * jnp.exp(gc_last[t] - gc_col[t]) for t in rng]
        r = [_dot(jnp.concatenate([kd[t].T, intra[t]], axis=0), wu[t]) for t in rng]
        mn = [r[t][:HEAD_DIM] for t in rng]
        io = [r[t][HEAD_DIM:] for t in rng]
        for t, (c, h) in enumerate(inst):
            rows = slice(c * C, (c + 1) * C)
            mneg_ref[slot, c, h] = (-mn[t][:, :HEAD_DIM]).astype(BF16)
            n_ref[slot, c, h] = mn[t][:, HEAD_DIM:]
            o1_ref[slot, rows, cols[h]] = (q[t] * eg[t] - io[t][:, :HEAD_DIM]).astype(BF16)
            o2_ref[slot, rows, cols[h]] = io[t][:, HEAD_DIM:]
            a_ref[slot, c, h] = jnp.broadcast_to(jnp.exp(gc_last[t]), (SUBLANES, HEAD_DIM))
        yield

    def phase_b_step(slot, c):
        hs = range(H)
        state = [s_ref[h] for h in hs]
        sb = [state[h].astype(BF16) for h in hs]
        ms = [jnp.dot(mneg_ref[slot, c, h], sb[h], preferred_element_type=F32) for h in hs]
        for h in hs:
            snap_ref[slot, c, h] = sb[h]
            s_ref[h] = state[h] * a_ref[slot, c, h, 0:1, :] + ms[h] + n_ref[slot, c, h]

    def phase_c(row0, slot, chunks):
        inst = [(c, h) for c in chunks for h in range(H)]
        os_ = [jnp.dot(o1_ref[slot, c * C:(c + 1) * C, cols[h]], snap_ref[slot, c, h],
                       preferred_element_type=F32) for c, h in inst]
        for t, (c, h) in enumerate(inst):
            r0 = _aligned(row0 + c * C, C)
            o = os_[t] + o2_ref[slot, c * C:(c + 1) * C, cols[h]]
            z = z_ref[pl.ds(r0, C), cols[h]].astype(F32)
            o_ref[pl.ds(r0, C), cols[h]] = (_rms(o, nw_ref[...])
                                            * (z * _sigmoid(z))).astype(o_ref.dtype)

    def trip(g, do_a, do_b, do_c):
        row0 = lambda d: _aligned((g - d) * rows_per_trip, rows_per_trip)
        slot = lambda d: (g - d) % GDN_RING
        c_steps = ([functools.partial(phase_c, row0(2), slot(2), (c,)) for c in range(G)]
                   if do_c else [])
        b_steps = [functools.partial(phase_b_step, slot(1), c) for c in range(G)] if do_b else []
        if do_a:
            per_wave = G // GDN_WAVES
            waves = [phase_a(row0(0), slot(0), tuple(range(w * per_wave, (w + 1) * per_wave)))
                     for w in range(GDN_WAVES)]
            n = 0
            while any(w is not None for w in waves):
                for i, w in enumerate(waves):
                    if w is not None and n >= i * GDN_WAVE_LAG:
                        if next(w, "done") == "done":
                            waves[i] = None
                if n % 2 == 0 and c_steps:
                    c_steps.pop(0)()
                if n % 2 == 1 and b_steps:
                    b_steps.pop(0)()
                n += 1
        while b_steps or c_steps:
            for steps in (b_steps, c_steps):
                if steps:
                    steps.pop(0)()

    def steady(g, carry):
        trip(g, True, True, True)
        return carry

    all_on = lambda g: g < ntrips and 1 <= g <= ntrips and 2 <= g <= ntrips + 1
    g = 0
    while g < ntrips + 2:
        if all_on(g):
            last = g
            while all_on(last + 1):
                last += 1
            lax.fori_loop(g, last + 1, steady, 0)
            g = last + 1
        else:
            trip(g, g < ntrips, 1 <= g <= ntrips, 2 <= g <= ntrips + 1)
            g += 1


def _gdn(gq, gk, gv, gz, bg, nw, weights):
    B, T, _ = gq.shape
    banded = all(w.shape[0] % (B * BF16_ROWS) == 0 for w in weights)
    cast_in = list(weights) if banded else []
    band = lambda w: pl.BlockSpec((w.shape[0] // B, w.shape[1]), lambda b: (b, 0))
    rows_per_trip = GDN_GROUP * GDN_CHUNK
    assert T % rows_per_trip == 0
    tok = lambda width: pl.BlockSpec((None, T, width), lambda b: (b, 0, 0))
    per_chunk = (GDN_RING, GDN_GROUP, GDN_HEADS)
    outs = pl.pallas_call(
        functools.partial(_gdn_body, T=T, n_cast=len(cast_in)),
        grid=(B,),
        in_specs=([tok(GDN_WIDTH)] * 4 + [tok(LANES), pl.BlockSpec((1, HEAD_DIM), lambda b: (0, 0))]
                  + [band(w) for w in cast_in]),
        out_specs=[band(w) for w in cast_in] + [tok(GDN_WIDTH)],
        out_shape=([jax.ShapeDtypeStruct(w.shape, BF16) for w in cast_in]
                   + [jax.ShapeDtypeStruct((B, T, GDN_WIDTH), BF16)]),
        scratch_shapes=[
            pltpu.VMEM((GDN_HEADS, HEAD_DIM, HEAD_DIM), F32),
            pltpu.VMEM(per_chunk + (HEAD_DIM, HEAD_DIM), BF16),
            pltpu.VMEM(per_chunk + (HEAD_DIM, HEAD_DIM), F32),
            pltpu.VMEM((GDN_RING, rows_per_trip, GDN_WIDTH), BF16),
            pltpu.VMEM((GDN_RING, rows_per_trip, GDN_WIDTH), F32),
            pltpu.VMEM(per_chunk + (SUBLANES, HEAD_DIM), F32),
            pltpu.VMEM(per_chunk + (HEAD_DIM, HEAD_DIM), BF16),
        ],
        compiler_params=pltpu.CompilerParams(
            dimension_semantics=("arbitrary",), vmem_limit_bytes=VMEM_LIMIT),
        name="gdn",
    )(gq, gk, gv, gz, bg, nw, *cast_in)
    cast = outs[:-1] if banded else [w.astype(BF16) for w in weights]
    return outs[-1], cast


def _moba_body(q_ref, k_ref, v_ref, o_ref, vt_ref, *, T):
    BS = MOBA_BLOCK
    nb = T // BS
    nrow = -(-nb // SUBLANES) * SUBLANES
    neg = -jnp.inf
    heads = [slice(h * HEAD_DIM, (h + 1) * HEAD_DIM) for h in range(MOBA_HEADS_PER_STEP)]
    blk = lax.broadcasted_iota(jnp.int32, (nrow, BS), 0)
    key = lax.broadcasted_iota(jnp.int32, (BS, BS), 0)
    qry = lax.broadcasted_iota(jnp.int32, (BS, BS), 1)

    kmean = []
    for n, hd in enumerate(heads):
        means = [jnp.mean(k_ref[j * BS:(j + 1) * BS, hd].astype(F32), axis=0, keepdims=True)
                 for j in range(nb)]
        if nrow > nb:
            means.append(jnp.zeros((nrow - nb, HEAD_DIM), F32))
        kmean.append(jnp.concatenate(means, axis=0))
        for j in range(nb):
            vt_ref[n, 0:HEAD_DIM, j * BS:(j + 1) * BS] = v_ref[j * BS:(j + 1) * BS, hd].T
        vt_ref[n, HEAD_DIM:, :] = jnp.ones((BF16_ROWS, T), BF16)

    def block_scores(n, i):
        q = q_ref[i * BS:(i + 1) * BS, heads[n]]
        sel = None
        if i > MOBA_TOPK:
            gate = lax.dot_general(kmean[n], q.astype(F32), (((1,), (1,)), ((), ())),
                                   precision=lax.Precision.HIGHEST, preferred_element_type=F32)
            gate = jnp.where(blk < i, gate, neg)
            rank = jnp.zeros((nrow, BS), F32)
            for jp in range(i):
                gj = gate[jp:jp + 1, :]
                beats = (gj > gate) | ((gj == gate) & (blk > jp))
                rank = rank + jnp.where(beats, 1.0, 0.0)
            sel = jnp.where((rank < MOBA_TOPK) & (blk < i), 1.0, 0.0)
        scores = []
        for j in range(i + 1):
            s = _dot_nt(k_ref[j * BS:(j + 1) * BS, heads[n]], q)
            if j == i:
                s = jnp.where(key <= qry, s, neg)
            elif sel is not None:
                s = jnp.where(sel[j:j + 1, :] > 0.5, s, neg)
            scores.append(s)
        return scores

    def block_output(n, i, scores):
        m = scores[0].max(axis=0, keepdims=True)
        for s in scores[1:]:
            m = jnp.maximum(m, s.max(axis=0, keepdims=True))
        acc = jnp.zeros((HEAD_DIM + BF16_ROWS, BS), F32)
        for j, s in enumerate(scores):
            acc = acc + _dot(vt_ref[n, :, j * BS:(j + 1) * BS], jnp.exp2(s - m))
        out = acc[:HEAD_DIM, :] / acc[HEAD_DIM:HEAD_DIM + 1, :]
        o_ref[i * BS:(i + 1) * BS, heads[n]] = out.T.astype(o_ref.dtype)

    tasks = [(n, i) for i in range(nb) for n in range(MOBA_HEADS_PER_STEP)]
    pending = {t: block_scores(*tasks[t]) for t in range(min(MOBA_LOOKAHEAD, len(tasks)))}
    for t, (n, i) in enumerate(tasks):
        if t + MOBA_LOOKAHEAD < len(tasks):
            pending[t + MOBA_LOOKAHEAD] = block_scores(*tasks[t + MOBA_LOOKAHEAD])
        block_output(n, i, pending.pop(t))


def _moba(mq, mk, mv):
    B, T, _ = mq.shape
    assert T % MOBA_BLOCK == 0 and MOBA_HEADS % MOBA_HEADS_PER_STEP == 0
    width = MOBA_HEADS_PER_STEP * HEAD_DIM
    head = pl.BlockSpec((None, T, width), lambda b, h: (b, 0, h))
    return pl.pallas_call(
        functools.partial(_moba_body, T=T),
        grid=(B, MOBA_HEADS // MOBA_HEADS_PER_STEP),
        in_specs=[head] * 3,
        out_specs=head,
        out_shape=jax.ShapeDtypeStruct((B, T, MOBA_WIDTH), BF16),
        scratch_shapes=[pltpu.VMEM((MOBA_HEADS_PER_STEP, HEAD_DIM + BF16_ROWS, T), BF16)],
        compiler_params=pltpu.CompilerParams(
            dimension_semantics=("arbitrary", "arbitrary"), vmem_limit_bytes=VMEM_LIMIT),
        name="moba",
    )(mq, mk, mv)


def _mlp_body(og_ref, om_ref, x_ref, p_ref, wo_ref, n1_ref, n2_ref, wu_ref, wd_ref, n3_ref,
              wg_ref, wp_ref, o_ref, *, ff_chunk):
    parts = [slice(r0, r0 + MLP_PART_ROWS) for r0 in range(0, x_ref.shape[0], MLP_PART_ROWS)]
    ns = range(len(parts))
    mix = [_dot(og_ref[r, :], wo_ref[0:GDN_WIDTH, :])
           + _dot(om_ref[r, :], wo_ref[GDN_WIDTH:GDN_WIDTH + MOBA_WIDTH, :]) for r in parts]
    h = [x_ref[r, :] + _rms(mix[n], n1_ref[...]) for n, r in enumerate(parts)]
    a = [_rms(h[n], n2_ref[...]).astype(BF16) for n in ns]
    f = [None for _ in ns]
    for c0 in range(0, D_FF, ff_chunk):
        up = [jnp.dot(a[n], wu_ref[:, c0:c0 + ff_chunk], preferred_element_type=F32)
              for n in ns]
        for n in ns:
            part = _dot(jnp.square(jnp.maximum(up[n], 0.0)), wd_ref[c0:c0 + ff_chunk, :])
            f[n] = part if f[n] is None else f[n] + part
    h = [h[n] + _rms(f[n], n3_ref[...]) for n in ns]
    gate = [_dot(h[n], wg_ref[...]) for n in ns]
    ple = [_dot(p_ref[r, :], wp_ref[...]) for r in parts]
    for n, r in enumerate(parts):
        o_ref[r, :] = h[n] + _sigmoid(gate[n]) * ple[n]


def _mlp(og, om, x, p, wo, n1, n2, wu, wd, n3, wg, wp, *, tm):
    B, T, D = x.shape
    tok = lambda width: pl.BlockSpec((None, tm, width), lambda b, i: (b, i, 0))
    const = lambda shape: pl.BlockSpec(shape, lambda b, i: (0,) * len(shape),
                                       pipeline_mode=pl.Buffered(1))
    return pl.pallas_call(
        functools.partial(_mlp_body, ff_chunk=1024),
        grid=(B, T // tm),
        in_specs=[tok(GDN_WIDTH), tok(MOBA_WIDTH), tok(D), tok(PLE_DIM),
                  const((D, D)), const((1, D)), const((1, D)), const((D, D_FF)),
                  const((D_FF, D)), const((1, D)), const((D, D)), const((PLE_DIM, D))],
        out_specs=tok(D),
        out_shape=jax.ShapeDtypeStruct((B, T, D), F32),
        compiler_params=pltpu.CompilerParams(
            dimension_semantics=("arbitrary", "arbitrary"), vmem_limit_bytes=VMEM_LIMIT),
        name="mlp",
    )(og, om, x, p, wo, n1, n2, wu, wd, n3, wg, wp)


def _rope_tables(T):
    half = ROPE_DIMS // 2
    inv_freq = ROPE_THETA ** (-np.arange(half, dtype=np.float64) * (2.0 / ROPE_DIMS))
    ang = np.arange(T, dtype=np.float64)[:, None] * inv_freq[None, :]
    cos, sin = np.cos(ang), np.sin(ang)
    zeros = np.zeros((T, HEAD_DIM - ROPE_DIMS))
    z_half = np.zeros((T, half))
    cos_t = np.concatenate([cos, cos, np.ones((T, HEAD_DIM - ROPE_DIMS))], axis=-1)
    sa_t = np.concatenate([-sin, z_half, zeros], axis=-1)
    sb_t = np.concatenate([z_half, sin, zeros], axis=-1)
    cos_t, sa_t, sb_t = (jnp.asarray(t, dtype=F32) for t in (cos_t, sa_t, sb_t))
    return cos_t, sa_t, sb_t


def _layer(h, p_i, w_in, layer, conv_w, a_log, dt_bias, gdn_norm_w, w_out, attn_pre_norm,
           attn_post_norm, mlp_pre_norm, mlp_post_norm, w_up, w_down, w_ple, w_ple_gate, tables):
    B, T, D = h.shape
    tm = next(t for t in (512, MOBA_BLOCK) if T % t == 0)
    assert T % tm == 0 and T % MOBA_BLOCK == 0 and T % GDN_CHUNK == 0
    nh = 2 * GDN_HEADS
    gparams = jnp.pad(jnp.stack([a_log.astype(F32), dt_bias.astype(F32)]),
                      ((0, 0), (GDN_HEADS, LANES - nh)))
    row = lambda v: v.reshape(1, -1).astype(F32)

    gq, gk, gv, gz, bg, mq, mk, mv = _in_proj(
        h, row(attn_pre_norm), jnp.swapaxes(w_in.astype(F32), 1, 2), layer, conv_w.astype(F32),
        gparams, *tables,
        tm=next(t for t in (1024, 512, MOBA_BLOCK) if T % t == 0))
    o_gdn, (wo, wu, wd, wg, wp) = _gdn(
        gq, gk, gv, gz, bg, row(gdn_norm_w),
        [w.astype(F32) for w in (w_out, w_up, w_down, w_ple_gate, w_ple)])
    o_moba = _moba(mq, mk, mv)
    return _mlp(o_gdn, o_moba, h, p_i, wo, row(attn_post_norm), row(mlp_pre_norm), wu, wd,
                row(mlp_post_norm), wg, wp,
                tm=next(t for t in (1024, 512, MOBA_BLOCK) if T % t == 0))


def kernel(x, p, w_in, conv_w, a_log, dt_bias, gdn_norm_w, w_out, attn_pre_norm, attn_post_norm,
           mlp_pre_norm, mlp_post_norm, w_up, w_down, w_ple, w_ple_gate):
    tables = _rope_tables(x.shape[1])
    h = x
    for i in range(w_in.shape[0]):
        h = _layer(h, p[i], w_in, i, conv_w[i], a_log[i], dt_bias[i], gdn_norm_w[i], w_out[i],
                   attn_pre_norm[i], attn_post_norm[i], mlp_pre_norm[i], mlp_post_norm[i],
                   w_up[i], w_down[i], w_ple[i], w_ple_gate[i], tables)
    return h
```

```python
import functools
import math

import jax
import jax.numpy as jnp
import numpy as np
from jax import lax
from jax.experimental import pallas as pl
from jax.experimental.pallas import tpu as pltpu

F32 = jnp.float32
BF16 = jnp.bfloat16

D_MODEL = 1024
HEAD_DIM = 128
GDN_HEADS = 4
GDN_WIDTH = GDN_HEADS * HEAD_DIM
GDN_CONV = 4
GDN_CHUNK = 64
GDN_GROUP = 4
GDN_WAVES = 2
GDN_WAVE_LAG = 1
GDN_RING = 3
MOBA_HEADS = 4
MOBA_WIDTH = MOBA_HEADS * HEAD_DIM
MOBA_BLOCK = 256
MOBA_TOPK = 3
MOBA_HEADS_PER_STEP = 4
MOBA_LOOKAHEAD = 3
ROPE_DIMS = 32
ROPE_THETA = 500000.0
D_FF = 4 * D_MODEL
PLE_DIM = 256
RMS_EPS = 1e-6
LANES = 128
SUBLANES = 8
BF16_ROWS = 16
CONV_HALO = 8
REPACK_ROWS = 128
MLP_PART_ROWS = 256
IN_PART_ROWS = 128
IN_PARTS_TOGETHER = 1
QK_SCALE = HEAD_DIM ** -0.5
MOBA_Q_SCALE = QK_SCALE * math.log2(math.e)
VMEM_LIMIT = 56 * 1024 * 1024

C_GQKV = 0
C_GZ = 3 * GDN_WIDTH
C_MQ = 4 * GDN_WIDTH
C_MK = C_MQ + MOBA_WIDTH
C_MV = C_MK + MOBA_WIDTH
C_BA = C_MV + MOBA_WIDTH
C_END = C_BA + LANES


def _rms(x, w):
    return x * lax.rsqrt(jnp.mean(x * x, axis=-1, keepdims=True) + RMS_EPS) * w


def _sigmoid(x):
    return 1.0 / (1.0 + jnp.exp(-x))


def _aligned(x, m):
    return x if isinstance(x, int) else pl.multiple_of(x, m)


def _dot(a, b):
    return jnp.dot(a.astype(BF16), b.astype(BF16), preferred_element_type=F32)


def _dot_nt(a, b):
    return lax.dot_general(a.astype(BF16), b.astype(BF16), (((1,), (1,)), ((), ())),
                           preferred_element_type=F32)


def _in_proj_body(x_ref, nw_ref, win_ref, cw_ref, gp_ref, cos_ref, sa_ref, sb_ref,
                  gq_ref, gk_ref, gv_ref, gz_ref, bg_ref, mq_ref, mk_ref, mv_ref,
                  halo, w_ref, *, tm):
    i = pl.program_id(1)

    @pl.when((pl.program_id(0) == 0) & (i == 0))
    def _():
        n_gate = 2 * GDN_HEADS
        for c0 in range(0, C_BA, REPACK_ROWS):
            src = c0 if c0 < C_MQ else c0 + n_gate
            w_ref[:, c0:c0 + REPACK_ROWS] = win_ref[src:src + REPACK_ROWS, :].T.astype(BF16)
        gate_rows = jnp.concatenate([win_ref[C_MQ:C_MQ + n_gate, :],
                                     jnp.zeros((LANES - n_gate, D_MODEL), F32)], axis=0)
        w_ref[:, C_BA:C_END] = gate_rows.T.astype(BF16)

    parts = [slice(r0, r0 + IN_PART_ROWS) for r0 in range(0, tm, IN_PART_ROWS)]
    ns = []
    u = [None for _ in parts]

    def proj(n, c0, width):
        if u[n] is None:
            u[n] = _rms(x_ref[parts[n], :], nw_ref[...]).astype(BF16)
        return jnp.dot(u[n], w_ref[:, c0:c0 + width], preferred_element_type=F32)

    @pl.when(i == 0)
    def _():
        halo[...] = jnp.zeros_like(halo)

    pair_w = 2 * HEAD_DIM

    def conv_pair(p):
        pairs = {n: proj(n, C_GQKV + p * pair_w, pair_w) for n in ns}
        for s in (2 * p, 2 * p + 1):
            c0 = s * HEAD_DIM
            cw = [cw_ref[j:j + 1, c0:c0 + HEAD_DIM] for j in range(GDN_CONV)]
            which, h = divmod(s, GDN_HEADS)
            cur = {n: pairs[n][:, (s % 2) * HEAD_DIM:(s % 2 + 1) * HEAD_DIM] for n in ns}
            prev = halo[:, c0:c0 + HEAD_DIM]
            halo[:, c0:c0 + HEAD_DIM] = cur[ns[-1]][IN_PART_ROWS - CONV_HALO:, :]
            for n in ns:
                xs = jnp.concatenate([prev, cur[n]], axis=0)
                prev = cur[n][IN_PART_ROWS - CONV_HALO:, :]
                x1 = pltpu.roll(xs, 1, 0)
                acc = (cw[3] * xs + cw[2] * x1) + pltpu.roll(cw[1] * xs + cw[0] * x1, 2, 0)
                acc = acc[CONV_HALO:, :]
                y = acc * _sigmoid(acc)
                if which < 2:
                    inv_norm = lax.rsqrt(jnp.sum(y * y, axis=-1, keepdims=True) + RMS_EPS)
                    y = y * (inv_norm * QK_SCALE if which == 0 else inv_norm)
                (gq_ref, gk_ref, gv_ref)[which][parts[n], h * HEAD_DIM:(h + 1) * HEAD_DIM] = (
                    y.astype(BF16))

    def plain_pair(ref, c_base, p):
        vals = {n: proj(n, c_base + p * pair_w, pair_w) for n in ns}
        for n in ns:
            ref[parts[n], p * pair_w:(p + 1) * pair_w] = vals[n].astype(BF16)

    def rotary_pair(ref, c_base, scale, p):
        half = ROPE_DIMS // 2
        pairs = {n: proj(n, c_base + p * pair_w, pair_w) for n in ns}
        for n in ns:
            cos, sa, sb = cos_ref[parts[n], :], sa_ref[parts[n], :], sb_ref[parts[n], :]
            for h in (2 * p, 2 * p + 1):
                xr = pairs[n][:, (h % 2) * HEAD_DIM:(h % 2 + 1) * HEAD_DIM]
                rot = (xr * cos + pltpu.roll(xr, LANES - half, 1) * sa
                       + pltpu.roll(xr, half, 1) * sb)
                if scale is not None:
                    rot = rot * scale
                ref[parts[n], h * HEAD_DIM:(h + 1) * HEAD_DIM] = rot.astype(BF16)

    def gates():
        lane = lax.broadcasted_iota(jnp.int32, (IN_PART_ROWS, LANES), 1)
        bas = {n: proj(n, C_BA, LANES) for n in ns}
        for n in ns:
            xg = bas[n] + gp_ref[1:2, :]
            e = jnp.exp(-jnp.abs(xg))
            r = 1.0 / (1.0 + e)
            sig = jnp.where(xg >= 0.0, r, e * r)
            g = -jnp.exp(gp_ref[0:1, :]) * (jnp.maximum(xg, 0.0) + jnp.log1p(e))
            bg_ref[parts[n], :] = jnp.where(lane < GDN_HEADS, sig,
                                            jnp.where(lane < 2 * GDN_HEADS, g, 0.0))

    convs = [functools.partial(conv_pair, p) for p in range(3 * GDN_WIDTH // pair_w)]
    light = ([functools.partial(plain_pair, gz_ref, C_GZ, p) for p in range(GDN_WIDTH // pair_w)]
             + [functools.partial(plain_pair, mv_ref, C_MV, p) for p in range(MOBA_WIDTH // pair_w)]
             + [functools.partial(rotary_pair, mq_ref, C_MQ, MOBA_Q_SCALE, p)
                for p in range(MOBA_WIDTH // pair_w)]
             + [functools.partial(rotary_pair, mk_ref, C_MK, None, p)
                for p in range(MOBA_WIDTH // pair_w)]
             + [gates])
    for first in range(0, len(parts), IN_PARTS_TOGETHER):
        ns[:] = range(first, first + IN_PARTS_TOGETHER)
        done = 0
        for n, conv in enumerate(convs, start=1):
            conv()
            upto = len(light) * n // len(convs)
            for stage in light[done:upto]:
                stage()
            done = upto


def _in_proj(x, nw, w_in, layer, conv_w, gparams, cos_t, sa_t, sb_t, *, tm):
    B, T, D = x.shape
    tok = lambda width: pl.BlockSpec((None, tm, width), lambda b, i: (b, i, 0))
    const = lambda shape: pl.BlockSpec(shape, lambda b, i: (0,) * len(shape))
    table = pl.BlockSpec((tm, LANES), lambda b, i: (i, 0))
    o512 = jax.ShapeDtypeStruct((B, T, GDN_WIDTH), BF16)
    return pl.pallas_call(
        functools.partial(_in_proj_body, tm=tm),
        grid=(B, T // tm),
        in_specs=[tok(D), const((1, D)),
                  pl.BlockSpec((None,) + w_in.shape[1:], lambda b, i: (layer, 0, 0),
                               pipeline_mode=pl.Buffered(1)),
                  const((GDN_CONV, 3 * GDN_WIDTH)),
                  const((2, LANES)), table, table, table],
        out_specs=[tok(GDN_WIDTH)] * 4 + [tok(LANES)] + [tok(MOBA_WIDTH)] * 3,
        out_shape=[o512] * 4 + [jax.ShapeDtypeStruct((B, T, LANES), F32)] + [o512] * 3,
        scratch_shapes=[pltpu.VMEM((CONV_HALO, 3 * GDN_WIDTH), F32),
                        pltpu.VMEM((D, C_END), BF16)],
        compiler_params=pltpu.CompilerParams(
            dimension_semantics=("arbitrary", "arbitrary"), vmem_limit_bytes=VMEM_LIMIT),
        name="in_proj",
    )(x, nw, w_in, conv_w, gparams, cos_t, sa_t, sb_t)


def _gdn_body(q_ref, k_ref, v_ref, z_ref, bg_ref, nw_ref, *rest, T, n_cast):
    C = GDN_CHUNK
    H = GDN_HEADS
    G = GDN_GROUP
    rows_per_trip = G * C
    ntrips = T // rows_per_trip
    ri = lax.broadcasted_iota(jnp.int32, (C, C), 0)
    ci = lax.broadcasted_iota(jnp.int32, (C, C), 1)
    causal = ci <= ri
    strict = ci < ri
    eye = (ri == ci).astype(F32)
    cols = [slice(h * HEAD_DIM, (h + 1) * HEAD_DIM) for h in range(H)]
    for src, dst in zip(rest[:n_cast], rest[n_cast:2 * n_cast]):
        dst[...] = src[...].astype(dst.dtype)
    o_ref, s_ref, mneg_ref, n_ref, o1_ref, o2_ref, a_ref, snap_ref = rest[2 * n_cast:]
    s_ref[...] = jnp.zeros_like(s_ref)

    def phase_a(row0, slot, chunks):
        inst = [(c, h) for c in chunks for h in range(H)]
        rng = range(len(inst))
        r0s = {c: _aligned(row0 + c * C, C) for c in chunks}
        bgs = {c: bg_ref[pl.ds(r0s[c], C), :] for c in chunks}
        q = [q_ref[pl.ds(r0s[c], C), cols[h]].astype(F32) for c, h in inst]
        k = [k_ref[pl.ds(r0s[c], C), cols[h]].astype(F32) for c, h in inst]
        v = [v_ref[pl.ds(r0s[c], C), cols[h]].astype(F32) for c, h in inst]
        beta = [bgs[c][:, h:h + 1] for c, h in inst]
        gc_col, gc_last, decay = [], [], []
        for c, h in inst:
            gb = jnp.broadcast_to(bgs[c][:, H + h:H + h + 1], (C, C))
            row = jnp.sum(jnp.where(ri <= ci, gb, 0.0), axis=0, keepdims=True)
            g_row = jnp.sum(jnp.where(ri == ci, gb, 0.0), axis=0, keepdims=True)
            col = jnp.sum(jnp.where(causal, jnp.broadcast_to(g_row, (C, C)), 0.0),
                          axis=1, keepdims=True)
            gc_col.append(col)
            gc_last.append(row[:, C - 1:C])
            decay.append(jnp.exp(jnp.where(causal, col - row, -jnp.inf)))
        kb = [k[t] * beta[t] for t in rng]
        eg = [jnp.exp(gc_col[t]) for t in rng]
        kq = [_dot_nt(jnp.concatenate([kb[t], q[t]], axis=0), k[t]) for t in rng]
        yield
        low = [jnp.where(strict, kq[t][:C] * decay[t], 0.0) for t in rng]
        intra = [jnp.where(causal, kq[t][C:] * decay[t], 0.0) for t in rng]
        inv = [eye - low[t] for t in rng]
        pw = [_dot(low[t], low[t]) for t in rng]
        yield
        for _ in range(4):
            r = [_dot(jnp.concatenate([pw[t], inv[t]], axis=0), pw[t]) for t in rng]
            inv = [inv[t] + r[t][C:] for t in rng]
            pw = [r[t][:C] for t in rng]
            yield
        inv = [inv[t] + _dot(inv[t], pw[t]) for t in rng]
        yield
        wu = [_dot(inv[t], jnp.concatenate([kb[t] * eg[t], v[t] * beta[t]], axis=-1))
              for t in rng]
        yield
        kd = [k[t] * jnp.exp(gc_last[t] - gc_col[t]) for t in rng]
        r = [_dot(jnp.concatenate([kd[t].T, intra[t]], axis=0), wu[t]) for t in rng]
        mn = [r[t][:HEAD_DIM] for t in rng]
        io = [r[t][HEAD_DIM:] for t in rng]
        for t, (c, h) in enumerate(inst):
            rows = slice(c * C, (c + 1) * C)
            mneg_ref[slot, c, h] = (-mn[t][:, :HEAD_DIM]).astype(BF16)
            n_ref[slot, c, h] = mn[t][:, HEAD_DIM:]
            o1_ref[slot, rows, cols[h]] = (q[t] * eg[t] - io[t][:, :HEAD_DIM]).astype(BF16)
            o2_ref[slot, rows, cols[h]] = io[t][:, HEAD_DIM:]
            a_ref[slot, c, h] = jnp.broadcast_to(jnp.exp(gc_last[t]), (SUBLANES, HEAD_DIM))
        yield

    def phase_b_step(slot, c):
        hs = range(H)
        state = [s_ref[h] for h in hs]
        sb = [state[h].astype(BF16) for h in hs]
        ms = [jnp.dot(mneg_ref[slot, c, h], sb[h], preferred_element_type=F32) for h in hs]
        for h in hs:
            snap_ref[slot, c, h] = sb[h]
            s_ref[h] = state[h] * a_ref[slot, c, h, 0:1, :] + ms[h] + n_ref[slot, c, h]

    def phase_c(row0, slot, chunks):
        inst = [(c, h) for c in chunks for h in range(H)]
        os_ = [jnp.dot(o1_ref[slot, c * C:(c + 1) * C, cols[h]], snap_ref[slot, c, h],
                       preferred_element_type=F32) for c, h in inst]
        for t, (c, h) in enumerate(inst):
            r0 = _aligned(row0 + c * C, C)
            o = os_[t] + o2_ref[slot, c * C:(c + 1) * C, cols[h]]
            z = z_ref[pl.ds(r0, C), cols[h]].astype(F32)
            o_ref[pl.ds(r0, C), cols[h]] = (_rms(o, nw_ref[...])
                                            * (z * _sigmoid(z))).astype(o_ref.dtype)

    def trip(g, do_a, do_b, do_c):
        row0 = lambda d: _aligned((g - d) * rows_per_trip, rows_per_trip)
        slot = lambda d: (g - d) % GDN_RING
        c_steps = ([functools.partial(phase_c, row0(2), slot(2), (c,)) for c in range(G)]
                   if do_c else [])
        b_steps = [functools.partial(phase_b_step, slot(1), c) for c in range(G)] if do_b else []
        if do_a:
            per_wave = G // GDN_WAVES
            waves = [phase_a(row0(0), slot(0), tuple(range(w * per_wave, (w + 1) * per_wave)))
                     for w in range(GDN_WAVES)]
            n = 0
            while any(w is not None for w in waves):
                for i, w in enumerate(waves):
                    if w is not None and n >= i * GDN_WAVE_LAG:
                        if next(w, "done") == "done":
                            waves[i] = None
                if n % 2 == 0 and c_steps:
                    c_steps.pop(0)()
                if n % 2 == 1 and b_steps:
                    b_steps.pop(0)()
                n += 1
        while b_steps or c_steps:
            for steps in (b_steps, c_steps):
                if steps:
                    steps.pop(0)()

    def steady(g, carry):
        trip(g, True, True, True)
        return carry

    all_on = lambda g: g < ntrips and 1 <= g <= ntrips and 2 <= g <= ntrips + 1
    g = 0
    while g < ntrips + 2:
        if all_on(g):
            last = g
            while all_on(last + 1):
                last += 1
            lax.fori_loop(g, last + 1, steady, 0)
            g = last + 1
        else:
            trip(g, g < ntrips, 1 <= g <= ntrips, 2 <= g <= ntrips + 1)
            g += 1


def _gdn(gq, gk, gv, gz, bg, nw, weights):
    B, T, _ = gq.shape
    banded = all(w.shape[0] % (B * BF16_ROWS) == 0 for w in weights)
    cast_in = list(weights) if banded else []
    band = lambda w: pl.BlockSpec((w.shape[0] // B, w.shape[1]), lambda b: (b, 0))
    rows_per_trip = GDN_GROUP * GDN_CHUNK
    assert T % rows_per_trip == 0
    tok = lambda width: pl.BlockSpec((None, T, width), lambda b: (b, 0, 0))
    per_chunk = (GDN_RING, GDN_GROUP, GDN_HEADS)
    outs = pl.pallas_call(
        functools.partial(_gdn_body, T=T, n_cast=len(cast_in)),
        grid=(B,),
        in_specs=([tok(GDN_WIDTH)] * 4 + [tok(LANES), pl.BlockSpec((1, HEAD_DIM), lambda b: (0, 0))]
                  + [band(w) for w in cast_in]),
        out_specs=[band(w) for w in cast_in] + [tok(GDN_WIDTH)],
        out_shape=([jax.ShapeDtypeStruct(w.shape, BF16) for w in cast_in]
                   + [jax.ShapeDtypeStruct((B, T, GDN_WIDTH), BF16)]),
        scratch_shapes=[
            pltpu.VMEM((GDN_HEADS, HEAD_DIM, HEAD_DIM), F32),
            pltpu.VMEM(per_chunk + (HEAD_DIM, HEAD_DIM), BF16),
            pltpu.VMEM(per_chunk + (HEAD_DIM, HEAD_DIM), F32),
            pltpu.VMEM((GDN_RING, rows_per_trip, GDN_WIDTH), BF16),
            pltpu.VMEM((GDN_RING, rows_per_trip, GDN_WIDTH), F32),
            pltpu.VMEM(per_chunk + (SUBLANES, HEAD_DIM), F32),
            pltpu.VMEM(per_chunk + (HEAD_DIM, HEAD_DIM), BF16),
        ],
        compiler_params=pltpu.CompilerParams(
            dimension_semantics=("arbitrary",), vmem_limit_bytes=VMEM_LIMIT),
        name="gdn",
    )(gq, gk, gv, gz, bg, nw, *cast_in)
    cast = outs[:-1] if banded else [w.astype(BF16) for w in weights]
    return outs[-1], cast


def _moba_body(q_ref, k_ref, v_ref, o_ref, vt_ref, *, T):
    BS = MOBA_BLOCK
    nb = T // BS
    nrow = -(-nb // SUBLANES) * SUBLANES
    neg = -jnp.inf
    heads = [slice(h * HEAD_DIM, (h + 1) * HEAD_DIM) for h in range(MOBA_HEADS_PER_STEP)]
    blk = lax.broadcasted_iota(jnp.int32, (nrow, BS), 0)
    key = lax.broadcasted_iota(jnp.int32, (BS, BS), 0)
    qry = lax.broadcasted_iota(jnp.int32, (BS, BS), 1)

    kmean = []
    for n, hd in enumerate(heads):
        means = [jnp.mean(k_ref[j * BS:(j + 1) * BS, hd].astype(F32), axis=0, keepdims=True)
                 for j in range(nb)]
        if nrow > nb:
            means.append(jnp.zeros((nrow - nb, HEAD_DIM), F32))
        kmean.append(jnp.concatenate(means, axis=0))
        for j in range(nb):
            vt_ref[n, 0:HEAD_DIM, j * BS:(j + 1) * BS] = v_ref[j * BS:(j + 1) * BS, hd].T
        vt_ref[n, HEAD_DIM:, :] = jnp.ones((BF16_ROWS, T), BF16)

    def block_scores(n, i):
        q = q_ref[i * BS:(i + 1) * BS, heads[n]]
        sel = None
        if i > MOBA_TOPK:
            gate = lax.dot_general(kmean[n], q.astype(F32), (((1,), (1,)), ((), ())),
                                   precision=lax.Precision.HIGHEST, preferred_element_type=F32)
            gate = jnp.where(blk < i, gate, neg)
            rank = jnp.zeros((nrow, BS), F32)
            for jp in range(i):
                gj = gate[jp:jp + 1, :]
                beats = (gj > gate) | ((gj == gate) & (blk > jp))
                rank = rank + jnp.where(beats, 1.0, 0.0)
            sel = jnp.where((rank < MOBA_TOPK) & (blk < i), 1.0, 0.0)
        scores = []
        for j in range(i + 1):
            s = _dot_nt(k_ref[j * BS:(j + 1) * BS, heads[n]], q)
            if j == i:
                s = jnp.where(key <= qry, s, neg)
            scores.append(s)
        return scores, sel

    def block_output(n, i, scores_sel):
        scores, sel = scores_sel
        keep = [None if (sel is None or j == i) else sel[j:j + 1, :] > 0.5 for j in range(i + 1)]
        m = None
        for j, s in enumerate(scores):
            mj = s.max(axis=0, keepdims=True)
            if keep[j] is not None:
                mj = jnp.where(keep[j], mj, neg)
            m = mj if m is None else jnp.maximum(m, mj)
        acc = jnp.zeros((HEAD_DIM + BF16_ROWS, BS), F32)
        for j, s in enumerate(scores):
            shift = m if keep[j] is None else jnp.where(keep[j], m, jnp.inf)
            acc = acc + _dot(vt_ref[n, :, j * BS:(j + 1) * BS], jnp.exp2(s - shift))
        out = acc[:HEAD_DIM, :] / acc[HEAD_DIM:HEAD_DIM + 1, :]
        o_ref[i * BS:(i + 1) * BS, heads[n]] = out.T.astype(o_ref.dtype)

    tasks = [(n, i) for i in range(nb) for n in range(MOBA_HEADS_PER_STEP)]
    pending = {t: block_scores(*tasks[t]) for t in range(min(MOBA_LOOKAHEAD, len(tasks)))}
    for t, (n, i) in enumerate(tasks):
        if t + MOBA_LOOKAHEAD < len(tasks):
            pending[t + MOBA_LOOKAHEAD] = block_scores(*tasks[t + MOBA_LOOKAHEAD])
        block_output(n, i, pending.pop(t))


def _moba(mq, mk, mv):
    B, T, _ = mq.shape
    assert T % MOBA_BLOCK == 0 and MOBA_HEADS % MOBA_HEADS_PER_STEP == 0
    width = MOBA_HEADS_PER_STEP * HEAD_DIM
    head = pl.BlockSpec((None, T, width), lambda b, h: (b, 0, h))
    return pl.pallas_call(
        functools.partial(_moba_body, T=T),
        grid=(B, MOBA_HEADS // MOBA_HEADS_PER_STEP),
        in_specs=[head] * 3,
        out_specs=head,
        out_shape=jax.ShapeDtypeStruct((B, T, MOBA_WIDTH), BF16),
        scratch_shapes=[pltpu.VMEM((MOBA_HEADS_PER_STEP, HEAD_DIM + BF16_ROWS, T), BF16)],
        compiler_params=pltpu.CompilerParams(
            dimension_semantics=("arbitrary", "arbitrary"), vmem_limit_bytes=VMEM_LIMIT),
        name="moba",
    )(mq, mk, mv)


def _mlp_body(og_ref, om_ref, x_ref, p_ref, wo_ref, n1_ref, n2_ref, wu_ref, wd_ref, n3_ref,
              wg_ref, wp_ref, o_ref, *, ff_chunk):
    parts = [slice(r0, r0 + MLP_PART_ROWS) for r0 in range(0, x_ref.shape[0], MLP_PART_ROWS)]
    ns = range(len(parts))
    mix = [_dot(og_ref[r, :], wo_ref[0:GDN_WIDTH, :])
           + _dot(om_ref[r, :], wo_ref[GDN_WIDTH:GDN_WIDTH + MOBA_WIDTH, :]) for r in parts]
    h = [x_ref[r, :] + _rms(mix[n], n1_ref[...]) for n, r in enumerate(parts)]
    a = [_rms(h[n], n2_ref[...]).astype(BF16) for n in ns]
    f = [None for _ in ns]
    for c0 in range(0, D_FF, ff_chunk):
        up = [jnp.dot(a[n], wu_ref[:, c0:c0 + ff_chunk], preferred_element_type=F32)
              for n in ns]
        for n in ns:
            part = _dot(jnp.square(jnp.maximum(up[n], 0.0)), wd_ref[c0:c0 + ff_chunk, :])
            f[n] = part if f[n] is None else f[n] + part
    h = [h[n] + _rms(f[n], n3_ref[...]) for n in ns]
    gate = [_dot(h[n], wg_ref[...]) for n in ns]
    ple = [_dot(p_ref[r, :], wp_ref[...]) for r in parts]
    for n, r in enumerate(parts):
        o_ref[r, :] = h[n] + _sigmoid(gate[n]) * ple[n]


def _mlp(og, om, x, p, wo, n1, n2, wu, wd, n3, wg, wp, *, tm):
    B, T, D = x.shape
    tok = lambda width: pl.BlockSpec((None, tm, width), lambda b, i: (b, i, 0))
    const = lambda shape: pl.BlockSpec(shape, lambda b, i: (0,) * len(shape),
                                       pipeline_mode=pl.Buffered(1))
    return pl.pallas_call(
        functools.partial(_mlp_body, ff_chunk=1024),
        grid=(B, T // tm),
        in_specs=[tok(GDN_WIDTH), tok(MOBA_WIDTH), tok(D), tok(PLE_DIM),
                  const((D, D)), const((1, D)), const((1, D)), const((D, D_FF)),
                  const((D_FF, D)), const((1, D)), const((D, D)), const((PLE_DIM, D))],
        out_specs=tok(D),
        out_shape=jax.ShapeDtypeStruct((B, T, D), F32),
        compiler_params=pltpu.CompilerParams(
            dimension_semantics=("arbitrary", "arbitrary"), vmem_limit_bytes=VMEM_LIMIT),
        name="mlp",
    )(og, om, x, p, wo, n1, n2, wu, wd, n3, wg, wp)


def _rope_tables(T):
    half = ROPE_DIMS // 2
    inv_freq = ROPE_THETA ** (-np.arange(half, dtype=np.float64) * (2.0 / ROPE_DIMS))
    ang = np.arange(T, dtype=np.float64)[:, None] * inv_freq[None, :]
    cos, sin = np.cos(ang), np.sin(ang)
    zeros = np.zeros((T, HEAD_DIM - ROPE_DIMS))
    z_half = np.zeros((T, half))
    cos_t = np.concatenate([cos, cos, np.ones((T, HEAD_DIM - ROPE_DIMS))], axis=-1)
    sa_t = np.concatenate([-sin, z_half, zeros], axis=-1)
    sb_t = np.concatenate([z_half, sin, zeros], axis=-1)
    cos_t, sa_t, sb_t = (jnp.asarray(t, dtype=F32) for t in (cos_t, sa_t, sb_t))
    return cos_t, sa_t, sb_t


def _layer(h, p_i, w_in, layer, conv_w, a_log, dt_bias, gdn_norm_w, w_out, attn_pre_norm,
           attn_post_norm, mlp_pre_norm, mlp_post_norm, w_up, w_down, w_ple, w_ple_gate, tables):
    B, T, D = h.shape
    tm = next(t for t in (512, MOBA_BLOCK) if T % t == 0)
    assert T % tm == 0 and T % MOBA_BLOCK == 0 and T % GDN_CHUNK == 0
    nh = 2 * GDN_HEADS
    gparams = jnp.pad(jnp.stack([a_log.astype(F32), dt_bias.astype(F32)]),
                      ((0, 0), (GDN_HEADS, LANES - nh)))
    row = lambda v: v.reshape(1, -1).astype(F32)

    gq, gk, gv, gz, bg, mq, mk, mv = _in_proj(
        h, row(attn_pre_norm), jnp.swapaxes(w_in.astype(F32), 1, 2), layer, conv_w.astype(F32),
        gparams, *tables,
        tm=next(t for t in (1024, 512, MOBA_BLOCK) if T % t == 0))
    o_gdn, (wo, wu, wd, wg, wp) = _gdn(
        gq, gk, gv, gz, bg, row(gdn_norm_w),
        [w.astype(F32) for w in (w_out, w_up, w_down, w_ple_gate, w_ple)])
    o_moba = _moba(mq, mk, mv)
    return _mlp(o_gdn, o_moba, h, p_i, wo, row(attn_post_norm), row(mlp_pre_norm), wu, wd,
                row(mlp_post_norm), wg, wp,
                tm=next(t for t in (1024, 512, MOBA_BLOCK) if T % t == 0))


def kernel(x, p, w_in, conv_w, a_log, dt_bias, gdn_norm_w, w_out, attn_pre_norm, attn_post_norm,
           mlp_pre_norm, mlp_post_norm, w_up, w_down, w_ple, w_ple_gate):
    tables = _rope_tables(x.shape[1])
    h = x
    for i in range(w_in.shape[0]):
        h = _layer(h, p[i], w_in, i, conv_w[i], a_log[i], dt_bias[i], gdn_norm_w[i], w_out[i],
                   attn_pre_norm[i], attn_post_norm[i], mlp_pre_norm[i], mlp_post_norm[i],
                   w_up[i], w_down[i], w_ple[i], w_ple_gate[i], tables)
    return h
```
